```python
import math
import jax, jax.numpy as jnp
from jax import lax
import numpy as np

D_MODEL = 2048
BATCH = 4
SEQ = 2048
DEPTH = 2
DEC_BATCH = 128
DEC_SEQ = 4
PAST_LEN = 16384
PAGE_SIZE = 128

D_MIX = D_MODEL
D_RG = D_MIX // 2
D_SC = D_MIX - D_RG
RG_HEADS = 16
RG_HEAD_DIM = D_RG // RG_HEADS
RG_CONV_W = 4
RG_C = 8.0
SC_GROUPS = 16
SC_CONV_W = 3
D_IN = 2 * D_RG + 3 * D_SC
N_MEM = 256
XA_HEADS = 4
XA_HEAD_DIM = D_MODEL // XA_HEADS
N_GROUPS = 4
EXPERTS_PER_GROUP = 4
N_EXPERTS = N_GROUPS * EXPERTS_PER_GROUP
TOP_K = 2
D_FF = D_MODEL // 4
EPS = 1e-6

kernel_name = "hymba_rglru_shortconv_memxattn_hmoe_step"


def rmsnorm(x, g):
    xf = x.astype(jnp.float32)
    y = xf * lax.rsqrt(jnp.mean(xf * xf, axis=-1, keepdims=True) + EPS)
    return (y * g.astype(jnp.float32)).astype(x.dtype)


def causal_dwconv(x, buf, w):
    width = w.shape[0]
    t_len = x.shape[1]
    xf = jnp.concatenate([buf.astype(x.dtype), x], axis=1)
    y = xf[:, 0:t_len] * w[0]
    for k in range(1, width):
        y = y + xf[:, k:k + t_len] * w[k]
    return y, xf[:, -(width - 1):]


def rglru_scan(a, b, h0):
    def step(h, ab):
        a_t, b_t = ab
        h = a_t * h + b_t
        return h, h
    h_last, hs = lax.scan(step, h0, (jnp.swapaxes(a, 0, 1), jnp.swapaxes(b, 0, 1)))
    return jnp.swapaxes(hs, 0, 1), h_last


def mixing_block(h, conv_buf, h0, sc_buf, w_in, rg_conv_w, rg_conv_b, rg_w_a, rg_b_a,
                 rg_w_x, rg_b_x, rg_lambda, sc_conv_w, norm_rg_out, norm_sc_out, w_out):
    bsz, t_len, _ = h.shape
    z = h @ w_in
    offs = [D_RG, 2 * D_RG, 2 * D_RG + D_SC, 2 * D_RG + 2 * D_SC]
    rg_x, rg_gate, sc_b, sc_c, sc_x = jnp.split(z, offs, axis=-1)
    xc, conv_buf_new = causal_dwconv(rg_x, conv_buf, rg_conv_w)
    xc = xc + rg_conv_b
    xh = xc.reshape(bsz, t_len, RG_HEADS, RG_HEAD_DIM)
    r = jax.nn.sigmoid(jnp.einsum('bthi,hij->bthj', xh, rg_w_a).reshape(bsz, t_len, D_RG) + rg_b_a)
    i = jax.nn.sigmoid(jnp.einsum('bthi,hij->bthj', xh, rg_w_x).reshape(bsz, t_len, D_RG) + rg_b_x)
    log_a = -RG_C * r.astype(jnp.float32) * jax.nn.softplus(-rg_lambda.astype(jnp.float32))
    a = jnp.exp(log_a)
    mult = jnp.sqrt(-jnp.expm1(2.0 * log_a))
    hs, h_last = rglru_scan(a, mult * (i * xc).astype(jnp.float32), h0.astype(jnp.float32))
    rg_out = rmsnorm(hs.astype(h.dtype) * jax.nn.gelu(rg_gate), norm_rg_out)
    u = sc_c * sc_x
    uc, sc_buf_new = causal_dwconv(u, sc_buf, sc_conv_w)
    sc_out = rmsnorm(sc_b * uc, norm_sc_out)
    y = jnp.concatenate([rg_out, sc_out], axis=-1) @ w_out
    return y, conv_buf_new, h_last.astype(h0.dtype), sc_buf_new


def mem_kv(mem, g_mem, w_k, w_v):
    bsz, n_mem, _ = mem.shape
    m = rmsnorm(mem, g_mem)
    k = (m @ w_k).reshape(bsz, n_mem, XA_HEADS, XA_HEAD_DIM)
    v = (m @ w_v).reshape(bsz, n_mem, XA_HEADS, XA_HEAD_DIM)
    return k, v


def cross_attention(h, k, v, w_q, w_o):
    bsz, t_len, _ = h.shape
    q = (h @ w_q).reshape(bsz, t_len, XA_HEADS, XA_HEAD_DIM)
    s = jnp.einsum('bthd,bmhd->bhtm', q, k.astype(h.dtype)).astype(jnp.float32) / math.sqrt(XA_HEAD_DIM)
    p = jax.nn.softmax(s, axis=-1).astype(h.dtype)
    o = jnp.einsum('bhtm,bmhd->bthd', p, v.astype(h.dtype)).reshape(bsz, t_len, D_MODEL)
    return o @ w_o


def hier_moe(h, wg, bg, we, be, w_gate, w_up, w_down):
    bsz, t_len, d = h.shape
    t = h.reshape(-1, d)
    p_g = jax.nn.softmax((t @ wg + bg).astype(jnp.float32), axis=-1)
    g_sel = jnp.argmax(p_g, axis=-1)
    pg_sel = jnp.take_along_axis(p_g, g_sel[:, None], axis=-1)
    e_logits = (t @ we + be).astype(jnp.float32).reshape(-1, N_GROUPS, EXPERTS_PER_GROUP)
    e_sel = jnp.take_along_axis(e_logits, g_sel[:, None, None], axis=1)[:, 0]
    top_p, top_i = lax.top_k(jax.nn.softmax(e_sel, axis=-1), TOP_K)
    wts = top_p / jnp.sum(top_p, axis=-1, keepdims=True) * pg_sel
    expert_idx = g_sel[:, None] * EXPERTS_PER_GROUP + top_i
    gate = jnp.sum(jax.nn.one_hot(expert_idx, N_EXPERTS, dtype=jnp.float32) * wts[..., None], axis=1)
    a = jnp.einsum('nd,edf->nef', t, w_gate)
    u = jnp.einsum('nd,edf->nef', t, w_up)
    act = jax.nn.silu(a) * u * gate[..., None].astype(h.dtype)
    out = jnp.einsum('nef,efd->nd', act, w_down)
    return out.reshape(bsz, t_len, d)


def setup_inputs(seed: int = 0) -> dict:
    key = jax.random.key(seed)
    ks = iter(jax.random.split(key, 48))
    f32 = jnp.float32

    def nrm(shape, scale):
        return jax.random.normal(next(ks), shape, f32) * scale

    def gain(shape):
        return 1.0 + 0.05 * jax.random.normal(next(ks), shape, f32)

    s = jax.random.uniform(next(ks), (DEPTH, D_RG), f32, 0.9, 0.999) ** (1.0 / RG_C)
    rg_lambda = jnp.log(s) - jnp.log1p(-s)
    return {
        'x_prompt': nrm((BATCH, SEQ, D_MODEL), 1.0),
        'x_sample': nrm((DEC_BATCH, DEC_SEQ, D_MODEL), 1.0),
        'state_rglru_h': nrm((DEPTH, DEC_BATCH, D_RG), 0.5),
        'state_rglru_conv': nrm((DEPTH, DEC_BATCH, RG_CONV_W - 1, D_RG), 1.0),
        'state_sconv': nrm((DEPTH, DEC_BATCH, SC_CONV_W - 1, D_SC), 1.0),
        'cache_mem_k': nrm((DEPTH, DEC_BATCH, N_MEM, XA_HEADS, XA_HEAD_DIM), 1.0),
        'cache_mem_v': nrm((DEPTH, DEC_BATCH, N_MEM, XA_HEADS, XA_HEAD_DIM), 1.0),
        'mem_prompt': nrm((BATCH, N_MEM, D_MODEL), 1.0),
        'norm_mix': gain((DEPTH, D_MODEL)),
        'w_in': nrm((DEPTH, D_MODEL, D_IN), D_MODEL ** -0.5),
        'rg_conv_w': nrm((DEPTH, RG_CONV_W, D_RG), RG_CONV_W ** -0.5),
        'rg_conv_b': nrm((DEPTH, D_RG), 0.02),
        'rg_w_a': nrm((DEPTH, RG_HEADS, RG_HEAD_DIM, RG_HEAD_DIM), RG_HEAD_DIM ** -0.5),
        'rg_b_a': nrm((DEPTH, D_RG), 0.02),
        'rg_w_x': nrm((DEPTH, RG_HEADS, RG_HEAD_DIM, RG_HEAD_DIM), RG_HEAD_DIM ** -0.5),
        'rg_b_x': nrm((DEPTH, D_RG), 0.02),
        'rg_lambda': rg_lambda,
        'sc_conv_w': nrm((DEPTH, SC_CONV_W, D_SC), SC_CONV_W ** -0.5),
        'norm_rg_out': gain((DEPTH, D_RG)),
        'norm_sc_out': gain((DEPTH, D_SC)),
        'w_out': nrm((DEPTH, D_MIX, D_MODEL), D_MIX ** -0.5),
        'norm_xattn': gain((DEPTH, D_MODEL)),
        'norm_mem': gain((DEPTH, D_MODEL)),
        'xa_w_q': nrm((DEPTH, D_MODEL, D_MODEL), D_MODEL ** -0.5),
        'xa_w_k': nrm((DEPTH, D_MODEL, D_MODEL), D_MODEL ** -0.5),
        'xa_w_v': nrm((DEPTH, D_MODEL, D_MODEL), D_MODEL ** -0.5),
        'xa_w_o': nrm((DEPTH, D_MODEL, D_MODEL), D_MODEL ** -0.5),
        'norm_moe': gain((DEPTH, D_MODEL)),
        'router_group_w': nrm((DEPTH, D_MODEL, N_GROUPS), D_MODEL ** -0.5),
        'router_group_b': nrm((DEPTH, N_GROUPS), 0.01),
        'router_expert_w': nrm((DEPTH, D_MODEL, N_EXPERTS), D_MODEL ** -0.5),
        'router_expert_b': nrm((DEPTH, N_EXPERTS), 0.01),
        'expert_w_gate': nrm((DEPTH, N_EXPERTS, D_MODEL, D_FF), D_MODEL ** -0.5),
        'expert_w_up': nrm((DEPTH, N_EXPERTS, D_MODEL, D_FF), D_MODEL ** -0.5),
        'expert_w_down': nrm((DEPTH, N_EXPERTS, D_FF, D_MODEL), D_FF ** -0.5),
        'norm_final': gain((D_MODEL,)),
    }


def reference(x_prompt, x_sample, state_rglru_h, state_rglru_conv, state_sconv, cache_mem_k,
              cache_mem_v, mem_prompt, norm_mix, w_in, rg_conv_w, rg_conv_b, rg_w_a, rg_b_a,
              rg_w_x, rg_b_x, rg_lambda, sc_conv_w, norm_rg_out, norm_sc_out, w_out, norm_xattn,
              norm_mem, xa_w_q, xa_w_k, xa_w_v, xa_w_o, norm_moe, router_group_w, router_group_b,
              router_expert_w, router_expert_b, expert_w_gate, expert_w_up, expert_w_down,
              norm_final):

    def trunk(x, conv_state, h_state, sc_state, mem_keys, mem_values):
        new_conv, new_h, new_sc = [], [], []
        for l in range(DEPTH):
            y, c, hh, sc = mixing_block(
                rmsnorm(x, norm_mix[l]), conv_state[l], h_state[l], sc_state[l], w_in[l],
                rg_conv_w[l], rg_conv_b[l], rg_w_a[l], rg_b_a[l], rg_w_x[l], rg_b_x[l],
                rg_lambda[l], sc_conv_w[l], norm_rg_out[l], norm_sc_out[l], w_out[l])
            x = x + y
            x = x + cross_attention(rmsnorm(x, norm_xattn[l]), mem_keys[l], mem_values[l],
                                    xa_w_q[l], xa_w_o[l])
            x = x + hier_moe(rmsnorm(x, norm_moe[l]), router_group_w[l], router_group_b[l],
                             router_expert_w[l], router_expert_b[l], expert_w_gate[l],
                             expert_w_up[l], expert_w_down[l])
            new_conv.append(c)
            new_h.append(hh)
            new_sc.append(sc)
        return (rmsnorm(x, norm_final), jnp.stack(new_conv), jnp.stack(new_h), jnp.stack(new_sc))

    bp = x_prompt.shape[0]
    dt = x_prompt.dtype
    p_conv0 = jnp.zeros((DEPTH, bp, RG_CONV_W - 1, D_RG), dt)
    p_h0 = jnp.zeros((DEPTH, bp, D_RG), dt)
    p_sc0 = jnp.zeros((DEPTH, bp, SC_CONV_W - 1, D_SC), dt)
    kv = [mem_kv(mem_prompt, norm_mem[l], xa_w_k[l], xa_w_v[l]) for l in range(DEPTH)]
    p_mem_k = jnp.stack([kv[l][0] for l in range(DEPTH)])
    p_mem_v = jnp.stack([kv[l][1] for l in range(DEPTH)])
    y_prompt, p_conv, p_h, p_sc = trunk(x_prompt, p_conv0, p_h0, p_sc0, p_mem_k, p_mem_v)

    y_sample, s_conv, s_h, s_sc = trunk(x_sample, state_rglru_conv, state_rglru_h, state_sconv,
                                        cache_mem_k, cache_mem_v)
    return (y_prompt, y_sample, p_h, p_conv, p_sc, p_mem_k, p_mem_v, s_h, s_conv, s_sc)
```

```python
import functools
import math

import jax
import jax.numpy as jnp
from jax import lax
from jax.experimental import pallas as pl
from jax.experimental.pallas import tpu as pltpu

D_MODEL = 2048
BATCH = 4
SEQ = 2048
DEPTH = 2
DEC_BATCH = 128
DEC_SEQ = 4
D_RG = 1024
D_SC = 1024
RG_HEADS = 16
RG_HEAD_DIM = 64
RG_CONV_W = 4
RG_C = 8.0
SC_CONV_W = 3
D_IN = 2 * D_RG + 3 * D_SC
N_MEM = 256
XA_HEADS = 4
XA_HEAD_DIM = 512
N_GROUPS = 4
EXPERTS_PER_GROUP = 4
N_EXPERTS = 16
D_FF = 512
EPS = 1e-6

N_PROMPT = BATCH * SEQ
N_SAMPLE = DEC_BATCH * DEC_SEQ
N_TOK = N_PROMPT + N_SAMPLE

V7X_VMEM_LIMIT_BYTES = 56 * 1024 * 1024
SUBLANES = 8
LANES = 128
MXU_DIM = 256

HEADS_PER_GATE_BLOCK = MXU_DIM // RG_HEAD_DIM
N_GATE_BLOCKS = D_RG // MXU_DIM

TM_MIX = 256
TM_ATT = 512
TM_MOE = 512
SAMPLE_BLOCK = N_PROMPT // N_SAMPLE
ATT_BB = 8
ROUTER_LANES = 128

BF16 = jnp.bfloat16
F32 = jnp.float32


def _params(sem, vmem=V7X_VMEM_LIMIT_BYTES):
    return pltpu.CompilerParams(dimension_semantics=sem, vmem_limit_bytes=vmem)


def _resident(shape):
    nd = len(shape)
    return pl.BlockSpec(shape, lambda *_: (0,) * nd, pipeline_mode=pl.Buffered(1))


def _rms(x, g):
    return x * lax.rsqrt(jnp.mean(x * x, axis=-1, keepdims=True) + EPS) * g


def _dot(a, b):
    return jnp.dot(a, b, preferred_element_type=F32)


def _rg_gate_inputs(xc, wg_ref, rgv_ref):
    xcb = xc.astype(BF16)
    r_parts, i_parts = [], []
    for q in range(N_GATE_BLOCKS):
        g = _dot(xcb[:, q * MXU_DIM:(q + 1) * MXU_DIM], wg_ref[q])
        r_parts.append(g[:, :MXU_DIM])
        i_parts.append(g[:, MXU_DIM:])
    r = jax.nn.sigmoid(jnp.concatenate(r_parts, axis=1) + rgv_ref[5:6, :])
    i = jax.nn.sigmoid(jnp.concatenate(i_parts, axis=1) + rgv_ref[6:7, :])
    log_a = -RG_C * r * jax.nn.softplus(-rgv_ref[7:8, :])
    a = jnp.exp(log_a)
    mult = jnp.sqrt(1.0 - a * a)
    return a, mult * (i * xc)


def _mix_out(hs, rg_gate, sc_b, uc, scv_ref, w_out_ref):
    rg_out = _rms(hs * jax.nn.gelu(rg_gate), scv_ref[3:4, :]).astype(BF16)
    sc_out = _rms(sc_b * uc, scv_ref[4:5, :]).astype(BF16)
    return _dot(rg_out, w_out_ref[0:D_RG, :]) + _dot(sc_out, w_out_ref[D_RG:, :])


def _shift_rows(x, d, fill):
    m = x.shape[0]
    if d % SUBLANES == 0:
        head = jnp.full((d, x.shape[1]), fill, x.dtype)
        return jnp.concatenate([head, x[:m - d]], axis=0)
    rolled = pltpu.roll(x, d, axis=0)
    row = lax.broadcasted_iota(jnp.int32, x.shape, 0)
    return jnp.where(row >= d, rolled, fill)


def _scan_rows(a, b, h0):
    m = a.shape[0]
    d = 1
    while d < m:
        a_sh = _shift_rows(a, d, 1.0)
        b_sh = _shift_rows(b, d, 0.0)
        b = a * b_sh + b
        a = a * a_sh
        d *= 2
    return a * h0 + b


def _mixer_prompt_kernel(x_ref, gmix_ref, w_in_ref, wg_ref, rgv_ref, scv_ref, w_out_ref,
                         o_ref, conv_ref, h_ref, sc_ref, rgx, usc, hcar):
    t = pl.program_id(1)
    tm = x_ref.shape[0]

    @pl.when(t == 0)
    def _():
        rgx[0:SUBLANES, :] = jnp.zeros((SUBLANES, D_RG), F32)
        usc[0:SUBLANES, :] = jnp.zeros((SUBLANES, D_SC), F32)
        hcar[...] = jnp.zeros_like(hcar)

    x = x_ref[...]
    xn = _rms(x, gmix_ref[...]).astype(BF16)

    rg_x = _dot(xn, w_in_ref[:, 0:D_RG])
    rgx[SUBLANES:SUBLANES + tm, :] = rg_x
    xc = rgx[5:5 + tm, :] * rgv_ref[0:1, :]
    xc = xc + rgx[6:6 + tm, :] * rgv_ref[1:2, :]
    xc = xc + rgx[7:7 + tm, :] * rgv_ref[2:3, :]
    xc = xc + rg_x * rgv_ref[3:4, :]
    xc = xc + rgv_ref[4:5, :]
    a, b = _rg_gate_inputs(xc, wg_ref, rgv_ref)
    hs = _scan_rows(a, b, hcar[0:1, :])
    hcar[0:1, :] = hs[tm - 1:tm, :]

    rg_gate = _dot(xn, w_in_ref[:, D_RG:2 * D_RG])
    sc_b = _dot(xn, w_in_ref[:, 2 * D_RG:2 * D_RG + D_SC])
    sc_c = _dot(xn, w_in_ref[:, 2 * D_RG + D_SC:2 * D_RG + 2 * D_SC])
    sc_x = _dot(xn, w_in_ref[:, 2 * D_RG + 2 * D_SC:])
    u = sc_c * sc_x
    usc[SUBLANES:SUBLANES + tm, :] = u
    uc = usc[6:6 + tm, :] * scv_ref[0:1, :]
    uc = uc + usc[7:7 + tm, :] * scv_ref[1:2, :]
    uc = uc + u * scv_ref[2:3, :]

    o_ref[...] = x + _mix_out(hs, rg_gate, sc_b, uc, scv_ref, w_out_ref)

    @pl.when(t == pl.num_programs(1) - 1)
    def _():
        conv_ref[0] = rgx[tm + 5:tm + 8, :]
        sc_ref[0] = usc[tm + 6:tm + 8, :]
        h_ref[0] = hs[tm - 1:tm, :]

    rgx[0:SUBLANES, :] = rgx[tm:tm + SUBLANES, :]
    usc[0:SUBLANES, :] = usc[tm:tm + SUBLANES, :]


def _mixer_prompt(x, lw):
    nt = SEQ // TM_MIX
    return pl.pallas_call(
        _mixer_prompt_kernel,
        grid=(BATCH, nt),
        in_specs=[
            pl.BlockSpec((TM_MIX, D_MODEL), lambda b, t: (b * nt + t, 0)),
            _resident((1, D_MODEL)),
            _resident((D_MODEL, D_IN)),
            _resident((N_GATE_BLOCKS, MXU_DIM, 2 * MXU_DIM)),
            _resident((8, D_RG)),
            _resident((8, D_SC)),
            _resident((D_MODEL, D_MODEL)),
        ],
        out_specs=[
            pl.BlockSpec((TM_MIX, D_MODEL), lambda b, t: (b * nt + t, 0)),
            pl.BlockSpec((1, RG_CONV_W - 1, D_RG), lambda b, t: (b, 0, 0)),
            pl.BlockSpec((1, 1, D_RG), lambda b, t: (b, 0, 0)),
            pl.BlockSpec((1, SC_CONV_W - 1, D_SC), lambda b, t: (b, 0, 0)),
        ],
        out_shape=[
            jax.ShapeDtypeStruct((N_TOK, D_MODEL), F32),
            jax.ShapeDtypeStruct((BATCH, RG_CONV_W - 1, D_RG), F32),
            jax.ShapeDtypeStruct((BATCH, 1, D_RG), F32),
            jax.ShapeDtypeStruct((BATCH, SC_CONV_W - 1, D_SC), F32),
        ],
        scratch_shapes=[
            pltpu.VMEM((SUBLANES + TM_MIX, D_RG), F32),
            pltpu.VMEM((SUBLANES + TM_MIX, D_SC), F32),
            pltpu.VMEM((SUBLANES, D_RG), F32),
        ],
        compiler_params=_params(("arbitrary", "arbitrary")),
        name="mixer_prompt",
    )(x, lw["gmix"], lw["w_in"], lw["wg"], lw["rgv"], lw["scv"], lw["w_out"])


def _mixer_sample_kernel(x_ref, xall_ref, gmix_ref, w_in_ref, wg_ref, rgv_ref, scv_ref, w_out_ref,
                         conv_in_ref, h_in_ref, sc_in_ref,
                         o_ref, conv_ref, h_ref, sc_ref, xcs, hss, ucs):
    del xall_ref
    nb = DEC_BATCH
    x = x_ref[...]
    xn = _rms(x, gmix_ref[...]).astype(BF16)

    rg_x = _dot(xn, w_in_ref[:, 0:D_RG])
    seq = [conv_in_ref[:, k * D_RG:(k + 1) * D_RG] for k in range(RG_CONV_W - 1)]
    seq += [rg_x[t * nb:(t + 1) * nb, :] for t in range(DEC_SEQ)]
    for t in range(DEC_SEQ):
        xc_t = seq[t] * rgv_ref[0:1, :]
        for k in range(1, RG_CONV_W):
            xc_t = xc_t + seq[t + k] * rgv_ref[k:k + 1, :]
        xcs[t * nb:(t + 1) * nb, :] = xc_t + rgv_ref[4:5, :]
    for k in range(RG_CONV_W - 1):
        conv_ref[:, k * D_RG:(k + 1) * D_RG] = seq[DEC_SEQ + k]

    a, b = _rg_gate_inputs(xcs[...], wg_ref, rgv_ref)
    h = h_in_ref[...]
    for t in range(DEC_SEQ):
        h = a[t * nb:(t + 1) * nb, :] * h + b[t * nb:(t + 1) * nb, :]
        hss[t * nb:(t + 1) * nb, :] = h
    h_ref[...] = h

    rg_gate = _dot(xn, w_in_ref[:, D_RG:2 * D_RG])
    sc_b = _dot(xn, w_in_ref[:, 2 * D_RG:2 * D_RG + D_SC])
    sc_c = _dot(xn, w_in_ref[:, 2 * D_RG + D_SC:2 * D_RG + 2 * D_SC])
    sc_x = _dot(xn, w_in_ref[:, 2 * D_RG + 2 * D_SC:])
    u = sc_c * sc_x
    useq = [sc_in_ref[:, k * D_SC:(k + 1) * D_SC] for k in range(SC_CONV_W - 1)]
    useq += [u[t * nb:(t + 1) * nb, :] for t in range(DEC_SEQ)]
    for t in range(DEC_SEQ):
        uc_t = useq[t] * scv_ref[0:1, :]
        for k in range(1, SC_CONV_W):
            uc_t = uc_t + useq[t + k] * scv_ref[k:k + 1, :]
        ucs[t * nb:(t + 1) * nb, :] = uc_t
    for k in range(SC_CONV_W - 1):
        sc_ref[:, k * D_SC:(k + 1) * D_SC] = useq[DEC_SEQ + k]

    o_ref[...] = x + _mix_out(hss[...], rg_gate, sc_b, ucs[...], scv_ref, w_out_ref)


def _mixer_sample(x_src, src_block, x_all, lw, conv_in, h_in, sc_in):
    return pl.pallas_call(
        _mixer_sample_kernel,
        grid=(1,),
        in_specs=[
            pl.BlockSpec((N_SAMPLE, D_MODEL), lambda i: (src_block, 0)),
            pl.BlockSpec(memory_space=pl.ANY),
            _resident((1, D_MODEL)),
            _resident((D_MODEL, D_IN)),
            _resident((N_GATE_BLOCKS, MXU_DIM, 2 * MXU_DIM)),
            _resident((8, D_RG)),
            _resident((8, D_SC)),
            _resident((D_MODEL, D_MODEL)),
            _resident((DEC_BATCH, (RG_CONV_W - 1) * D_RG)),
            _resident((DEC_BATCH, D_RG)),
            _resident((DEC_BATCH, (SC_CONV_W - 1) * D_SC)),
        ],
        out_specs=[
            pl.BlockSpec((N_SAMPLE, D_MODEL), lambda i: (SAMPLE_BLOCK, 0)),
            pl.BlockSpec((DEC_BATCH, (RG_CONV_W - 1) * D_RG), lambda i: (0, 0)),
            pl.BlockSpec((DEC_BATCH, D_RG), lambda i: (0, 0)),
            pl.BlockSpec((DEC_BATCH, (SC_CONV_W - 1) * D_SC), lambda i: (0, 0)),
        ],
        out_shape=[
            jax.ShapeDtypeStruct((N_TOK, D_MODEL), F32),
            jax.ShapeDtypeStruct((DEC_BATCH, (RG_CONV_W - 1) * D_RG), F32),
            jax.ShapeDtypeStruct((DEC_BATCH, D_RG), F32),
            jax.ShapeDtypeStruct((DEC_BATCH, (SC_CONV_W - 1) * D_SC), F32),
        ],
        scratch_shapes=[
            pltpu.VMEM((N_SAMPLE, D_RG), F32),
            pltpu.VMEM((N_SAMPLE, D_RG), F32),
            pltpu.VMEM((N_SAMPLE, D_SC), F32),
        ],
        input_output_aliases={1: 0},
        compiler_params=_params(("arbitrary",)),
        name="mixer_sample",
    )(x_src, x_all, lw["gmix"], lw["w_in"], lw["wg"], lw["rgv"], lw["scv"], lw["w_out"],
      conv_in, h_in, sc_in)


def _norm_matmul_kernel(x_ref, g_ref, w_ref, o_ref):
    xn = _rms(x_ref[...], g_ref[...]).astype(BF16)
    o_ref[...] = _dot(xn, w_ref[...]).astype(o_ref.dtype)


def _norm_matmul(x, row_block0, n_rows, g, w, tm, tn, out_dtype, name):
    n = w.shape[1]
    return pl.pallas_call(
        _norm_matmul_kernel,
        grid=(n_rows // tm, n // tn),
        in_specs=[
            pl.BlockSpec((tm, D_MODEL), lambda i, j: (row_block0 + i, 0)),
            pl.BlockSpec((1, D_MODEL), lambda i, j: (0, 0)),
            pl.BlockSpec((D_MODEL, tn), lambda i, j: (0, j)),
        ],
        out_specs=pl.BlockSpec((tm, tn), lambda i, j: (i, j)),
        out_shape=jax.ShapeDtypeStruct((n_rows, n), out_dtype),
        compiler_params=_params(("arbitrary", "arbitrary")),
        name=name,
    )(x, g, w)


def _oproj_sample_kernel(a_ref, w_ref, res_ref, xall_ref, o_ref):
    del xall_ref
    o_ref[...] = res_ref[...] + _dot(a_ref[...].astype(BF16), w_ref[...])


def _oproj_sample(a, w, res_all, x_all, tn=1024):
    return pl.pallas_call(
        _oproj_sample_kernel,
        grid=(D_MODEL // tn,),
        in_specs=[
            pl.BlockSpec((N_SAMPLE, D_MODEL), lambda j: (0, 0)),
            pl.BlockSpec((D_MODEL, tn), lambda j: (0, j)),
            pl.BlockSpec((N_SAMPLE, tn), lambda j: (SAMPLE_BLOCK, j)),
            pl.BlockSpec(memory_space=pl.ANY),
        ],
        out_specs=pl.BlockSpec((N_SAMPLE, tn), lambda j: (SAMPLE_BLOCK, j)),
        out_shape=jax.ShapeDtypeStruct((N_TOK, D_MODEL), F32),
        input_output_aliases={3: 0},
        compiler_params=_params(("arbitrary",)),
        name="oproj_sample",
    )(a, w, res_all, x_all)


_INV_SQRT_HD = 1.0 / math.sqrt(XA_HEAD_DIM)


def _softmax_rows(s):
    m = jnp.max(s, axis=-1, keepdims=True)
    e = jnp.exp(s - m)
    return e / jnp.sum(e, axis=-1, keepdims=True)


def _attn_prompt_kernel(x_ref, g_ref, wq_ref, k_ref, v_ref, wo_ref, o_ref):
    x = x_ref[...]
    xn = _rms(x, g_ref[...]).astype(BF16)
    q = _dot(xn, wq_ref[...]).astype(BF16)
    k = k_ref[0].astype(BF16)
    v = v_ref[0].astype(BF16)
    outs = []
    for h in range(XA_HEADS):
        sl = slice(h * XA_HEAD_DIM, (h + 1) * XA_HEAD_DIM)
        s = lax.dot_general(q[:, sl], k[:, sl], (((1,), (1,)), ((), ())),
                            preferred_element_type=F32) / math.sqrt(XA_HEAD_DIM)
        p = _softmax_rows(s).astype(BF16)
        outs.append(_dot(p, v[:, sl]).astype(BF16))
    o = jnp.concatenate(outs, axis=1)
    o_ref[...] = x + _dot(o, wo_ref[...])


def _attn_prompt(x, g, wq, k, v, wo):
    nt = SEQ // TM_ATT
    return pl.pallas_call(
        _attn_prompt_kernel,
        grid=(BATCH, nt),
        in_specs=[
            pl.BlockSpec((TM_ATT, D_MODEL), lambda b, t: (b * nt + t, 0)),
            _resident((1, D_MODEL)),
            _resident((D_MODEL, D_MODEL)),
            pl.BlockSpec((1, N_MEM, D_MODEL), lambda b, t: (b, 0, 0)),
            pl.BlockSpec((1, N_MEM, D_MODEL), lambda b, t: (b, 0, 0)),
            _resident((D_MODEL, D_MODEL)),
        ],
        out_specs=pl.BlockSpec((TM_ATT, D_MODEL), lambda b, t: (b * nt + t, 0)),
        out_shape=jax.ShapeDtypeStruct((N_TOK, D_MODEL), F32),
        compiler_params=_params(("arbitrary", "arbitrary")),
        name="attn_prompt",
    )(x, g, wq, k, v, wo)


def _attn_sample_kernel(q_ref, k_ref, v_ref, o_ref):
    rows = DEC_SEQ * ATT_BB
    keys = ATT_BB * N_MEM
    q = q_ref[...].reshape(rows, XA_HEAD_DIM).astype(BF16)
    k = k_ref[...].reshape(keys, XA_HEAD_DIM).astype(BF16)
    v = v_ref[...].reshape(keys, XA_HEAD_DIM).astype(BF16)
    s = lax.dot_general(q, k, (((1,), (1,)), ((), ())),
                        preferred_element_type=F32) / math.sqrt(XA_HEAD_DIM)
    row_b = lax.broadcasted_iota(jnp.int32, (rows, keys), 0) % ATT_BB
    key_b = lax.broadcasted_iota(jnp.int32, (rows, keys), 1) // N_MEM
    own = row_b == key_b
    s = jnp.where(own, s, -jnp.inf)
    m = jnp.max(s, axis=-1, keepdims=True)
    e = jnp.where(own, jnp.exp(s - m), 0.0)
    p = (e / jnp.sum(e, axis=-1, keepdims=True)).astype(BF16)
    o_ref[...] = _dot(p, v).reshape(DEC_SEQ, ATT_BB, XA_HEAD_DIM)


def _attn_sample(q, k, v):
    return pl.pallas_call(
        _attn_sample_kernel,
        grid=(DEC_BATCH // ATT_BB, XA_HEADS),
        in_specs=[
            pl.BlockSpec((DEC_SEQ, ATT_BB, XA_HEAD_DIM), lambda i, h: (0, i, h)),
            pl.BlockSpec((ATT_BB, N_MEM, XA_HEAD_DIM), lambda i, h: (i, 0, h)),
            pl.BlockSpec((ATT_BB, N_MEM, XA_HEAD_DIM), lambda i, h: (i, 0, h)),
        ],
        out_specs=pl.BlockSpec((DEC_SEQ, ATT_BB, XA_HEAD_DIM), lambda i, h: (0, i, h)),
        out_shape=jax.ShapeDtypeStruct((DEC_SEQ, DEC_BATCH, D_MODEL), F32),
        compiler_params=_params(("arbitrary", "arbitrary")),
        name="attn_sample",
    )(q, k, v)


def _route(logits):
    lane = lax.broadcasted_iota(jnp.int32, logits.shape, 1)
    neg = -jnp.inf
    is_g = lane < N_GROUPS
    lg = jnp.where(is_g, logits, neg)
    mg = jnp.max(lg, axis=-1, keepdims=True)
    eg = jnp.where(is_g, jnp.exp(lg - mg), 0.0)
    p_g = eg / jnp.sum(eg, axis=-1, keepdims=True)
    pg_sel = jnp.max(p_g, axis=-1, keepdims=True)
    g_sel = jnp.min(jnp.where(p_g == pg_sel, lane, ROUTER_LANES), axis=-1, keepdims=True)

    lo = N_GROUPS + g_sel * EXPERTS_PER_GROUP
    in_grp = (lane >= lo) & (lane < lo + EXPERTS_PER_GROUP)
    le = jnp.where(in_grp, logits, neg)
    me = jnp.max(le, axis=-1, keepdims=True)
    ee = jnp.where(in_grp, jnp.exp(le - me), 0.0)
    p_e = ee / jnp.sum(ee, axis=-1, keepdims=True)

    p1 = jnp.max(p_e, axis=-1, keepdims=True)
    i1 = jnp.min(jnp.where(in_grp & (p_e == p1), lane, ROUTER_LANES), axis=-1, keepdims=True)
    rest = in_grp & (lane != i1)
    p2 = jnp.max(jnp.where(rest, p_e, neg), axis=-1, keepdims=True)
    i2 = jnp.min(jnp.where(rest & (p_e == p2), lane, ROUTER_LANES), axis=-1, keepdims=True)
    tot = p1 + p2
    w1 = p1 / tot * pg_sel
    w2 = p2 / tot * pg_sel
    return jnp.where(lane == i1, w1, 0.0) + jnp.where(lane == i2, w2, 0.0)


def _moe_kernel(x_ref, g_ref, wr_ref, br_ref, wgate_ref, wup_ref, wdown_ref, gfin_ref,
                o_ref, xn_s, gates_s, acc_s, *, final_norm):
    e = pl.program_id(1)

    @pl.when(e == 0)
    def _():
        xn = _rms(x_ref[...], g_ref[...])
        xn_s[...] = xn.astype(BF16)
        logits = jnp.dot(xn, wr_ref[...], preferred_element_type=F32,
                         precision=lax.Precision.HIGHEST) + br_ref[...]
        gates_s[...] = _route(logits)
        acc_s[...] = jnp.zeros_like(acc_s)

    xn = xn_s[...]
    a = _dot(xn, wgate_ref[0])
    u = _dot(xn, wup_ref[0])
    lane = lax.broadcasted_iota(jnp.int32, gates_s.shape, 1)
    gate = jnp.sum(jnp.where(lane == N_GROUPS + e, gates_s[...], 0.0), axis=-1, keepdims=True)
    act = (jax.nn.silu(a) * u * gate).astype(BF16)
    acc_s[...] += _dot(act, wdown_ref[0])

    @pl.when(e == N_EXPERTS - 1)
    def _():
        y = x_ref[...] + acc_s[...]
        if final_norm:
            y = _rms(y, gfin_ref[...])
        o_ref[...] = y


def _moe(x, lw, gfin, final_norm):
    return pl.pallas_call(
        functools.partial(_moe_kernel, final_norm=final_norm),
        grid=(N_TOK // TM_MOE, N_EXPERTS),
        in_specs=[
            pl.BlockSpec((TM_MOE, D_MODEL), lambda i, e: (i, 0)),
            pl.BlockSpec((1, D_MODEL), lambda i, e: (0, 0)),
            pl.BlockSpec((D_MODEL, ROUTER_LANES), lambda i, e: (0, 0)),
            pl.BlockSpec((1, ROUTER_LANES), lambda i, e: (0, 0)),
            pl.BlockSpec((1, D_MODEL, D_FF), lambda i, e: (e, 0, 0)),
            pl.BlockSpec((1, D_MODEL, D_FF), lambda i, e: (e, 0, 0)),
            pl.BlockSpec((1, D_FF, D_MODEL), lambda i, e: (e, 0, 0)),
            pl.BlockSpec((1, D_MODEL), lambda i, e: (0, 0)),
        ],
        out_specs=pl.BlockSpec((TM_MOE, D_MODEL), lambda i, e: (i, 0)),
        out_shape=jax.ShapeDtypeStruct((N_TOK, D_MODEL), F32),
        scratch_shapes=[
            pltpu.VMEM((TM_MOE, D_MODEL), BF16),
            pltpu.VMEM((TM_MOE, ROUTER_LANES), F32),
            pltpu.VMEM((TM_MOE, D_MODEL), F32),
        ],
        compiler_params=_params(("arbitrary", "arbitrary")),
        name="moe",
    )(x, lw["gmoe"], lw["wr"], lw["br"], lw["wgate"], lw["wup"], lw["wdown"], gfin)


def _block_diag_gates(w_a, w_x):
    def diag(w):
        w = w.reshape(N_GATE_BLOCKS, HEADS_PER_GATE_BLOCK, RG_HEAD_DIM, RG_HEAD_DIM)
        eye = jnp.eye(HEADS_PER_GATE_BLOCK, dtype=w.dtype)
        full = jnp.einsum("qhij,hk->qhikj", w, eye)
        return full.reshape(N_GATE_BLOCKS, MXU_DIM, MXU_DIM)
    return jnp.concatenate([diag(w_a), diag(w_x)], axis=-1).astype(BF16)


def _pad_rows(rows, n):
    width = rows[0].shape[-1]
    rows = [r.reshape(-1, width) for r in rows]
    have = sum(r.shape[0] for r in rows)
    if have < n:
        rows.append(jnp.zeros((n - have, width), F32))
    return jnp.concatenate(rows, axis=0)


def kernel(x_prompt, x_sample, state_rglru_h, state_rglru_conv, state_sconv, cache_mem_k,
           cache_mem_v, mem_prompt, norm_mix, w_in, rg_conv_w, rg_conv_b, rg_w_a, rg_b_a,
           rg_w_x, rg_b_x, rg_lambda, sc_conv_w, norm_rg_out, norm_sc_out, w_out, norm_xattn,
           norm_mem, xa_w_q, xa_w_k, xa_w_v, xa_w_o, norm_moe, router_group_w, router_group_b,
           router_expert_w, router_expert_b, expert_w_gate, expert_w_up, expert_w_down,
           norm_final):
    layers = []
    for l in range(DEPTH):
        wr = jnp.concatenate([router_group_w[l], router_expert_w[l]], axis=1)
        wr = jnp.pad(wr, ((0, 0), (0, ROUTER_LANES - wr.shape[1])))
        br = jnp.concatenate([router_group_b[l], router_expert_b[l]])
        br = jnp.pad(br, (0, ROUTER_LANES - br.shape[0])).reshape(1, ROUTER_LANES)
        layers.append(dict(
            gmix=norm_mix[l].reshape(1, D_MODEL),
            w_in=w_in[l].astype(BF16),
            wg=_block_diag_gates(rg_w_a[l], rg_w_x[l]),
            rgv=_pad_rows([rg_conv_w[l], rg_conv_b[l], rg_b_a[l], rg_b_x[l], rg_lambda[l]], 8),
            scv=_pad_rows([sc_conv_w[l], norm_rg_out[l], norm_sc_out[l]], 8),
            w_out=w_out[l].astype(BF16),
            gxa=norm_xattn[l].reshape(1, D_MODEL),
            gmem=norm_mem[l].reshape(1, D_MODEL),
            wq=xa_w_q[l].astype(BF16),
            wk=xa_w_k[l].astype(BF16),
            wv=xa_w_v[l].astype(BF16),
            wo=xa_w_o[l].astype(BF16),
            gmoe=norm_moe[l].reshape(1, D_MODEL),
            wr=wr, br=br,
            wgate=expert_w_gate[l].astype(BF16),
            wup=expert_w_up[l].astype(BF16),
            wdown=expert_w_down[l].astype(BF16),
        ))
    gfin = norm_final.reshape(1, D_MODEL)

    mem = mem_prompt.reshape(BATCH * N_MEM, D_MODEL)
    xs_tm = jnp.transpose(x_sample, (1, 0, 2)).reshape(N_SAMPLE, D_MODEL)
    x = x_prompt.reshape(N_PROMPT, D_MODEL)

    p_conv, p_h, p_sc, s_conv, s_h, s_sc, p_k, p_v = [], [], [], [], [], [], [], []
    for l, lw in enumerate(layers):
        k_l = _norm_matmul(mem, 0, BATCH * N_MEM, lw["gmem"], lw["wk"], 512, 1024, F32, "mem_k")
        v_l = _norm_matmul(mem, 0, BATCH * N_MEM, lw["gmem"], lw["wv"], 512, 1024, F32, "mem_v")
        p_k.append(k_l)
        p_v.append(v_l)

        x1, c, hh, sc = _mixer_prompt(x, lw)
        src, blk = (xs_tm, 0) if l == 0 else (x, SAMPLE_BLOCK)
        x1, cs, hs, scs = _mixer_sample(
            src, blk, x1, lw,
            state_rglru_conv[l].reshape(DEC_BATCH, (RG_CONV_W - 1) * D_RG),
            state_rglru_h[l],
            state_sconv[l].reshape(DEC_BATCH, (SC_CONV_W - 1) * D_SC))
        p_conv.append(c)
        p_h.append(hh.reshape(BATCH, D_RG))
        p_sc.append(sc)
        s_conv.append(cs.reshape(DEC_BATCH, RG_CONV_W - 1, D_RG))
        s_h.append(hs)
        s_sc.append(scs.reshape(DEC_BATCH, SC_CONV_W - 1, D_SC))

        x2 = _attn_prompt(x1, lw["gxa"], lw["wq"],
                          k_l.reshape(BATCH, N_MEM, D_MODEL), v_l.reshape(BATCH, N_MEM, D_MODEL),
                          lw["wo"])
        q_s = _norm_matmul(x1, SAMPLE_BLOCK, N_SAMPLE, lw["gxa"], lw["wq"], N_SAMPLE, 1024, F32,
                           "q_sample")
        o_s = _attn_sample(q_s.reshape(DEC_SEQ, DEC_BATCH, D_MODEL),
                           cache_mem_k[l].reshape(DEC_BATCH, N_MEM, D_MODEL),
                           cache_mem_v[l].reshape(DEC_BATCH, N_MEM, D_MODEL))
        x2 = _oproj_sample(o_s.reshape(N_SAMPLE, D_MODEL), lw["wo"], x1, x2)

        x = _moe(x2, lw, gfin, final_norm=(l == DEPTH - 1))

    y_prompt = x[:N_PROMPT].reshape(BATCH, SEQ, D_MODEL)
    y_sample = jnp.transpose(x[N_PROMPT:].reshape(DEC_SEQ, DEC_BATCH, D_MODEL), (1, 0, 2))
    mem_shape = (DEPTH, BATCH, N_MEM, XA_HEADS, XA_HEAD_DIM)
    return (y_prompt, y_sample,
            jnp.stack(p_h), jnp.stack(p_conv), jnp.stack(p_sc),
            jnp.stack(p_k).reshape(mem_shape), jnp.stack(p_v).reshape(mem_shape),
            jnp.stack(s_h), jnp.stack(s_conv), jnp.stack(s_sc))
```

```python
import functools
import math

import jax
import jax.numpy as jnp
from jax import lax
from jax.experimental import pallas as pl
from jax.experimental.pallas import tpu as pltpu

D_MODEL = 2048
BATCH = 4
SEQ = 2048
DEPTH = 2
DEC_BATCH = 128
DEC_SEQ = 4
D_RG = 1024
D_SC = 1024
RG_HEADS = 16
RG_HEAD_DIM = 64
RG_CONV_W = 4
RG_C = 8.0
SC_CONV_W = 3
D_IN = 2 * D_RG + 3 * D_SC
N_MEM = 256
XA_HEADS = 4
XA_HEAD_DIM = 512
N_GROUPS = 4
EXPERTS_PER_GROUP = 4
N_EXPERTS = 16
D_FF = 512
EPS = 1e-6

N_PROMPT = BATCH * SEQ
N_SAMPLE = DEC_BATCH * DEC_SEQ
N_TOK = N_PROMPT + N_SAMPLE

V7X_VMEM_LIMIT_BYTES = 56 * 1024 * 1024
SUBLANES = 8
LANES = 128
MXU_DIM = 256

HEADS_PER_GATE_BLOCK = MXU_DIM // RG_HEAD_DIM
N_GATE_BLOCKS = D_RG // MXU_DIM

TM_MIX = 256
TM_ATT = 512
TM_ROUTE = 512
ATT_BB = 8

PAIR_A = (0, 0, 0, 1, 2, 2)
PAIR_B = (1, 2, 3, 3, 3, 1)
N_PAIRS = len(PAIR_A)
N_BUCKETS = N_GROUPS * N_PAIRS
TG = 256
DMA_UNROLL = 8
N_TILES = -(-(N_TOK + N_BUCKETS * (TG - 1)) // TG)
N_SLOTS = N_TILES * TG
BUCKET_LANE = EXPERTS_PER_GROUP

BF16 = jnp.bfloat16
F32 = jnp.float32


def _params(sem, vmem=V7X_VMEM_LIMIT_BYTES):
    return pltpu.CompilerParams(dimension_semantics=sem, vmem_limit_bytes=vmem)


def _layer_resident(shape, l):
    nd = len(shape)
    return pl.BlockSpec((None,) + shape, lambda *_: (l,) + (0,) * nd, pipeline_mode=pl.Buffered(1))


def _layer_block(shape, l):
    nd = len(shape)
    return pl.BlockSpec((None,) + shape, lambda *_: (l,) + (0,) * nd)


def _rms(x, g):
    return x * lax.rsqrt(jnp.mean(x * x, axis=-1, keepdims=True) + EPS) * g


def _dot(a, b):
    return jnp.dot(a, b, preferred_element_type=F32)


def _rg_gate_inputs(xc, wg_ref, rgv_ref):
    xcb = xc.astype(BF16)
    r_parts, i_parts = [], []
    for q in range(N_GATE_BLOCKS):
        g = _dot(xcb[:, q * MXU_DIM:(q + 1) * MXU_DIM], wg_ref[q])
        r_parts.append(g[:, :MXU_DIM])
        i_parts.append(g[:, MXU_DIM:])
    r = jax.nn.sigmoid(jnp.concatenate(r_parts, axis=1) + rgv_ref[5:6, :])
    i = jax.nn.sigmoid(jnp.concatenate(i_parts, axis=1) + rgv_ref[6:7, :])
    log_a = -RG_C * r * jax.nn.softplus(-rgv_ref[7:8, :])
    a = jnp.exp(log_a)
    mult = jnp.sqrt(1.0 - a * a)
    return a, mult * (i * xc)


def _mix_out(hs, rg_gate, sc_b, uc, scv_ref, w_out_ref):
    rg_out = _rms(hs * jax.nn.gelu(rg_gate), scv_ref[3:4, :]).astype(BF16)
    sc_out = _rms(sc_b * uc, scv_ref[4:5, :]).astype(BF16)
    return _dot(rg_out, w_out_ref[0:D_RG, :]) + _dot(sc_out, w_out_ref[D_RG:, :])


def _shift_rows(x, d, fill):
    m = x.shape[0]
    if d % SUBLANES == 0:
        head = jnp.full((d, x.shape[1]), fill, x.dtype)
        return jnp.concatenate([head, x[:m - d]], axis=0)
    rolled = pltpu.roll(x, d, axis=0)
    row = lax.broadcasted_iota(jnp.int32, x.shape, 0)
    return jnp.where(row >= d, rolled, fill)


def _scan_rows(a, b, h0):
    m = a.shape[0]
    d = 1
    while d < m:
        a_sh = _shift_rows(a, d, 1.0)
        b_sh = _shift_rows(b, d, 0.0)
        b = a * b_sh + b
        a = a * a_sh
        d *= 2
    return a * h0 + b


def _mixer_prompt_kernel(x_ref, gmix_ref, w_in_ref, wg_ref, rgv_ref, scv_ref, w_out_ref,
                         o_ref, conv_ref, h_ref, sc_ref, rgx, usc, hcar):
    t = pl.program_id(1)
    tm = x_ref.shape[0]

    @pl.when(t == 0)
    def _():
        rgx[0:SUBLANES, :] = jnp.zeros((SUBLANES, D_RG), F32)
        usc[0:SUBLANES, :] = jnp.zeros((SUBLANES, D_SC), F32)
        hcar[...] = jnp.zeros_like(hcar)

    x = x_ref[...]
    xn = _rms(x, gmix_ref[...]).astype(BF16)

    rg_x = _dot(xn, w_in_ref[:, 0:D_RG])
    rgx[SUBLANES:SUBLANES + tm, :] = rg_x
    xc = rgx[5:5 + tm, :] * rgv_ref[0:1, :]
    xc = xc + rgx[6:6 + tm, :] * rgv_ref[1:2, :]
    xc = xc + rgx[7:7 + tm, :] * rgv_ref[2:3, :]
    xc = xc + rg_x * rgv_ref[3:4, :]
    xc = xc + rgv_ref[4:5, :]
    a, b = _rg_gate_inputs(xc, wg_ref, rgv_ref)
    hs = _scan_rows(a, b, hcar[0:1, :])
    hcar[0:1, :] = hs[tm - 1:tm, :]

    rg_gate = _dot(xn, w_in_ref[:, D_RG:2 * D_RG])
    sc_b = _dot(xn, w_in_ref[:, 2 * D_RG:2 * D_RG + D_SC])
    sc_c = _dot(xn, w_in_ref[:, 2 * D_RG + D_SC:2 * D_RG + 2 * D_SC])
    sc_x = _dot(xn, w_in_ref[:, 2 * D_RG + 2 * D_SC:])
    u = sc_c * sc_x
    usc[SUBLANES:SUBLANES + tm, :] = u
    uc = usc[6:6 + tm, :] * scv_ref[0:1, :]
    uc = uc + usc[7:7 + tm, :] * scv_ref[1:2, :]
    uc = uc + u * scv_ref[2:3, :]

    o_ref[...] = x + _mix_out(hs, rg_gate, sc_b, uc, scv_ref, w_out_ref)

    @pl.when(t == pl.num_programs(1) - 1)
    def _():
        conv_ref[0] = rgx[tm + 5:tm + 8, :]
        sc_ref[0] = usc[tm + 6:tm + 8, :]
        h_ref[0] = hs[tm - 1:tm, :]

    rgx[0:SUBLANES, :] = rgx[tm:tm + SUBLANES, :]
    usc[0:SUBLANES, :] = usc[tm:tm + SUBLANES, :]


def _mixer_weight_specs(l):
    return [
        _layer_resident((1, D_MODEL), l),
        _layer_resident((D_MODEL, D_IN), l),
        _layer_resident((N_GATE_BLOCKS, MXU_DIM, 2 * MXU_DIM), l),
        _layer_resident((8, D_RG), l),
        _layer_resident((8, D_SC), l),
        _layer_resident((D_MODEL, D_MODEL), l),
    ]


def _mixer_prompt(x, l, pw):
    nt = SEQ // TM_MIX
    return pl.pallas_call(
        _mixer_prompt_kernel,
        grid=(BATCH, nt),
        in_specs=[pl.BlockSpec((TM_MIX, D_MODEL), lambda b, t: (b * nt + t, 0))]
        + _mixer_weight_specs(l),
        out_specs=[
            pl.BlockSpec((TM_MIX, D_MODEL), lambda b, t: (b * nt + t, 0)),
            pl.BlockSpec((1, RG_CONV_W - 1, D_RG), lambda b, t: (b, 0, 0)),
            pl.BlockSpec((1, 1, D_RG), lambda b, t: (b, 0, 0)),
            pl.BlockSpec((1, SC_CONV_W - 1, D_SC), lambda b, t: (b, 0, 0)),
        ],
        out_shape=[
            jax.ShapeDtypeStruct((N_PROMPT, D_MODEL), F32),
            jax.ShapeDtypeStruct((BATCH, RG_CONV_W - 1, D_RG), F32),
            jax.ShapeDtypeStruct((BATCH, 1, D_RG), F32),
            jax.ShapeDtypeStruct((BATCH, SC_CONV_W - 1, D_SC), F32),
        ],
        scratch_shapes=[
            pltpu.VMEM((SUBLANES + TM_MIX, D_RG), F32),
            pltpu.VMEM((SUBLANES + TM_MIX, D_SC), F32),
            pltpu.VMEM((SUBLANES, D_RG), F32),
        ],
        compiler_params=_params(("arbitrary", "arbitrary")),
        name="mixer_prompt",
    )(x, pw["gmix"], pw["w_in"], pw["wg"], pw["rgv"], pw["scv"], pw["w_out"])


def _mixer_sample_kernel(x_ref, gmix_ref, w_in_ref, wg_ref, rgv_ref, scv_ref, w_out_ref,
                         conv_in_ref, h_in_ref, sc_in_ref,
                         o_ref, conv_ref, h_ref, sc_ref, xcs, hss, ucs):
    nb = DEC_BATCH
    x = x_ref[...]
    xn = _rms(x, gmix_ref[...]).astype(BF16)

    rg_x = _dot(xn, w_in_ref[:, 0:D_RG])
    seq = [conv_in_ref[:, k * D_RG:(k + 1) * D_RG] for k in range(RG_CONV_W - 1)]
    seq += [rg_x[t * nb:(t + 1) * nb, :] for t in range(DEC_SEQ)]
    for t in range(DEC_SEQ):
        xc_t = seq[t] * rgv_ref[0:1, :]
        for k in range(1, RG_CONV_W):
            xc_t = xc_t + seq[t + k] * rgv_ref[k:k + 1, :]
        xcs[t * nb:(t + 1) * nb, :] = xc_t + rgv_ref[4:5, :]
    for k in range(RG_CONV_W - 1):
        conv_ref[:, k * D_RG:(k + 1) * D_RG] = seq[DEC_SEQ + k]

    a, b = _rg_gate_inputs(xcs[...], wg_ref, rgv_ref)
    h = h_in_ref[...]
    for t in range(DEC_SEQ):
        h = a[t * nb:(t + 1) * nb, :] * h + b[t * nb:(t + 1) * nb, :]
        hss[t * nb:(t + 1) * nb, :] = h
    h_ref[...] = h

    rg_gate = _dot(xn, w_in_ref[:, D_RG:2 * D_RG])
    sc_b = _dot(xn, w_in_ref[:, 2 * D_RG:2 * D_RG + D_SC])
    sc_c = _dot(xn, w_in_ref[:, 2 * D_RG + D_SC:2 * D_RG + 2 * D_SC])
    sc_x = _dot(xn, w_in_ref[:, 2 * D_RG + 2 * D_SC:])
    u = sc_c * sc_x
    useq = [sc_in_ref[:, k * D_SC:(k + 1) * D_SC] for k in range(SC_CONV_W - 1)]
    useq += [u[t * nb:(t + 1) * nb, :] for t in range(DEC_SEQ)]
    for t in range(DEC_SEQ):
        uc_t = useq[t] * scv_ref[0:1, :]
        for k in range(1, SC_CONV_W):
            uc_t = uc_t + useq[t + k] * scv_ref[k:k + 1, :]
        ucs[t * nb:(t + 1) * nb, :] = uc_t
    for k in range(SC_CONV_W - 1):
        sc_ref[:, k * D_SC:(k + 1) * D_SC] = useq[DEC_SEQ + k]

    o_ref[...] = x + _mix_out(hss[...], rg_gate, sc_b, ucs[...], scv_ref, w_out_ref)


def _mixer_sample(x, l, pw, conv_in, h_in, sc_in):
    conv_w = (RG_CONV_W - 1) * D_RG
    sc_w = (SC_CONV_W - 1) * D_SC
    return pl.pallas_call(
        _mixer_sample_kernel,
        grid=(1,),
        in_specs=[
            pl.BlockSpec((N_SAMPLE, D_MODEL), lambda i: (0, 0)),
        ] + _mixer_weight_specs(l) + [
            _layer_resident((DEC_BATCH, conv_w), l),
            _layer_resident((DEC_BATCH, D_RG), l),
            _layer_resident((DEC_BATCH, sc_w), l),
        ],
        out_specs=[
            pl.BlockSpec((N_SAMPLE, D_MODEL), lambda i: (0, 0)),
            pl.BlockSpec((DEC_BATCH, conv_w), lambda i: (0, 0)),
            pl.BlockSpec((DEC_BATCH, D_RG), lambda i: (0, 0)),
            pl.BlockSpec((DEC_BATCH, sc_w), lambda i: (0, 0)),
        ],
        out_shape=[
            jax.ShapeDtypeStruct((N_SAMPLE, D_MODEL), F32),
            jax.ShapeDtypeStruct((DEC_BATCH, conv_w), F32),
            jax.ShapeDtypeStruct((DEC_BATCH, D_RG), F32),
            jax.ShapeDtypeStruct((DEC_BATCH, sc_w), F32),
        ],
        scratch_shapes=[
            pltpu.VMEM((N_SAMPLE, D_RG), F32),
            pltpu.VMEM((N_SAMPLE, D_RG), F32),
            pltpu.VMEM((N_SAMPLE, D_SC), F32),
        ],
        compiler_params=_params(("arbitrary",)),
        name="mixer_sample",
    )(x, pw["gmix"], pw["w_in"], pw["wg"], pw["rgv"], pw["scv"], pw["w_out"],
      conv_in, h_in, sc_in)


def _mem_kv_kernel(m_ref, g_ref, wk_ref, wv_ref, k_ref, v_ref):
    mn = _rms(m_ref[...], g_ref[...]).astype(BF16)
    k_ref[...] = _dot(mn, wk_ref[...])
    v_ref[...] = _dot(mn, wv_ref[...])


def _mem_kv(mem, pw, tm=512, tn=1024):
    rows = BATCH * N_MEM
    w_spec = pl.BlockSpec((None, D_MODEL, tn), lambda l, i, j: (l, 0, j))
    o_spec = pl.BlockSpec((None, tm, tn), lambda l, i, j: (l, i, j))
    shape = jax.ShapeDtypeStruct((DEPTH, rows, D_MODEL), F32)
    return pl.pallas_call(
        _mem_kv_kernel,
        grid=(DEPTH, rows // tm, D_MODEL // tn),
        in_specs=[
            pl.BlockSpec((tm, D_MODEL), lambda l, i, j: (i, 0)),
            pl.BlockSpec((None, 1, D_MODEL), lambda l, i, j: (l, 0, 0)),
            w_spec, w_spec,
        ],
        out_specs=[o_spec, o_spec],
        out_shape=[shape, shape],
        compiler_params=_params(("arbitrary",) * 3),
        name="mem_kv",
    )(mem, pw["gmem"], pw["wk"], pw["wv"])


def _q_sample_kernel(x_ref, g_ref, w_ref, o_ref):
    xn = _rms(x_ref[...], g_ref[...]).astype(BF16)
    o_ref[...] = _dot(xn, w_ref[...])


def _q_sample(x, l, pw, tn=1024):
    return pl.pallas_call(
        _q_sample_kernel,
        grid=(D_MODEL // tn,),
        in_specs=[
            pl.BlockSpec((N_SAMPLE, D_MODEL), lambda j: (0, 0)),
            _layer_block((1, D_MODEL), l),
            pl.BlockSpec((None, D_MODEL, tn), lambda j: (l, 0, j)),
        ],
        out_specs=pl.BlockSpec((N_SAMPLE, tn), lambda j: (0, j)),
        out_shape=jax.ShapeDtypeStruct((N_SAMPLE, D_MODEL), F32),
        compiler_params=_params(("arbitrary",)),
        name="q_sample",
    )(x, pw["gxa"], pw["wq"])


def _oproj_sample_kernel(a_ref, w_ref, res_ref, o_ref):
    o_ref[...] = res_ref[...] + _dot(a_ref[...].astype(BF16), w_ref[...])


def _oproj_sample(a, l, pw, res, tn=1024):
    return pl.pallas_call(
        _oproj_sample_kernel,
        grid=(D_MODEL // tn,),
        in_specs=[
            pl.BlockSpec((N_SAMPLE, D_MODEL), lambda j: (0, 0)),
            pl.BlockSpec((None, D_MODEL, tn), lambda j: (l, 0, j)),
            pl.BlockSpec((N_SAMPLE, tn), lambda j: (0, j)),
        ],
        out_specs=pl.BlockSpec((N_SAMPLE, tn), lambda j: (0, j)),
        out_shape=jax.ShapeDtypeStruct((N_SAMPLE, D_MODEL), F32),
        compiler_params=_params(("arbitrary",)),
        name="oproj_sample",
    )(a, pw["wo"], res)


def _softmax_rows(s):
    m = jnp.max(s, axis=-1, keepdims=True)
    e = jnp.exp(s - m)
    return e / jnp.sum(e, axis=-1, keepdims=True)


def _attn_prompt_kernel(x_ref, g_ref, wq_ref, k_ref, v_ref, wo_ref, o_ref):
    x = x_ref[...]
    xn = _rms(x, g_ref[...]).astype(BF16)
    q = _dot(xn, wq_ref[...]).astype(BF16)
    k = k_ref[...].astype(BF16)
    v = v_ref[...].astype(BF16)
    outs = []
    for h in range(XA_HEADS):
        sl = slice(h * XA_HEAD_DIM, (h + 1) * XA_HEAD_DIM)
        s = lax.dot_general(q[:, sl], k[:, sl], (((1,), (1,)), ((), ())),
                            preferred_element_type=F32) / math.sqrt(XA_HEAD_DIM)
        p = _softmax_rows(s).astype(BF16)
        outs.append(_dot(p, v[:, sl]).astype(BF16))
    o = jnp.concatenate(outs, axis=1)
    o_ref[...] = x + _dot(o, wo_ref[...])


def _attn_prompt(x, l, pw, k, v):
    nt = SEQ // TM_ATT
    kv_spec = pl.BlockSpec((None, N_MEM, D_MODEL), lambda b, t: (l, b, 0))
    return pl.pallas_call(
        _attn_prompt_kernel,
        grid=(BATCH, nt),
        in_specs=[
            pl.BlockSpec((TM_ATT, D_MODEL), lambda b, t: (b * nt + t, 0)),
            _layer_resident((1, D_MODEL), l),
            _layer_resident((D_MODEL, D_MODEL), l),
            kv_spec, kv_spec,
            _layer_resident((D_MODEL, D_MODEL), l),
        ],
        out_specs=pl.BlockSpec((TM_ATT, D_MODEL), lambda b, t: (b * nt + t, 0)),
        out_shape=jax.ShapeDtypeStruct((N_PROMPT, D_MODEL), F32),
        compiler_params=_params(("arbitrary", "arbitrary")),
        name="attn_prompt",
    )(x, pw["gxa"], pw["wq"], k, v, pw["wo"])


def _attn_sample_kernel(q_ref, k_ref, v_ref, o_ref):
    rows = DEC_SEQ * ATT_BB
    keys = ATT_BB * N_MEM
    q = q_ref[...].reshape(rows, XA_HEAD_DIM).astype(BF16)
    k = k_ref[...].reshape(keys, XA_HEAD_DIM).astype(BF16)
    v = v_ref[...].reshape(keys, XA_HEAD_DIM).astype(BF16)
    s = lax.dot_general(q, k, (((1,), (1,)), ((), ())),
                        preferred_element_type=F32) / math.sqrt(XA_HEAD_DIM)
    row_b = lax.broadcasted_iota(jnp.int32, (rows, keys), 0) % ATT_BB
    key_b = lax.broadcasted_iota(jnp.int32, (rows, keys), 1) // N_MEM
    own = row_b == key_b
    s = jnp.where(own, s, -jnp.inf)
    m = jnp.max(s, axis=-1, keepdims=True)
    e = jnp.where(own, jnp.exp(s - m), 0.0)
    p = (e / jnp.sum(e, axis=-1, keepdims=True)).astype(BF16)
    o_ref[...] = _dot(p, v).reshape(DEC_SEQ, ATT_BB, XA_HEAD_DIM)


def _attn_sample(q, l, k, v):
    nb = DEC_BATCH // ATT_BB
    kv_spec = pl.BlockSpec((ATT_BB, N_MEM, XA_HEAD_DIM), lambda i, h: (l * nb + i, 0, h))
    return pl.pallas_call(
        _attn_sample_kernel,
        grid=(nb, XA_HEADS),
        in_specs=[
            pl.BlockSpec((DEC_SEQ, ATT_BB, XA_HEAD_DIM), lambda i, h: (0, i, h)),
            kv_spec, kv_spec,
        ],
        out_specs=pl.BlockSpec((DEC_SEQ, ATT_BB, XA_HEAD_DIM), lambda i, h: (0, i, h)),
        out_shape=jax.ShapeDtypeStruct((DEC_SEQ, DEC_BATCH, D_MODEL), F32),
        compiler_params=_params(("arbitrary", "arbitrary")),
        name="attn_sample",
    )(q, k, v)


def _route(logits):
    lane = lax.broadcasted_iota(jnp.int32, logits.shape, 1)
    neg = -jnp.inf
    is_g = lane < N_GROUPS
    lg = jnp.where(is_g, logits, neg)
    mg = jnp.max(lg, axis=-1, keepdims=True)
    eg = jnp.where(is_g, jnp.exp(lg - mg), 0.0)
    p_g = eg / jnp.sum(eg, axis=-1, keepdims=True)
    pg_sel = jnp.max(p_g, axis=-1, keepdims=True)
    g_sel = jnp.min(jnp.where(p_g == pg_sel, lane, LANES), axis=-1, keepdims=True)

    lo = N_GROUPS + g_sel * EXPERTS_PER_GROUP
    in_grp = (lane >= lo) & (lane < lo + EXPERTS_PER_GROUP)
    le = jnp.where(in_grp, logits, neg)
    me = jnp.max(le, axis=-1, keepdims=True)
    ee = jnp.where(in_grp, jnp.exp(le - me), 0.0)
    p_e = ee / jnp.sum(ee, axis=-1, keepdims=True)

    p1 = jnp.max(p_e, axis=-1, keepdims=True)
    i1 = jnp.min(jnp.where(in_grp & (p_e == p1), lane, LANES), axis=-1, keepdims=True)
    rest = in_grp & (lane != i1)
    p2 = jnp.max(jnp.where(rest, p_e, neg), axis=-1, keepdims=True)
    i2 = jnp.min(jnp.where(rest & (p_e == p2), lane, LANES), axis=-1, keepdims=True)
    tot = p1 + p2
    w1 = p1 / tot * pg_sel
    w2 = p2 / tot * pg_sel
    gates = jnp.where(lane == i1, w1, 0.0) + jnp.where(lane == i2, w2, 0.0)

    local = jnp.zeros_like(gates)
    for g in range(N_GROUPS):
        start = N_GROUPS + g * EXPERTS_PER_GROUP
        local = local + jnp.where(lane < EXPERTS_PER_GROUP,
                                  pltpu.roll(gates, LANES - start, axis=1), 0.0)

    ja = jnp.minimum(i1, i2) - lo
    jb = jnp.maximum(i1, i2) - lo
    code = ja * EXPERTS_PER_GROUP + jb
    pair = jnp.full_like(code, N_PAIRS - 1)
    for p in range(N_PAIRS - 1):
        lo_j, hi_j = min(PAIR_A[p], PAIR_B[p]), max(PAIR_A[p], PAIR_B[p])
        pair = jnp.where(code == lo_j * EXPERTS_PER_GROUP + hi_j, p, pair)
    bucket = (g_sel * N_PAIRS + pair).astype(F32)
    return local + jnp.where(lane == BUCKET_LANE, bucket, 0.0)


def _router_kernel(x_ref, g_ref, wr_ref, br_ref, o_ref):
    xn = _rms(x_ref[...], g_ref[...])
    logits = jnp.dot(xn, wr_ref[...], preferred_element_type=F32,
                     precision=lax.Precision.HIGHEST) + br_ref[...]
    o_ref[...] = _route(logits)


def _router(x, l, pw):
    n_rows = x.shape[0]
    return pl.pallas_call(
        _router_kernel,
        grid=(n_rows // TM_ROUTE,),
        in_specs=[
            pl.BlockSpec((TM_ROUTE, D_MODEL), lambda i: (i, 0)),
            _layer_block((1, D_MODEL), l),
            _layer_block((D_MODEL, LANES), l),
            _layer_block((1, LANES), l),
        ],
        out_specs=pl.BlockSpec((TM_ROUTE, LANES), lambda i: (i, 0)),
        out_shape=jax.ShapeDtypeStruct((n_rows, LANES), F32),
        compiler_params=_params(("arbitrary",)),
        name="router",
    )(x, pw["gmoe"], pw["wr"], pw["br"])


def _dispatch_tables(rinfo):
    bucket = rinfo[:, BUCKET_LANE].astype(jnp.int32)
    onehot = (bucket[:, None] == jnp.arange(N_BUCKETS, dtype=jnp.int32)[None, :]).astype(jnp.int32)
    csum = jnp.cumsum(onehot, axis=0)
    counts = csum[-1]
    rank = jnp.take_along_axis(csum, bucket[:, None], axis=1)[:, 0] - 1
    tiles_per = (counts + TG - 1) // TG
    tile_end = jnp.cumsum(tiles_per)
    tile_start = tile_end - tiles_per
    slot = tile_start[bucket] * TG + rank
    slot_tok = jnp.full((N_SLOTS,), -1, jnp.int32).at[slot].set(
        jnp.arange(N_TOK, dtype=jnp.int32), unique_indices=True)

    tile = jnp.arange(N_TILES, dtype=jnp.int32)
    n_valid = tile_end[-1]
    tile_bucket = jnp.searchsorted(tile_end, jnp.minimum(tile, n_valid - 1), side="right")
    tile_bucket = tile_bucket.astype(jnp.int32)
    in_bucket = tile - tile_start[tile_bucket]
    tile_cnt = jnp.clip(counts[tile_bucket] - in_bucket * TG, 0, TG)
    tile_cnt = jnp.where(tile < n_valid, tile_cnt, 0).astype(jnp.int32)
    group, pair = tile_bucket // N_PAIRS, tile_bucket % N_PAIRS
    tile_ea = group * EXPERTS_PER_GROUP + jnp.asarray(PAIR_A, jnp.int32)[pair]
    tile_eb = group * EXPERTS_PER_GROUP + jnp.asarray(PAIR_B, jnp.int32)[pair]
    gates = rinfo[jnp.maximum(slot_tok, 0), :EXPERTS_PER_GROUP]
    is_prompt = (slot_tok >= 0) & (slot_tok < N_PROMPT)
    tile_np = jnp.sum(is_prompt.reshape(N_TILES, TG).astype(jnp.int32), axis=1)
    return tile_ea, tile_eb, tile_cnt, tile_np, slot_tok.reshape(N_TILES, 1, TG), gates


def _experts_kernel(ea_ref, eb_ref, cnt_ref, np_ref,
                    xp_hbm, xs_hbm, tokp_ref, tokc_ref, tokn_ref, gates_ref, g_ref,
                    wga_ref, wua_ref, wda_ref, wgb_ref, wub_ref, wdb_ref, gfin_ref,
                    op_hbm, os_hbm, xbuf, obuf, gsem, ssem, *, final_norm, sample_batch_major):
    t = pl.program_id(0)
    nt = pl.num_programs(0)
    buf = lax.rem(t, 2)
    t_prev = jnp.maximum(t - 1, 0)
    t_next = jnp.minimum(t + 1, nt - 1)
    cnt = cnt_ref[t]
    has_next = (t + 1 < nt) & (cnt_ref[t_next] > 0)

    def sample_dst(r):
        if not sample_batch_major:
            return r
        return (r % DEC_BATCH) * DEC_SEQ + r // DEC_BATCH

    def gather_p(tok_ref, b, r):
        return pltpu.make_async_copy(
            xp_hbm.at[pl.ds(tok_ref[0, r], 1)], xbuf.at[b, pl.ds(r, 1)], gsem.at[b])

    def gather_s(tok_ref, b, r):
        return pltpu.make_async_copy(
            xs_hbm.at[pl.ds(tok_ref[0, r] - N_PROMPT, 1)], xbuf.at[b, pl.ds(r, 1)], gsem.at[b])

    def scatter_p(tok_ref, b, r):
        return pltpu.make_async_copy(
            obuf.at[b, pl.ds(r, 1)], op_hbm.at[pl.ds(tok_ref[0, r], 1)], ssem.at[b])

    def scatter_s(tok_ref, b, r):
        dst = sample_dst(tok_ref[0, r] - N_PROMPT)
        return pltpu.make_async_copy(
            obuf.at[b, pl.ds(r, 1)], os_hbm.at[pl.ds(dst, 1)], ssem.at[b])

    def for_range(lo, hi, fn):
        full = (hi - lo) // DMA_UNROLL

        def group(i, c):
            for j in range(DMA_UNROLL):
                fn(lo + i * DMA_UNROLL + j)
            return c

        def single(r, c):
            fn(r)
            return c
        lax.fori_loop(0, full, group, 0)
        lax.fori_loop(lo + full * DMA_UNROLL, hi, single, 0)

    def gather(tile, tok_ref, b, op):
        for_range(0, np_ref[tile], lambda r: op(gather_p(tok_ref, b, r)))
        for_range(np_ref[tile], cnt_ref[tile], lambda r: op(gather_s(tok_ref, b, r)))

    def scatter(tile, tok_ref, b, op):
        for_range(0, np_ref[tile], lambda r: op(scatter_p(tok_ref, b, r)))
        for_range(np_ref[tile], cnt_ref[tile], lambda r: op(scatter_s(tok_ref, b, r)))

    def start(copy):
        copy.start()

    def wait(copy):
        copy.wait()

    @pl.when(t == 0)
    def _():
        xbuf[...] = jnp.zeros_like(xbuf)
        gather(t, tokc_ref, 0, start)

    @pl.when(has_next)
    def _():
        gather(t_next, tokn_ref, 1 - buf, start)

    @pl.when(cnt > 0)
    def _():
        gather(t, tokc_ref, buf, wait)

        x = xbuf[buf]
        xn = _rms(x, g_ref[...]).astype(BF16)
        gates = gates_ref[...]
        lane = lax.broadcasted_iota(jnp.int32, gates.shape, 1)
        ja = ea_ref[t] % EXPERTS_PER_GROUP
        jb = eb_ref[t] % EXPERTS_PER_GROUP
        ga = jnp.sum(jnp.where(lane == ja, gates, 0.0), axis=-1, keepdims=True)
        gb = jnp.sum(jnp.where(lane == jb, gates, 0.0), axis=-1, keepdims=True)
        act_a = (jax.nn.silu(_dot(xn, wga_ref[...])) * _dot(xn, wua_ref[...]) * ga).astype(BF16)
        act_b = (jax.nn.silu(_dot(xn, wgb_ref[...])) * _dot(xn, wub_ref[...]) * gb).astype(BF16)
        y = x + (_dot(act_a, wda_ref[...]) + _dot(act_b, wdb_ref[...]))
        if final_norm:
            y = _rms(y, gfin_ref[...])
        obuf[buf] = y

        @pl.when(t > 0)
        def _():
            scatter(t_prev, tokp_ref, 1 - buf, wait)

        scatter(t, tokc_ref, buf, start)

        @pl.when(jnp.logical_not(has_next))
        def _():
            scatter(t, tokc_ref, buf, wait)


def _experts(xp, xs, l, pw, gfin, tables, final_norm):
    tile_ea, tile_eb, tile_cnt, tile_np, slot_tok, gates = tables

    def tok_spec(shift):
        return pl.BlockSpec(
            (None, 1, TG),
            lambda t, *_: (jnp.clip(t + shift, 0, N_TILES - 1), 0, 0),
            memory_space=pltpu.SMEM)

    def w_spec(shape, which):
        return pl.BlockSpec(
            (None, None) + shape,
            lambda t, ea, eb, *_: (l, (ea, eb)[which][t], 0, 0))

    any_spec = pl.BlockSpec(memory_space=pl.ANY)
    grid_spec = pltpu.PrefetchScalarGridSpec(
        num_scalar_prefetch=4,
        grid=(N_TILES,),
        in_specs=[
            any_spec, any_spec,
            tok_spec(-1), tok_spec(0), tok_spec(1),
            pl.BlockSpec((TG, EXPERTS_PER_GROUP), lambda t, *_: (t, 0)),
            pl.BlockSpec((None, 1, D_MODEL), lambda t, *_: (l, 0, 0)),
            w_spec((D_MODEL, D_FF), 0), w_spec((D_MODEL, D_FF), 0), w_spec((D_FF, D_MODEL), 0),
            w_spec((D_MODEL, D_FF), 1), w_spec((D_MODEL, D_FF), 1), w_spec((D_FF, D_MODEL), 1),
            pl.BlockSpec((1, D_MODEL), lambda t, *_: (0, 0)),
        ],
        out_specs=[any_spec, any_spec],
        scratch_shapes=[
            pltpu.VMEM((2, TG, D_MODEL), F32),
            pltpu.VMEM((2, TG, D_MODEL), F32),
            pltpu.SemaphoreType.DMA((2,)),
            pltpu.SemaphoreType.DMA((2,)),
        ],
    )
    return pl.pallas_call(
        functools.partial(_experts_kernel, final_norm=final_norm, sample_batch_major=final_norm),
        grid_spec=grid_spec,
        out_shape=[jax.ShapeDtypeStruct((N_PROMPT, D_MODEL), F32),
                   jax.ShapeDtypeStruct((N_SAMPLE, D_MODEL), F32)],
        compiler_params=_params(("arbitrary",)),
        name="experts",
    )(tile_ea, tile_eb, tile_cnt, tile_np, xp, xs, slot_tok, slot_tok, slot_tok, gates,
      pw["gmoe"], pw["wgate"], pw["wup"], pw["wdown"], pw["wgate"], pw["wup"], pw["wdown"], gfin)


def _block_diag_gates(w_a, w_x):
    def diag(w):
        w = w.reshape(DEPTH, N_GATE_BLOCKS, HEADS_PER_GATE_BLOCK, RG_HEAD_DIM, RG_HEAD_DIM)
        eye = jnp.eye(HEADS_PER_GATE_BLOCK, dtype=w.dtype)
        full = jnp.einsum("lqhij,hk->lqhikj", w, eye)
        return full.reshape(DEPTH, N_GATE_BLOCKS, MXU_DIM, MXU_DIM)
    return jnp.concatenate([diag(w_a), diag(w_x)], axis=-1).astype(BF16)


def _stack_rows(rows, n):
    width = rows[0].shape[-1]
    rows = [r.reshape(DEPTH, -1, width) for r in rows]
    have = sum(r.shape[1] for r in rows)
    if have < n:
        rows.append(jnp.zeros((DEPTH, n - have, width), F32))
    return jnp.concatenate(rows, axis=1)


def kernel(x_prompt, x_sample, state_rglru_h, state_rglru_conv, state_sconv, cache_mem_k,
           cache_mem_v, mem_prompt, norm_mix, w_in, rg_conv_w, rg_conv_b, rg_w_a, rg_b_a,
           rg_w_x, rg_b_x, rg_lambda, sc_conv_w, norm_rg_out, norm_sc_out, w_out, norm_xattn,
           norm_mem, xa_w_q, xa_w_k, xa_w_v, xa_w_o, norm_moe, router_group_w, router_group_b,
           router_expert_w, router_expert_b, expert_w_gate, expert_w_up, expert_w_down,
           norm_final):
    wr = jnp.concatenate([router_group_w, router_expert_w], axis=2)
    wr = jnp.pad(wr, ((0, 0), (0, 0), (0, LANES - wr.shape[2])))
    br = jnp.concatenate([router_group_b, router_expert_b], axis=1)
    br = jnp.pad(br, ((0, 0), (0, LANES - br.shape[1]))).reshape(DEPTH, 1, LANES)
    pw = dict(
        gmix=norm_mix.reshape(DEPTH, 1, D_MODEL),
        w_in=w_in.astype(BF16),
        wg=_block_diag_gates(rg_w_a, rg_w_x),
        rgv=_stack_rows([rg_conv_w, rg_conv_b, rg_b_a, rg_b_x, rg_lambda], 8),
        scv=_stack_rows([sc_conv_w, norm_rg_out, norm_sc_out], 8),
        w_out=w_out.astype(BF16),
        gxa=norm_xattn.reshape(DEPTH, 1, D_MODEL),
        gmem=norm_mem.reshape(DEPTH, 1, D_MODEL),
        wq=xa_w_q.astype(BF16),
        wk=xa_w_k.astype(BF16),
        wv=xa_w_v.astype(BF16),
        wo=xa_w_o.astype(BF16),
        gmoe=norm_moe.reshape(DEPTH, 1, D_MODEL),
        wr=wr, br=br,
        wgate=expert_w_gate.astype(BF16),
        wup=expert_w_up.astype(BF16),
        wdown=expert_w_down.astype(BF16),
    )
    gfin = norm_final.reshape(1, D_MODEL)

    conv_in = state_rglru_conv.reshape(DEPTH, DEC_BATCH, (RG_CONV_W - 1) * D_RG)
    sc_in = state_sconv.reshape(DEPTH, DEC_BATCH, (SC_CONV_W - 1) * D_SC)
    cache_k = cache_mem_k.reshape(DEPTH * DEC_BATCH, N_MEM, D_MODEL)
    cache_v = cache_mem_v.reshape(DEPTH * DEC_BATCH, N_MEM, D_MODEL)

    mem = mem_prompt.reshape(BATCH * N_MEM, D_MODEL)
    p_k, p_v = _mem_kv(mem, pw)

    xp = x_prompt.reshape(N_PROMPT, D_MODEL)
    xs = jnp.transpose(x_sample, (1, 0, 2)).reshape(N_SAMPLE, D_MODEL)

    p_conv, p_h, p_sc, s_conv, s_h, s_sc = [], [], [], [], [], []
    for l in range(DEPTH):
        xp, c, hh, sc = _mixer_prompt(xp, l, pw)
        xs, cs, hs, scs = _mixer_sample(xs, l, pw, conv_in, state_rglru_h, sc_in)
        p_conv.append(c)
        p_h.append(hh.reshape(BATCH, D_RG))
        p_sc.append(sc)
        s_conv.append(cs.reshape(DEC_BATCH, RG_CONV_W - 1, D_RG))
        s_h.append(hs)
        s_sc.append(scs.reshape(DEC_BATCH, SC_CONV_W - 1, D_SC))

        xp = _attn_prompt(xp, l, pw, p_k, p_v)
        q_s = _q_sample(xs, l, pw)
        o_s = _attn_sample(q_s.reshape(DEC_SEQ, DEC_BATCH, D_MODEL), l, cache_k, cache_v)
        xs = _oproj_sample(o_s.reshape(N_SAMPLE, D_MODEL), l, pw, xs)

        rinfo = jnp.concatenate([_router(xp, l, pw), _router(xs, l, pw)], axis=0)
        xp, xs = _experts(xp, xs, l, pw, gfin, _dispatch_tables(rinfo),
                          final_norm=(l == DEPTH - 1))

    y_prompt = xp.reshape(BATCH, SEQ, D_MODEL)
    y_sample = xs.reshape(DEC_BATCH, DEC_SEQ, D_MODEL)
    mem_shape = (DEPTH, BATCH, N_MEM, XA_HEADS, XA_HEAD_DIM)
    return (y_prompt, y_sample,
            jnp.stack(p_h), jnp.stack(p_conv), jnp.stack(p_sc),
            p_k.reshape(mem_shape), p_v.reshape(mem_shape),
            jnp.stack(s_h), jnp.stack(s_conv), jnp.stack(s_sc))
```

```python
import functools
import math

import jax
import jax.numpy as jnp
from jax import lax
from jax.experimental import pallas as pl
from jax.experimental.pallas import tpu as pltpu

D_MODEL = 2048
BATCH = 4
SEQ = 2048
DEPTH = 2
DEC_BATCH = 128
DEC_SEQ = 4
D_RG = 1024
D_SC = 1024
RG_HEADS = 16
RG_HEAD_DIM = 64
RG_CONV_W = 4
RG_C = 8.0
SC_CONV_W = 3
D_IN = 2 * D_RG + 3 * D_SC
N_MEM = 256
XA_HEADS = 4
XA_HEAD_DIM = 512
N_GROUPS = 4
EXPERTS_PER_GROUP = 4
N_EXPERTS = 16
D_FF = 512
EPS = 1e-6

N_PROMPT = BATCH * SEQ
N_SAMPLE = DEC_BATCH * DEC_SEQ
N_TOK = N_PROMPT + N_SAMPLE

V7X_VMEM_LIMIT_BYTES = 56 * 1024 * 1024
SUBLANES = 8
LANES = 128
MXU_DIM = 256

HEADS_PER_GATE_BLOCK = MXU_DIM // RG_HEAD_DIM
N_GATE_BLOCKS = D_RG // MXU_DIM

TM_MIX = 256
TM_ATT = 512
TM_ROUTE = 512
ATT_BB = 4
HD_CHUNKS = XA_HEAD_DIM // LANES

PAIR_A = (0, 0, 0, 1, 2, 2)
PAIR_B = (1, 2, 3, 3, 3, 1)
N_PAIRS = len(PAIR_A)
N_BUCKETS = N_GROUPS * N_PAIRS
TG = 256
DMA_UNROLL = 8
N_TILES = -(-(N_TOK + N_BUCKETS * (TG - 1)) // TG)
N_SLOTS = N_TILES * TG
BUCKET_LANE = EXPERTS_PER_GROUP

BF16 = jnp.bfloat16
F32 = jnp.float32


def _params(sem, vmem=V7X_VMEM_LIMIT_BYTES):
    return pltpu.CompilerParams(dimension_semantics=sem, vmem_limit_bytes=vmem)


def _layer_resident(shape, l):
    nd = len(shape)
    return pl.BlockSpec((None,) + shape, lambda *_: (l,) + (0,) * nd, pipeline_mode=pl.Buffered(1))


def _layer_block(shape, l):
    nd = len(shape)
    return pl.BlockSpec((None,) + shape, lambda *_: (l,) + (0,) * nd)


def _rms(x, g):
    return x * lax.rsqrt(jnp.mean(x * x, axis=-1, keepdims=True) + EPS) * g


def _dot(a, b):
    return jnp.dot(a, b, preferred_element_type=F32)


def _rg_gate_inputs(xc, wg_ref, rgv_ref):
    xcb = xc.astype(BF16)
    r_parts, i_parts = [], []
    for q in range(N_GATE_BLOCKS):
        g = _dot(xcb[:, q * MXU_DIM:(q + 1) * MXU_DIM], wg_ref[q])
        r_parts.append(g[:, :MXU_DIM])
        i_parts.append(g[:, MXU_DIM:])
    r = jax.nn.sigmoid(jnp.concatenate(r_parts, axis=1) + rgv_ref[5:6, :])
    i = jax.nn.sigmoid(jnp.concatenate(i_parts, axis=1) + rgv_ref[6:7, :])
    log_a = -RG_C * r * jax.nn.softplus(-rgv_ref[7:8, :])
    a = jnp.exp(log_a)
    mult = jnp.sqrt(1.0 - a * a)
    return a, mult * (i * xc)


def _mix_out(hs, rg_gate, sc_b, uc, scv_ref, w_out_ref):
    rg_out = _rms(hs * jax.nn.gelu(rg_gate), scv_ref[3:4, :]).astype(BF16)
    sc_out = _rms(sc_b * uc, scv_ref[4:5, :]).astype(BF16)
    return _dot(rg_out, w_out_ref[0:D_RG, :]) + _dot(sc_out, w_out_ref[D_RG:, :])


def _shift_rows(x, d, fill):
    m = x.shape[0]
    if d % SUBLANES == 0:
        head = jnp.full((d, x.shape[1]), fill, x.dtype)
        return jnp.concatenate([head, x[:m - d]], axis=0)
    rolled = pltpu.roll(x, d, axis=0)
    row = lax.broadcasted_iota(jnp.int32, x.shape, 0)
    return jnp.where(row >= d, rolled, fill)


def _scan_rows(a, b, h0):
    m = a.shape[0]
    d = 1
    while d < m:
        a_sh = _shift_rows(a, d, 1.0)
        b_sh = _shift_rows(b, d, 0.0)
        b = a * b_sh + b
        a = a * a_sh
        d *= 2
    return a * h0 + b


def _mixer_prompt_kernel(x_ref, gmix_ref, w_in_ref, wg_ref, rgv_ref, scv_ref, w_out_ref,
                         o_ref, conv_ref, h_ref, sc_ref, rgx, usc, hcar):
    t = pl.program_id(1)
    tm = x_ref.shape[0]

    @pl.when(t == 0)
    def _():
        rgx[0:SUBLANES, :] = jnp.zeros((SUBLANES, D_RG), F32)
        usc[0:SUBLANES, :] = jnp.zeros((SUBLANES, D_SC), F32)
        hcar[...] = jnp.zeros_like(hcar)

    x = x_ref[...]
    xn = _rms(x, gmix_ref[...]).astype(BF16)

    rg_x = _dot(xn, w_in_ref[:, 0:D_RG])
    rgx[SUBLANES:SUBLANES + tm, :] = rg_x
    xc = rgx[5:5 + tm, :] * rgv_ref[0:1, :]
    xc = xc + rgx[6:6 + tm, :] * rgv_ref[1:2, :]
    xc = xc + rgx[7:7 + tm, :] * rgv_ref[2:3, :]
    xc = xc + rg_x * rgv_ref[3:4, :]
    xc = xc + rgv_ref[4:5, :]
    a, b = _rg_gate_inputs(xc, wg_ref, rgv_ref)
    hs = _scan_rows(a, b, hcar[0:1, :])
    hcar[0:1, :] = hs[tm - 1:tm, :]

    rg_gate = _dot(xn, w_in_ref[:, D_RG:2 * D_RG])
    sc_b = _dot(xn, w_in_ref[:, 2 * D_RG:2 * D_RG + D_SC])
    sc_c = _dot(xn, w_in_ref[:, 2 * D_RG + D_SC:2 * D_RG + 2 * D_SC])
    sc_x = _dot(xn, w_in_ref[:, 2 * D_RG + 2 * D_SC:])
    u = sc_c * sc_x
    usc[SUBLANES:SUBLANES + tm, :] = u
    uc = usc[6:6 + tm, :] * scv_ref[0:1, :]
    uc = uc + usc[7:7 + tm, :] * scv_ref[1:2, :]
    uc = uc + u * scv_ref[2:3, :]

    o_ref[...] = x + _mix_out(hs, rg_gate, sc_b, uc, scv_ref, w_out_ref)

    @pl.when(t == pl.num_programs(1) - 1)
    def _():
        conv_ref[0] = rgx[tm + 5:tm + 8, :]
        sc_ref[0] = usc[tm + 6:tm + 8, :]
        h_ref[0] = hs[tm - 1:tm, :]

    rgx[0:SUBLANES, :] = rgx[tm:tm + SUBLANES, :]
    usc[0:SUBLANES, :] = usc[tm:tm + SUBLANES, :]


def _mixer_weight_specs(l):
    return [
        _layer_resident((1, D_MODEL), l),
        _layer_resident((D_MODEL, D_IN), l),
        _layer_resident((N_GATE_BLOCKS, MXU_DIM, 2 * MXU_DIM), l),
        _layer_resident((8, D_RG), l),
        _layer_resident((8, D_SC), l),
        _layer_resident((D_MODEL, D_MODEL), l),
    ]


def _mixer_prompt(x, l, pw):
    nt = SEQ // TM_MIX
    return pl.pallas_call(
        _mixer_prompt_kernel,
        grid=(BATCH, nt),
        in_specs=[pl.BlockSpec((TM_MIX, D_MODEL), lambda b, t: (b * nt + t, 0))]
        + _mixer_weight_specs(l),
        out_specs=[
            pl.BlockSpec((TM_MIX, D_MODEL), lambda b, t: (b * nt + t, 0)),
            pl.BlockSpec((1, RG_CONV_W - 1, D_RG), lambda b, t: (b, 0, 0)),
            pl.BlockSpec((1, 1, D_RG), lambda b, t: (b, 0, 0)),
            pl.BlockSpec((1, SC_CONV_W - 1, D_SC), lambda b, t: (b, 0, 0)),
        ],
        out_shape=[
            jax.ShapeDtypeStruct((N_PROMPT, D_MODEL), F32),
            jax.ShapeDtypeStruct((BATCH, RG_CONV_W - 1, D_RG), F32),
            jax.ShapeDtypeStruct((BATCH, 1, D_RG), F32),
            jax.ShapeDtypeStruct((BATCH, SC_CONV_W - 1, D_SC), F32),
        ],
        scratch_shapes=[
            pltpu.VMEM((SUBLANES + TM_MIX, D_RG), F32),
            pltpu.VMEM((SUBLANES + TM_MIX, D_SC), F32),
            pltpu.VMEM((SUBLANES, D_RG), F32),
        ],
        compiler_params=_params(("arbitrary", "arbitrary")),
        name="mixer_prompt",
    )(x, pw["gmix"], pw["w_in"], pw["wg"], pw["rgv"], pw["scv"], pw["w_out"])


def _mixer_sample_kernel(x_ref, gmix_ref, w_in_ref, wg_ref, rgv_ref, scv_ref, w_out_ref,
                         conv_in_ref, h_in_ref, sc_in_ref,
                         o_ref, conv_ref, h_ref, sc_ref, xcs, hss, ucs):
    nb = DEC_BATCH
    x = x_ref[...]
    xn = _rms(x, gmix_ref[...]).astype(BF16)

    rg_x = _dot(xn, w_in_ref[:, 0:D_RG])
    seq = [conv_in_ref[:, k * D_RG:(k + 1) * D_RG] for k in range(RG_CONV_W - 1)]
    seq += [rg_x[t * nb:(t + 1) * nb, :] for t in range(DEC_SEQ)]
    for t in range(DEC_SEQ):
        xc_t = seq[t] * rgv_ref[0:1, :]
        for k in range(1, RG_CONV_W):
            xc_t = xc_t + seq[t + k] * rgv_ref[k:k + 1, :]
        xcs[t * nb:(t + 1) * nb, :] = xc_t + rgv_ref[4:5, :]
    for k in range(RG_CONV_W - 1):
        conv_ref[:, k * D_RG:(k + 1) * D_RG] = seq[DEC_SEQ + k]

    a, b = _rg_gate_inputs(xcs[...], wg_ref, rgv_ref)
    h = h_in_ref[...]
    for t in range(DEC_SEQ):
        h = a[t * nb:(t + 1) * nb, :] * h + b[t * nb:(t + 1) * nb, :]
        hss[t * nb:(t + 1) * nb, :] = h
    h_ref[...] = h

    rg_gate = _dot(xn, w_in_ref[:, D_RG:2 * D_RG])
    sc_b = _dot(xn, w_in_ref[:, 2 * D_RG:2 * D_RG + D_SC])
    sc_c = _dot(xn, w_in_ref[:, 2 * D_RG + D_SC:2 * D_RG + 2 * D_SC])
    sc_x = _dot(xn, w_in_ref[:, 2 * D_RG + 2 * D_SC:])
    u = sc_c * sc_x
    useq = [sc_in_ref[:, k * D_SC:(k + 1) * D_SC] for k in range(SC_CONV_W - 1)]
    useq += [u[t * nb:(t + 1) * nb, :] for t in range(DEC_SEQ)]
    for t in range(DEC_SEQ):
        uc_t = useq[t] * scv_ref[0:1, :]
        for k in range(1, SC_CONV_W):
            uc_t = uc_t + useq[t + k] * scv_ref[k:k + 1, :]
        ucs[t * nb:(t + 1) * nb, :] = uc_t
    for k in range(SC_CONV_W - 1):
        sc_ref[:, k * D_SC:(k + 1) * D_SC] = useq[DEC_SEQ + k]

    o_ref[...] = x + _mix_out(hss[...], rg_gate, sc_b, ucs[...], scv_ref, w_out_ref)


def _mixer_sample(x, l, pw, conv_in, h_in, sc_in):
    conv_w = (RG_CONV_W - 1) * D_RG
    sc_w = (SC_CONV_W - 1) * D_SC
    return pl.pallas_call(
        _mixer_sample_kernel,
        grid=(1,),
        in_specs=[
            pl.BlockSpec((N_SAMPLE, D_MODEL), lambda i: (0, 0)),
        ] + _mixer_weight_specs(l) + [
            _layer_resident((DEC_BATCH, conv_w), l),
            _layer_resident((DEC_BATCH, D_RG), l),
            _layer_resident((DEC_BATCH, sc_w), l),
        ],
        out_specs=[
            pl.BlockSpec((N_SAMPLE, D_MODEL), lambda i: (0, 0)),
            pl.BlockSpec((DEC_BATCH, conv_w), lambda i: (0, 0)),
            pl.BlockSpec((DEC_BATCH, D_RG), lambda i: (0, 0)),
            pl.BlockSpec((DEC_BATCH, sc_w), lambda i: (0, 0)),
        ],
        out_shape=[
            jax.ShapeDtypeStruct((N_SAMPLE, D_MODEL), F32),
            jax.ShapeDtypeStruct((DEC_BATCH, conv_w), F32),
            jax.ShapeDtypeStruct((DEC_BATCH, D_RG), F32),
            jax.ShapeDtypeStruct((DEC_BATCH, sc_w), F32),
        ],
        scratch_shapes=[
            pltpu.VMEM((N_SAMPLE, D_RG), F32),
            pltpu.VMEM((N_SAMPLE, D_RG), F32),
            pltpu.VMEM((N_SAMPLE, D_SC), F32),
        ],
        compiler_params=_params(("arbitrary",)),
        name="mixer_sample",
    )(x, pw["gmix"], pw["w_in"], pw["wg"], pw["rgv"], pw["scv"], pw["w_out"],
      conv_in, h_in, sc_in)


def _mem_kv_kernel(m_ref, g_ref, wk_ref, wv_ref, k_ref, v_ref):
    mn = _rms(m_ref[...], g_ref[...]).astype(BF16)
    k_ref[...] = _dot(mn, wk_ref[...])
    v_ref[...] = _dot(mn, wv_ref[...])


def _mem_kv(mem, pw, tm=512, tn=1024):
    rows = BATCH * N_MEM
    w_spec = pl.BlockSpec((None, D_MODEL, tn), lambda l, i, j: (l, 0, j))
    o_spec = pl.BlockSpec((None, tm, tn), lambda l, i, j: (l, i, j))
    shape = jax.ShapeDtypeStruct((DEPTH, rows, D_MODEL), F32)
    return pl.pallas_call(
        _mem_kv_kernel,
        grid=(DEPTH, rows // tm, D_MODEL // tn),
        in_specs=[
            pl.BlockSpec((tm, D_MODEL), lambda l, i, j: (i, 0)),
            pl.BlockSpec((None, 1, D_MODEL), lambda l, i, j: (l, 0, 0)),
            w_spec, w_spec,
        ],
        out_specs=[o_spec, o_spec],
        out_shape=[shape, shape],
        compiler_params=_params(("arbitrary",) * 3),
        name="mem_kv",
    )(mem, pw["gmem"], pw["wk"], pw["wv"])


def _q_sample_kernel(x_ref, g_ref, w_ref, o_ref):
    xn = _rms(x_ref[...], g_ref[...]).astype(BF16)
    o_ref[...] = _dot(xn, w_ref[...])


def _q_sample(x, l, pw, tn=1024):
    return pl.pallas_call(
        _q_sample_kernel,
        grid=(D_MODEL // tn,),
        in_specs=[
            pl.BlockSpec((N_SAMPLE, D_MODEL), lambda j: (0, 0)),
            _layer_block((1, D_MODEL), l),
            pl.BlockSpec((None, D_MODEL, tn), lambda j: (l, 0, j)),
        ],
        out_specs=pl.BlockSpec((N_SAMPLE, tn), lambda j: (0, j)),
        out_shape=jax.ShapeDtypeStruct((N_SAMPLE, D_MODEL), F32),
        compiler_params=_params(("arbitrary",)),
        name="q_sample",
    )(x, pw["gxa"], pw["wq"])


def _oproj_sample_kernel(a_ref, w_ref, res_ref, o_ref):
    o_ref[...] = res_ref[...] + _dot(a_ref[...].astype(BF16), w_ref[...])


def _oproj_sample(a, l, pw, res, tn=1024):
    return pl.pallas_call(
        _oproj_sample_kernel,
        grid=(D_MODEL // tn,),
        in_specs=[
            pl.BlockSpec((N_SAMPLE, D_MODEL), lambda j: (0, 0)),
            pl.BlockSpec((None, D_MODEL, tn), lambda j: (l, 0, j)),
            pl.BlockSpec((N_SAMPLE, tn), lambda j: (0, j)),
        ],
        out_specs=pl.BlockSpec((N_SAMPLE, tn), lambda j: (0, j)),
        out_shape=jax.ShapeDtypeStruct((N_SAMPLE, D_MODEL), F32),
        compiler_params=_params(("arbitrary",)),
        name="oproj_sample",
    )(a, pw["wo"], res)


def _softmax_rows(s):
    m = jnp.max(s, axis=-1, keepdims=True)
    e = jnp.exp(s - m)
    return e / jnp.sum(e, axis=-1, keepdims=True)


def _attn_prompt_kernel(x_ref, g_ref, wq_ref, k_ref, v_ref, wo_ref, o_ref):
    x = x_ref[...]
    xn = _rms(x, g_ref[...]).astype(BF16)
    q = _dot(xn, wq_ref[...]).astype(BF16)
    k = k_ref[...].astype(BF16)
    v = v_ref[...].astype(BF16)
    outs = []
    for h in range(XA_HEADS):
        sl = slice(h * XA_HEAD_DIM, (h + 1) * XA_HEAD_DIM)
        s = lax.dot_general(q[:, sl], k[:, sl], (((1,), (1,)), ((), ())),
                            preferred_element_type=F32) / math.sqrt(XA_HEAD_DIM)
        p = _softmax_rows(s).astype(BF16)
        outs.append(_dot(p, v[:, sl]).astype(BF16))
    o = jnp.concatenate(outs, axis=1)
    o_ref[...] = x + _dot(o, wo_ref[...])


def _attn_prompt(x, l, pw, k, v):
    nt = SEQ // TM_ATT
    kv_spec = pl.BlockSpec((None, N_MEM, D_MODEL), lambda b, t: (l, b, 0))
    return pl.pallas_call(
        _attn_prompt_kernel,
        grid=(BATCH, nt),
        in_specs=[
            pl.BlockSpec((TM_ATT, D_MODEL), lambda b, t: (b * nt + t, 0)),
            _layer_resident((1, D_MODEL), l),
            _layer_resident((D_MODEL, D_MODEL), l),
            kv_spec, kv_spec,
            _layer_resident((D_MODEL, D_MODEL), l),
        ],
        out_specs=pl.BlockSpec((TM_ATT, D_MODEL), lambda b, t: (b * nt + t, 0)),
        out_shape=jax.ShapeDtypeStruct((N_PROMPT, D_MODEL), F32),
        compiler_params=_params(("arbitrary", "arbitrary")),
        name="attn_prompt",
    )(x, pw["gxa"], pw["wq"], k, v, pw["wo"])


def _head_matrix(mem_ref, stage_ref, b, h):
    n = N_MEM * HD_CHUNKS
    stage_ref[...] = mem_ref[pl.ds(b * n * XA_HEADS + h, n, stride=XA_HEADS), :]
    chunks = [stage_ref[pl.ds(c, N_MEM, stride=HD_CHUNKS), :] for c in range(HD_CHUNKS)]
    return jnp.concatenate(chunks, axis=1).astype(BF16)


def _attn_sample_kernel(q_ref, k_ref, v_ref, o_ref, stage_ref):
    rows = DEC_SEQ * ATT_BB
    keys = ATT_BB * N_MEM
    row_b = lax.broadcasted_iota(jnp.int32, (rows, keys), 0) // DEC_SEQ
    key_b = lax.broadcasted_iota(jnp.int32, (rows, keys), 1) // N_MEM
    own = row_b == key_b
    outs = []
    for h in range(XA_HEADS):
        q = q_ref[:, h * XA_HEAD_DIM:(h + 1) * XA_HEAD_DIM].astype(BF16)
        k = jnp.concatenate([_head_matrix(k_ref, stage_ref, b, h) for b in range(ATT_BB)], axis=0)
        v = jnp.concatenate([_head_matrix(v_ref, stage_ref, b, h) for b in range(ATT_BB)], axis=0)
        s = lax.dot_general(q, k, (((1,), (1,)), ((), ())),
                            preferred_element_type=F32) / math.sqrt(XA_HEAD_DIM)
        s = jnp.where(own, s, -jnp.inf)
        m = jnp.max(s, axis=-1, keepdims=True)
        e = jnp.where(own, jnp.exp(s - m), 0.0)
        p = (e / jnp.sum(e, axis=-1, keepdims=True)).astype(BF16)
        outs.append(_dot(p, v))
    o_ref[...] = jnp.concatenate(outs, axis=1)


def _lane_pieces(cache):
    c = cache.reshape(DEPTH, DEC_BATCH, N_MEM, XA_HEADS, HD_CHUNKS, LANES)
    c = jnp.transpose(c, (0, 1, 2, 4, 3, 5))
    return c.reshape(DEPTH * DEC_BATCH * N_MEM * HD_CHUNKS * XA_HEADS, LANES)


def _attn_sample(q, l, k, v):
    rows = ATT_BB * DEC_SEQ
    blk = ATT_BB * N_MEM * HD_CHUNKS * XA_HEADS
    nb = DEC_BATCH // ATT_BB
    kv_spec = pl.BlockSpec((blk, LANES), lambda i: (l * nb + i, 0))
    return pl.pallas_call(
        _attn_sample_kernel,
        grid=(nb,),
        in_specs=[pl.BlockSpec((rows, D_MODEL), lambda i: (i, 0)), kv_spec, kv_spec],
        out_specs=pl.BlockSpec((rows, D_MODEL), lambda i: (i, 0)),
        out_shape=jax.ShapeDtypeStruct((N_SAMPLE, D_MODEL), F32),
        scratch_shapes=[pltpu.VMEM((N_MEM * HD_CHUNKS, LANES), F32)],
        compiler_params=_params(("arbitrary",)),
        name="attn_sample",
    )(q, k, v)


def _route(logits):
    lane = lax.broadcasted_iota(jnp.int32, logits.shape, 1)
    neg = -jnp.inf
    is_g = lane < N_GROUPS
    lg = jnp.where(is_g, logits, neg)
    mg = jnp.max(lg, axis=-1, keepdims=True)
    eg = jnp.where(is_g, jnp.exp(lg - mg), 0.0)
    p_g = eg / jnp.sum(eg, axis=-1, keepdims=True)
    pg_sel = jnp.max(p_g, axis=-1, keepdims=True)
    g_sel = jnp.min(jnp.where(p_g == pg_sel, lane, LANES), axis=-1, keepdims=True)

    lo = N_GROUPS + g_sel * EXPERTS_PER_GROUP
    in_grp = (lane >= lo) & (lane < lo + EXPERTS_PER_GROUP)
    le = jnp.where(in_grp, logits, neg)
    me = jnp.max(le, axis=-1, keepdims=True)
    ee = jnp.where(in_grp, jnp.exp(le - me), 0.0)
    p_e = ee / jnp.sum(ee, axis=-1, keepdims=True)

    p1 = jnp.max(p_e, axis=-1, keepdims=True)
    i1 = jnp.min(jnp.where(in_grp & (p_e == p1), lane, LANES), axis=-1, keepdims=True)
    rest = in_grp & (lane != i1)
    p2 = jnp.max(jnp.where(rest, p_e, neg), axis=-1, keepdims=True)
    i2 = jnp.min(jnp.where(rest & (p_e == p2), lane, LANES), axis=-1, keepdims=True)
    tot = p1 + p2
    w1 = p1 / tot * pg_sel
    w2 = p2 / tot * pg_sel
    gates = jnp.where(lane == i1, w1, 0.0) + jnp.where(lane == i2, w2, 0.0)

    local = jnp.zeros_like(gates)
    for g in range(N_GROUPS):
        start = N_GROUPS + g * EXPERTS_PER_GROUP
        local = local + jnp.where(lane < EXPERTS_PER_GROUP,
                                  pltpu.roll(gates, LANES - start, axis=1), 0.0)

    ja = jnp.minimum(i1, i2) - lo
    jb = jnp.maximum(i1, i2) - lo
    code = ja * EXPERTS_PER_GROUP + jb
    pair = jnp.full_like(code, N_PAIRS - 1)
    for p in range(N_PAIRS - 1):
        lo_j, hi_j = min(PAIR_A[p], PAIR_B[p]), max(PAIR_A[p], PAIR_B[p])
        pair = jnp.where(code == lo_j * EXPERTS_PER_GROUP + hi_j, p, pair)
    bucket = (g_sel * N_PAIRS + pair).astype(F32)
    return local + jnp.where(lane == BUCKET_LANE, bucket, 0.0)


def _router_kernel(x_ref, g_ref, wr_ref, br_ref, o_ref):
    xn = _rms(x_ref[...], g_ref[...])
    logits = jnp.dot(xn, wr_ref[...], preferred_element_type=F32,
                     precision=lax.Precision.HIGHEST) + br_ref[...]
    o_ref[...] = _route(logits)


def _router(x, l, pw):
    n_rows = x.shape[0]
    return pl.pallas_call(
        _router_kernel,
        grid=(n_rows // TM_ROUTE,),
        in_specs=[
            pl.BlockSpec((TM_ROUTE, D_MODEL), lambda i: (i, 0)),
            _layer_block((1, D_MODEL), l),
            _layer_block((D_MODEL, LANES), l),
            _layer_block((1, LANES), l),
        ],
        out_specs=pl.BlockSpec((TM_ROUTE, LANES), lambda i: (i, 0)),
        out_shape=jax.ShapeDtypeStruct((n_rows, LANES), F32),
        compiler_params=_params(("arbitrary",)),
        name="router",
    )(x, pw["gmoe"], pw["wr"], pw["br"])


def _dispatch_tables(rinfo):
    i32 = jnp.int32
    bucket = rinfo[:, BUCKET_LANE].astype(i32)
    bucket_ids = jnp.arange(N_BUCKETS, dtype=i32)
    onehot = (bucket[:, None] == bucket_ids[None, :]).astype(i32)
    csum = jnp.cumsum(onehot, axis=0)
    counts = csum[-1]
    tiles_per = (counts + TG - 1) // TG
    tile_end = jnp.cumsum(tiles_per)
    tile_start = tile_end - tiles_per
    slot = jnp.sum(onehot * (tile_start[None, :] * TG + csum - 1), axis=1)

    tok1 = jnp.arange(1, N_TOK + 1, dtype=i32).astype(F32)
    payload = jnp.concatenate([tok1[:, None], rinfo[:, :EXPERTS_PER_GROUP]], axis=1)
    slots = jnp.zeros((N_SLOTS, 1 + EXPERTS_PER_GROUP), F32).at[slot].set(
        payload, unique_indices=True)
    slot_tok = slots[:, 0].astype(i32) - 1
    gates = slots[:, 1:]

    tile = jnp.arange(N_TILES, dtype=i32)
    n_valid = tile_end[-1]
    tq = jnp.minimum(tile, n_valid - 1)
    tile_bucket = jnp.sum((tile_end[None, :] <= tq[:, None]).astype(i32), axis=1)
    tile_oh = (tile_bucket[:, None] == bucket_ids[None, :]).astype(i32)
    in_bucket = tile - jnp.sum(tile_oh * tile_start[None, :], axis=1)
    tile_cnt = jnp.clip(jnp.sum(tile_oh * counts[None, :], axis=1) - in_bucket * TG, 0, TG)
    tile_cnt = jnp.where(tile < n_valid, tile_cnt, 0)
    pair_a = jnp.asarray([(b // N_PAIRS) * EXPERTS_PER_GROUP + PAIR_A[b % N_PAIRS]
                          for b in range(N_BUCKETS)], i32)
    pair_b = jnp.asarray([(b // N_PAIRS) * EXPERTS_PER_GROUP + PAIR_B[b % N_PAIRS]
                          for b in range(N_BUCKETS)], i32)
    tile_ea = jnp.sum(tile_oh * pair_a[None, :], axis=1)
    tile_eb = jnp.sum(tile_oh * pair_b[None, :], axis=1)
    is_prompt = (slot_tok >= 0) & (slot_tok < N_PROMPT)
    tile_np = jnp.sum(is_prompt.reshape(N_TILES, TG).astype(i32), axis=1)
    return tile_ea, tile_eb, tile_cnt, tile_np, slot_tok.reshape(N_TILES, 1, TG), gates


def _experts_kernel(ea_ref, eb_ref, cnt_ref, np_ref,
                    xp_hbm, xs_hbm, tokp_ref, tokc_ref, tokn_ref, gates_ref, g_ref,
                    wga_ref, wua_ref, wda_ref, wgb_ref, wub_ref, wdb_ref, gfin_ref,
                    op_hbm, os_hbm, xbuf, obuf, gsem, ssem, *, final_norm, sample_batch_major):
    t = pl.program_id(0)
    nt = pl.num_programs(0)
    buf = lax.rem(t, 2)
    t_prev = jnp.maximum(t - 1, 0)
    t_next = jnp.minimum(t + 1, nt - 1)
    cnt = cnt_ref[t]
    has_next = (t + 1 < nt) & (cnt_ref[t_next] > 0)

    def sample_dst(r):
        if not sample_batch_major:
            return r
        return (r % DEC_BATCH) * DEC_SEQ + r // DEC_BATCH

    def gather_p(tok_ref, b, r):
        return pltpu.make_async_copy(
            xp_hbm.at[pl.ds(tok_ref[0, r], 1)], xbuf.at[b, pl.ds(r, 1)], gsem.at[b])

    def gather_s(tok_ref, b, r):
        return pltpu.make_async_copy(
            xs_hbm.at[pl.ds(tok_ref[0, r] - N_PROMPT, 1)], xbuf.at[b, pl.ds(r, 1)], gsem.at[b])

    def scatter_p(tok_ref, b, r):
        return pltpu.make_async_copy(
            obuf.at[b, pl.ds(r, 1)], op_hbm.at[pl.ds(tok_ref[0, r], 1)], ssem.at[b])

    def scatter_s(tok_ref, b, r):
        dst = sample_dst(tok_ref[0, r] - N_PROMPT)
        return pltpu.make_async_copy(
            obuf.at[b, pl.ds(r, 1)], os_hbm.at[pl.ds(dst, 1)], ssem.at[b])

    def for_range(lo, hi, fn):
        full = (hi - lo) // DMA_UNROLL

        def group(i, c):
            for j in range(DMA_UNROLL):
                fn(lo + i * DMA_UNROLL + j)
            return c

        def single(r, c):
            fn(r)
            return c
        lax.fori_loop(0, full, group, 0)
        lax.fori_loop(lo + full * DMA_UNROLL, hi, single, 0)

    def gather(tile, tok_ref, b, op):
        for_range(0, np_ref[tile], lambda r: op(gather_p(tok_ref, b, r)))
        for_range(np_ref[tile], cnt_ref[tile], lambda r: op(gather_s(tok_ref, b, r)))

    def scatter(tile, tok_ref, b, op):
        for_range(0, np_ref[tile], lambda r: op(scatter_p(tok_ref, b, r)))
        for_range(np_ref[tile], cnt_ref[tile], lambda r: op(scatter_s(tok_ref, b, r)))

    def start(copy):
        copy.start()

    def wait(copy):
        copy.wait()

    @pl.when(t == 0)
    def _():
        xbuf[...] = jnp.zeros_like(xbuf)
        gather(t, tokc_ref, 0, start)

    @pl.when(has_next)
    def _():
        gather(t_next, tokn_ref, 1 - buf, start)

    @pl.when(cnt > 0)
    def _():
        gather(t, tokc_ref, buf, wait)

        x = xbuf[buf]
        xn = _rms(x, g_ref[...]).astype(BF16)
        gates = gates_ref[...]
        lane = lax.broadcasted_iota(jnp.int32, gates.shape, 1)
        ja = ea_ref[t] % EXPERTS_PER_GROUP
        jb = eb_ref[t] % EXPERTS_PER_GROUP
        ga = jnp.sum(jnp.where(lane == ja, gates, 0.0), axis=-1, keepdims=True)
        gb = jnp.sum(jnp.where(lane == jb, gates, 0.0), axis=-1, keepdims=True)
        act_a = (jax.nn.silu(_dot(xn, wga_ref[...])) * _dot(xn, wua_ref[...]) * ga).astype(BF16)
        act_b = (jax.nn.silu(_dot(xn, wgb_ref[...])) * _dot(xn, wub_ref[...]) * gb).astype(BF16)
        y = x + (_dot(act_a, wda_ref[...]) + _dot(act_b, wdb_ref[...]))
        if final_norm:
            y = _rms(y, gfin_ref[...])
        obuf[buf] = y

        @pl.when(t > 0)
        def _():
            scatter(t_prev, tokp_ref, 1 - buf, wait)

        scatter(t, tokc_ref, buf, start)

        @pl.when(jnp.logical_not(has_next))
        def _():
            scatter(t, tokc_ref, buf, wait)


def _experts(xp, xs, l, pw, gfin, tables, final_norm):
    tile_ea, tile_eb, tile_cnt, tile_np, slot_tok, gates = tables

    def tok_spec(shift):
        return pl.BlockSpec(
            (None, 1, TG),
            lambda t, *_: (jnp.clip(t + shift, 0, N_TILES - 1), 0, 0),
            memory_space=pltpu.SMEM)

    def w_spec(shape, which):
        return pl.BlockSpec(
            (None, None) + shape,
            lambda t, ea, eb, *_: (l, (ea, eb)[which][t], 0, 0))

    any_spec = pl.BlockSpec(memory_space=pl.ANY)
    grid_spec = pltpu.PrefetchScalarGridSpec(
        num_scalar_prefetch=4,
        grid=(N_TILES,),
        in_specs=[
            any_spec, any_spec,
            tok_spec(-1), tok_spec(0), tok_spec(1),
            pl.BlockSpec((TG, EXPERTS_PER_GROUP), lambda t, *_: (t, 0)),
            pl.BlockSpec((None, 1, D_MODEL), lambda t, *_: (l, 0, 0)),
            w_spec((D_MODEL, D_FF), 0), w_spec((D_MODEL, D_FF), 0), w_spec((D_FF, D_MODEL), 0),
            w_spec((D_MODEL, D_FF), 1), w_spec((D_MODEL, D_FF), 1), w_spec((D_FF, D_MODEL), 1),
            pl.BlockSpec((1, D_MODEL), lambda t, *_: (0, 0)),
        ],
        out_specs=[any_spec, any_spec],
        scratch_shapes=[
            pltpu.VMEM((2, TG, D_MODEL), F32),
            pltpu.VMEM((2, TG, D_MODEL), F32),
            pltpu.SemaphoreType.DMA((2,)),
            pltpu.SemaphoreType.DMA((2,)),
        ],
    )
    return pl.pallas_call(
        functools.partial(_experts_kernel, final_norm=final_norm, sample_batch_major=final_norm),
        grid_spec=grid_spec,
        out_shape=[jax.ShapeDtypeStruct((N_PROMPT, D_MODEL), F32),
                   jax.ShapeDtypeStruct((N_SAMPLE, D_MODEL), F32)],
        compiler_params=_params(("arbitrary",)),
        name="experts",
    )(tile_ea, tile_eb, tile_cnt, tile_np, xp, xs, slot_tok, slot_tok, slot_tok, gates,
      pw["gmoe"], pw["wgate"], pw["wup"], pw["wdown"], pw["wgate"], pw["wup"], pw["wdown"], gfin)


def _block_diag_gates(w_a, w_x):
    def diag(w):
        w = w.reshape(DEPTH, N_GATE_BLOCKS, HEADS_PER_GATE_BLOCK, RG_HEAD_DIM, RG_HEAD_DIM)
        eye = jnp.eye(HEADS_PER_GATE_BLOCK, dtype=w.dtype)
        full = jnp.einsum("lqhij,hk->lqhikj", w, eye)
        return full.reshape(DEPTH, N_GATE_BLOCKS, MXU_DIM, MXU_DIM)
    return jnp.concatenate([diag(w_a), diag(w_x)], axis=-1).astype(BF16)


def _to_batch_major(a):
    return jnp.transpose(a.reshape(DEC_SEQ, DEC_BATCH, -1), (1, 0, 2)).reshape(N_SAMPLE, -1)


def _to_time_major(a):
    return jnp.transpose(a.reshape(DEC_BATCH, DEC_SEQ, -1), (1, 0, 2)).reshape(N_SAMPLE, -1)


def _stack_rows(rows, n):
    width = rows[0].shape[-1]
    rows = [r.reshape(DEPTH, -1, width) for r in rows]
    have = sum(r.shape[1] for r in rows)
    if have < n:
        rows.append(jnp.zeros((DEPTH, n - have, width), F32))
    return jnp.concatenate(rows, axis=1)


def kernel(x_prompt, x_sample, state_rglru_h, state_rglru_conv, state_sconv, cache_mem_k,
           cache_mem_v, mem_prompt, norm_mix, w_in, rg_conv_w, rg_conv_b, rg_w_a, rg_b_a,
           rg_w_x, rg_b_x, rg_lambda, sc_conv_w, norm_rg_out, norm_sc_out, w_out, norm_xattn,
           norm_mem, xa_w_q, xa_w_k, xa_w_v, xa_w_o, norm_moe, router_group_w, router_group_b,
           router_expert_w, router_expert_b, expert_w_gate, expert_w_up, expert_w_down,
           norm_final):
    wr = jnp.concatenate([router_group_w, router_expert_w], axis=2)
    wr = jnp.pad(wr, ((0, 0), (0, 0), (0, LANES - wr.shape[2])))
    br = jnp.concatenate([router_group_b, router_expert_b], axis=1)
    br = jnp.pad(br, ((0, 0), (0, LANES - br.shape[1]))).reshape(DEPTH, 1, LANES)
    pw = dict(
        gmix=norm_mix.reshape(DEPTH, 1, D_MODEL),
        w_in=w_in.astype(BF16),
        wg=_block_diag_gates(rg_w_a, rg_w_x),
        rgv=_stack_rows([rg_conv_w, rg_conv_b, rg_b_a, rg_b_x, rg_lambda], 8),
        scv=_stack_rows([sc_conv_w, norm_rg_out, norm_sc_out], 8),
        w_out=w_out.astype(BF16),
        gxa=norm_xattn.reshape(DEPTH, 1, D_MODEL),
        gmem=norm_mem.reshape(DEPTH, 1, D_MODEL),
        wq=xa_w_q.astype(BF16),
        wk=xa_w_k.astype(BF16),
        wv=xa_w_v.astype(BF16),
        wo=xa_w_o.astype(BF16),
        gmoe=norm_moe.reshape(DEPTH, 1, D_MODEL),
        wr=wr, br=br,
        wgate=expert_w_gate.astype(BF16),
        wup=expert_w_up.astype(BF16),
        wdown=expert_w_down.astype(BF16),
    )
    gfin = norm_final.reshape(1, D_MODEL)

    conv_in = state_rglru_conv.reshape(DEPTH, DEC_BATCH, (RG_CONV_W - 1) * D_RG)
    sc_in = state_sconv.reshape(DEPTH, DEC_BATCH, (SC_CONV_W - 1) * D_SC)
    cache_k = _lane_pieces(cache_mem_k)
    cache_v = _lane_pieces(cache_mem_v)

    mem = mem_prompt.reshape(BATCH * N_MEM, D_MODEL)
    p_k, p_v = _mem_kv(mem, pw)

    xp = x_prompt.reshape(N_PROMPT, D_MODEL)
    xs = jnp.transpose(x_sample, (1, 0, 2)).reshape(N_SAMPLE, D_MODEL)

    p_conv, p_h, p_sc, s_conv, s_h, s_sc = [], [], [], [], [], []
    for l in range(DEPTH):
        xp, c, hh, sc = _mixer_prompt(xp, l, pw)
        xs, cs, hs, scs = _mixer_sample(xs, l, pw, conv_in, state_rglru_h, sc_in)
        p_conv.append(c)
        p_h.append(hh.reshape(BATCH, D_RG))
        p_sc.append(sc)
        s_conv.append(cs.reshape(DEC_BATCH, RG_CONV_W - 1, D_RG))
        s_h.append(hs)
        s_sc.append(scs.reshape(DEC_BATCH, SC_CONV_W - 1, D_SC))

        xp = _attn_prompt(xp, l, pw, p_k, p_v)
        q_s = _to_batch_major(_q_sample(xs, l, pw))
        o_s = _attn_sample(q_s, l, cache_k, cache_v)
        xs = _oproj_sample(_to_time_major(o_s), l, pw, xs)

        rinfo = jnp.concatenate([_router(xp, l, pw), _router(xs, l, pw)], axis=0)
        xp, xs = _experts(xp, xs, l, pw, gfin, _dispatch_tables(rinfo),
                          final_norm=(l == DEPTH - 1))

    y_prompt = xp.reshape(BATCH, SEQ, D_MODEL)
    y_sample = xs.reshape(DEC_BATCH, DEC_SEQ, D_MODEL)
    mem_shape = (DEPTH, BATCH, N_MEM, XA_HEADS, XA_HEAD_DIM)
    return (y_prompt, y_sample,
            jnp.stack(p_h), jnp.stack(p_conv), jnp.stack(p_sc),
            p_k.reshape(mem_shape), p_v.reshape(mem_shape),
            jnp.stack(s_h), jnp.stack(s_conv), jnp.stack(s_sc))
```

```python
import functools
import math

import jax
import jax.numpy as jnp
from jax import lax
from jax.experimental import pallas as pl
from jax.experimental.pallas import tpu as pltpu

D_MODEL = 2048
BATCH = 4
SEQ = 2048
DEPTH = 2
DEC_BATCH = 128
DEC_SEQ = 4
D_RG = 1024
D_SC = 1024
RG_HEADS = 16
RG_HEAD_DIM = 64
RG_CONV_W = 4
RG_C = 8.0
SC_CONV_W = 3
D_IN = 2 * D_RG + 3 * D_SC
N_MEM = 256
XA_HEADS = 4
XA_HEAD_DIM = 512
N_GROUPS = 4
EXPERTS_PER_GROUP = 4
N_EXPERTS = 16
D_FF = 512
EPS = 1e-6

N_PROMPT = BATCH * SEQ
N_SAMPLE = DEC_BATCH * DEC_SEQ
N_TOK = N_PROMPT + N_SAMPLE

V7X_VMEM_LIMIT_BYTES = 56 * 1024 * 1024
SUBLANES = 8
LANES = 128
MXU_DIM = 256

HEADS_PER_GATE_BLOCK = MXU_DIM // RG_HEAD_DIM
N_GATE_BLOCKS = D_RG // MXU_DIM

TM_MIX = 256
TM_ATT = 512
TM_ROUTE = 512
ATT_BB = 4
HD_CHUNKS = XA_HEAD_DIM // LANES

PAIR_A = (0, 0, 0, 1, 2, 2)
PAIR_B = (1, 2, 3, 3, 3, 1)
N_PAIRS = len(PAIR_A)
N_BUCKETS = N_GROUPS * N_PAIRS
TG = 512
ROW_CHUNK = 128
N_TILES = -(-(N_TOK + N_BUCKETS * (TG - 1)) // TG)
N_SLOTS = N_TILES * TG
BUCKET_LANE = EXPERTS_PER_GROUP

BF16 = jnp.bfloat16
F32 = jnp.float32


def _params(sem, vmem=V7X_VMEM_LIMIT_BYTES):
    return pltpu.CompilerParams(dimension_semantics=sem, vmem_limit_bytes=vmem)


def _layer_resident(shape, l):
    nd = len(shape)
    return pl.BlockSpec((None,) + shape, lambda *_: (l,) + (0,) * nd, pipeline_mode=pl.Buffered(1))


def _layer_block(shape, l):
    nd = len(shape)
    return pl.BlockSpec((None,) + shape, lambda *_: (l,) + (0,) * nd)


def _rms(x, g):
    return x * lax.rsqrt(jnp.mean(x * x, axis=-1, keepdims=True) + EPS) * g


def _dot(a, b):
    return jnp.dot(a, b, preferred_element_type=F32)


def _rg_gate_inputs(xc, wg_ref, rgv_ref):
    xcb = xc.astype(BF16)
    r_parts, i_parts = [], []
    for q in range(N_GATE_BLOCKS):
        g = _dot(xcb[:, q * MXU_DIM:(q + 1) * MXU_DIM], wg_ref[q])
        r_parts.append(g[:, :MXU_DIM])
        i_parts.append(g[:, MXU_DIM:])
    r = jax.nn.sigmoid(jnp.concatenate(r_parts, axis=1) + rgv_ref[5:6, :])
    i = jax.nn.sigmoid(jnp.concatenate(i_parts, axis=1) + rgv_ref[6:7, :])
    log_a = -RG_C * r * jax.nn.softplus(-rgv_ref[7:8, :])
    a = jnp.exp(log_a)
    mult = jnp.sqrt(1.0 - a * a)
    return a, mult * (i * xc)


def _mix_out(hs, rg_gate, sc_b, uc, scv_ref, w_out_ref):
    rg_out = _rms(hs * jax.nn.gelu(rg_gate), scv_ref[3:4, :]).astype(BF16)
    sc_out = _rms(sc_b * uc, scv_ref[4:5, :]).astype(BF16)
    return _dot(rg_out, w_out_ref[0:D_RG, :]) + _dot(sc_out, w_out_ref[D_RG:, :])


def _shift_rows(x, d, fill):
    m = x.shape[0]
    if d % SUBLANES == 0:
        head = jnp.full((d, x.shape[1]), fill, x.dtype)
        return jnp.concatenate([head, x[:m - d]], axis=0)
    rolled = pltpu.roll(x, d, axis=0)
    row = lax.broadcasted_iota(jnp.int32, x.shape, 0)
    return jnp.where(row >= d, rolled, fill)


def _scan_rows(a, b, h0):
    m = a.shape[0]
    d = 1
    while d < m:
        a_sh = _shift_rows(a, d, 1.0)
        b_sh = _shift_rows(b, d, 0.0)
        b = a * b_sh + b
        a = a * a_sh
        d *= 2
    return a * h0 + b


def _mixer_prompt_kernel(x_ref, gmix_ref, w_in_ref, wg_ref, rgv_ref, scv_ref, w_out_ref,
                         o_ref, conv_ref, h_ref, sc_ref, rgx, usc, hcar):
    t = pl.program_id(1)
    tm = x_ref.shape[0]

    @pl.when(t == 0)
    def _():
        rgx[0:SUBLANES, :] = jnp.zeros((SUBLANES, D_RG), F32)
        usc[0:SUBLANES, :] = jnp.zeros((SUBLANES, D_SC), F32)
        hcar[...] = jnp.zeros_like(hcar)

    x = x_ref[...]
    xn = _rms(x, gmix_ref[...]).astype(BF16)

    rg_x = _dot(xn, w_in_ref[:, 0:D_RG])
    rgx[SUBLANES:SUBLANES + tm, :] = rg_x
    xc = rgx[5:5 + tm, :] * rgv_ref[0:1, :]
    xc = xc + rgx[6:6 + tm, :] * rgv_ref[1:2, :]
    xc = xc + rgx[7:7 + tm, :] * rgv_ref[2:3, :]
    xc = xc + rg_x * rgv_ref[3:4, :]
    xc = xc + rgv_ref[4:5, :]
    a, b = _rg_gate_inputs(xc, wg_ref, rgv_ref)
    hs = _scan_rows(a, b, hcar[0:1, :])
    hcar[0:1, :] = hs[tm - 1:tm, :]

    rg_gate = _dot(xn, w_in_ref[:, D_RG:2 * D_RG])
    sc_b = _dot(xn, w_in_ref[:, 2 * D_RG:2 * D_RG + D_SC])
    sc_c = _dot(xn, w_in_ref[:, 2 * D_RG + D_SC:2 * D_RG + 2 * D_SC])
    sc_x = _dot(xn, w_in_ref[:, 2 * D_RG + 2 * D_SC:])
    u = sc_c * sc_x
    usc[SUBLANES:SUBLANES + tm, :] = u
    uc = usc[6:6 + tm, :] * scv_ref[0:1, :]
    uc = uc + usc[7:7 + tm, :] * scv_ref[1:2, :]
    uc = uc + u * scv_ref[2:3, :]

    o_ref[...] = x + _mix_out(hs, rg_gate, sc_b, uc, scv_ref, w_out_ref)

    @pl.when(t == pl.num_programs(1) - 1)
    def _():
        conv_ref[0] = rgx[tm + 5:tm + 8, :]
        sc_ref[0] = usc[tm + 6:tm + 8, :]
        h_ref[0] = hs[tm - 1:tm, :]

    rgx[0:SUBLANES, :] = rgx[tm:tm + SUBLANES, :]
    usc[0:SUBLANES, :] = usc[tm:tm + SUBLANES, :]


def _mixer_weight_specs(l):
    return [
        _layer_resident((1, D_MODEL), l),
        _layer_resident((D_MODEL, D_IN), l),
        _layer_resident((N_GATE_BLOCKS, MXU_DIM, 2 * MXU_DIM), l),
        _layer_resident((8, D_RG), l),
        _layer_resident((8, D_SC), l),
        _layer_resident((D_MODEL, D_MODEL), l),
    ]


def _mixer_prompt(x, l, pw):
    nt = SEQ // TM_MIX
    return pl.pallas_call(
        _mixer_prompt_kernel,
        grid=(BATCH, nt),
        in_specs=[pl.BlockSpec((TM_MIX, D_MODEL), lambda b, t: (b * nt + t, 0))]
        + _mixer_weight_specs(l),
        out_specs=[
            pl.BlockSpec((TM_MIX, D_MODEL), lambda b, t: (b * nt + t, 0)),
            pl.BlockSpec((1, RG_CONV_W - 1, D_RG), lambda b, t: (b, 0, 0)),
            pl.BlockSpec((1, 1, D_RG), lambda b, t: (b, 0, 0)),
            pl.BlockSpec((1, SC_CONV_W - 1, D_SC), lambda b, t: (b, 0, 0)),
        ],
        out_shape=[
            jax.ShapeDtypeStruct((N_PROMPT, D_MODEL), F32),
            jax.ShapeDtypeStruct((BATCH, RG_CONV_W - 1, D_RG), F32),
            jax.ShapeDtypeStruct((BATCH, 1, D_RG), F32),
            jax.ShapeDtypeStruct((BATCH, SC_CONV_W - 1, D_SC), F32),
        ],
        scratch_shapes=[
            pltpu.VMEM((SUBLANES + TM_MIX, D_RG), F32),
            pltpu.VMEM((SUBLANES + TM_MIX, D_SC), F32),
            pltpu.VMEM((SUBLANES, D_RG), F32),
        ],
        compiler_params=_params(("arbitrary", "arbitrary")),
        name="mixer_prompt",
    )(x, pw["gmix"], pw["w_in"], pw["wg"], pw["rgv"], pw["scv"], pw["w_out"])


def _mixer_sample_kernel(x_ref, gmix_ref, w_in_ref, wg_ref, rgv_ref, scv_ref, w_out_ref,
                         conv_in_ref, h_in_ref, sc_in_ref,
                         o_ref, conv_ref, h_ref, sc_ref, xcs, hss, ucs):
    nb = DEC_BATCH
    x = x_ref[...]
    xn = _rms(x, gmix_ref[...]).astype(BF16)

    rg_x = _dot(xn, w_in_ref[:, 0:D_RG])
    seq = [conv_in_ref[:, k * D_RG:(k + 1) * D_RG] for k in range(RG_CONV_W - 1)]
    seq += [rg_x[t * nb:(t + 1) * nb, :] for t in range(DEC_SEQ)]
    for t in range(DEC_SEQ):
        xc_t = seq[t] * rgv_ref[0:1, :]
        for k in range(1, RG_CONV_W):
            xc_t = xc_t + seq[t + k] * rgv_ref[k:k + 1, :]
        xcs[t * nb:(t + 1) * nb, :] = xc_t + rgv_ref[4:5, :]
    for k in range(RG_CONV_W - 1):
        conv_ref[:, k * D_RG:(k + 1) * D_RG] = seq[DEC_SEQ + k]

    a, b = _rg_gate_inputs(xcs[...], wg_ref, rgv_ref)
    h = h_in_ref[...]
    for t in range(DEC_SEQ):
        h = a[t * nb:(t + 1) * nb, :] * h + b[t * nb:(t + 1) * nb, :]
        hss[t * nb:(t + 1) * nb, :] = h
    h_ref[...] = h

    rg_gate = _dot(xn, w_in_ref[:, D_RG:2 * D_RG])
    sc_b = _dot(xn, w_in_ref[:, 2 * D_RG:2 * D_RG + D_SC])
    sc_c = _dot(xn, w_in_ref[:, 2 * D_RG + D_SC:2 * D_RG + 2 * D_SC])
    sc_x = _dot(xn, w_in_ref[:, 2 * D_RG + 2 * D_SC:])
    u = sc_c * sc_x
    useq = [sc_in_ref[:, k * D_SC:(k + 1) * D_SC] for k in range(SC_CONV_W - 1)]
    useq += [u[t * nb:(t + 1) * nb, :] for t in range(DEC_SEQ)]
    for t in range(DEC_SEQ):
        uc_t = useq[t] * scv_ref[0:1, :]
        for k in range(1, SC_CONV_W):
            uc_t = uc_t + useq[t + k] * scv_ref[k:k + 1, :]
        ucs[t * nb:(t + 1) * nb, :] = uc_t
    for k in range(SC_CONV_W - 1):
        sc_ref[:, k * D_SC:(k + 1) * D_SC] = useq[DEC_SEQ + k]

    o_ref[...] = x + _mix_out(hss[...], rg_gate, sc_b, ucs[...], scv_ref, w_out_ref)


def _mixer_sample(x, l, pw, conv_in, h_in, sc_in):
    conv_w = (RG_CONV_W - 1) * D_RG
    sc_w = (SC_CONV_W - 1) * D_SC
    return pl.pallas_call(
        _mixer_sample_kernel,
        grid=(1,),
        in_specs=[
            pl.BlockSpec((N_SAMPLE, D_MODEL), lambda i: (0, 0)),
        ] + _mixer_weight_specs(l) + [
            _layer_resident((DEC_BATCH, conv_w), l),
            _layer_resident((DEC_BATCH, D_RG), l),
            _layer_resident((DEC_BATCH, sc_w), l),
        ],
        out_specs=[
            pl.BlockSpec((N_SAMPLE, D_MODEL), lambda i: (0, 0)),
            pl.BlockSpec((DEC_BATCH, conv_w), lambda i: (0, 0)),
            pl.BlockSpec((DEC_BATCH, D_RG), lambda i: (0, 0)),
            pl.BlockSpec((DEC_BATCH, sc_w), lambda i: (0, 0)),
        ],
        out_shape=[
            jax.ShapeDtypeStruct((N_SAMPLE, D_MODEL), F32),
            jax.ShapeDtypeStruct((DEC_BATCH, conv_w), F32),
            jax.ShapeDtypeStruct((DEC_BATCH, D_RG), F32),
            jax.ShapeDtypeStruct((DEC_BATCH, sc_w), F32),
        ],
        scratch_shapes=[
            pltpu.VMEM((N_SAMPLE, D_RG), F32),
            pltpu.VMEM((N_SAMPLE, D_RG), F32),
            pltpu.VMEM((N_SAMPLE, D_SC), F32),
        ],
        compiler_params=_params(("arbitrary",)),
        name="mixer_sample",
    )(x, pw["gmix"], pw["w_in"], pw["wg"], pw["rgv"], pw["scv"], pw["w_out"],
      conv_in, h_in, sc_in)


def _mem_kv_kernel(m_ref, g_ref, wk_ref, wv_ref, k_ref, v_ref, mn_s):
    @pl.when(pl.program_id(1) == 0)
    def _():
        mn_s[...] = _rms(m_ref[...], g_ref[...]).astype(BF16)

    mn = mn_s[...]
    k_ref[...] = _dot(mn, wk_ref[...].astype(BF16))
    v_ref[...] = _dot(mn, wv_ref[...].astype(BF16))


def _mem_kv(mem, gmem, w_k, w_v, tn=512):
    rows = BATCH * N_MEM
    w_spec = pl.BlockSpec((None, D_MODEL, tn), lambda l, j: (l, 0, j))
    o_spec = pl.BlockSpec((None, rows, tn), lambda l, j: (l, 0, j))
    shape = jax.ShapeDtypeStruct((DEPTH, rows, D_MODEL), F32)
    return pl.pallas_call(
        _mem_kv_kernel,
        grid=(DEPTH, D_MODEL // tn),
        in_specs=[
            pl.BlockSpec((rows, D_MODEL), lambda l, j: (0, 0), pipeline_mode=pl.Buffered(1)),
            pl.BlockSpec((None, 1, D_MODEL), lambda l, j: (l, 0, 0)),
            w_spec, w_spec,
        ],
        out_specs=[o_spec, o_spec],
        out_shape=[shape, shape],
        scratch_shapes=[pltpu.VMEM((rows, D_MODEL), BF16)],
        compiler_params=_params(("arbitrary",) * 2),
        name="mem_kv",
    )(mem, gmem, w_k, w_v)


def _q_sample_kernel(x_ref, g_ref, w_ref, o_ref):
    xn = _rms(x_ref[...], g_ref[...]).astype(BF16)
    o_ref[...] = _dot(xn, w_ref[...])


def _q_sample(x, l, pw, tn=1024):
    return pl.pallas_call(
        _q_sample_kernel,
        grid=(D_MODEL // tn,),
        in_specs=[
            pl.BlockSpec((N_SAMPLE, D_MODEL), lambda j: (0, 0)),
            _layer_block((1, D_MODEL), l),
            pl.BlockSpec((None, D_MODEL, tn), lambda j: (l, 0, j)),
        ],
        out_specs=pl.BlockSpec((N_SAMPLE, tn), lambda j: (0, j)),
        out_shape=jax.ShapeDtypeStruct((N_SAMPLE, D_MODEL), F32),
        compiler_params=_params(("arbitrary",)),
        name="q_sample",
    )(x, pw["gxa"], pw["wq"])


def _oproj_sample_kernel(a_ref, w_ref, res_ref, o_ref):
    o_ref[...] = res_ref[...] + _dot(a_ref[...].astype(BF16), w_ref[...])


def _oproj_sample(a, l, pw, res, tn=1024):
    return pl.pallas_call(
        _oproj_sample_kernel,
        grid=(D_MODEL // tn,),
        in_specs=[
            pl.BlockSpec((N_SAMPLE, D_MODEL), lambda j: (0, 0)),
            pl.BlockSpec((None, D_MODEL, tn), lambda j: (l, 0, j)),
            pl.BlockSpec((N_SAMPLE, tn), lambda j: (0, j)),
        ],
        out_specs=pl.BlockSpec((N_SAMPLE, tn), lambda j: (0, j)),
        out_shape=jax.ShapeDtypeStruct((N_SAMPLE, D_MODEL), F32),
        compiler_params=_params(("arbitrary",)),
        name="oproj_sample",
    )(a, pw["wo"], res)


def _softmax_rows(s):
    m = jnp.max(s, axis=-1, keepdims=True)
    e = jnp.exp(s - m)
    return e / jnp.sum(e, axis=-1, keepdims=True)


def _attn_prompt_kernel(x_ref, g_ref, wq_ref, k_ref, v_ref, wo_ref, o_ref):
    x = x_ref[...]
    xn = _rms(x, g_ref[...]).astype(BF16)
    q = _dot(xn, wq_ref[...]).astype(BF16)
    k = k_ref[...].astype(BF16)
    v = v_ref[...].astype(BF16)
    outs = []
    for h in range(XA_HEADS):
        sl = slice(h * XA_HEAD_DIM, (h + 1) * XA_HEAD_DIM)
        s = lax.dot_general(q[:, sl], k[:, sl], (((1,), (1,)), ((), ())),
                            preferred_element_type=F32) / math.sqrt(XA_HEAD_DIM)
        p = _softmax_rows(s).astype(BF16)
        outs.append(_dot(p, v[:, sl]).astype(BF16))
    o = jnp.concatenate(outs, axis=1)
    o_ref[...] = x + _dot(o, wo_ref[...])


def _attn_prompt(x, l, pw, k, v):
    nt = SEQ // TM_ATT
    kv_spec = pl.BlockSpec((None, N_MEM, D_MODEL), lambda b, t: (l, b, 0))
    return pl.pallas_call(
        _attn_prompt_kernel,
        grid=(BATCH, nt),
        in_specs=[
            pl.BlockSpec((TM_ATT, D_MODEL), lambda b, t: (b * nt + t, 0)),
            _layer_resident((1, D_MODEL), l),
            _layer_resident((D_MODEL, D_MODEL), l),
            kv_spec, kv_spec,
            _layer_resident((D_MODEL, D_MODEL), l),
        ],
        out_specs=pl.BlockSpec((TM_ATT, D_MODEL), lambda b, t: (b * nt + t, 0)),
        out_shape=jax.ShapeDtypeStruct((N_PROMPT, D_MODEL), F32),
        compiler_params=_params(("arbitrary", "arbitrary")),
        name="attn_prompt",
    )(x, pw["gxa"], pw["wq"], k, v, pw["wo"])


def _head_matrix(mem_ref, stage_ref, b, h):
    n = N_MEM * HD_CHUNKS
    stage_ref[...] = mem_ref[pl.ds(b * n * XA_HEADS + h, n, stride=XA_HEADS), :]
    chunks = [stage_ref[pl.ds(c, N_MEM, stride=HD_CHUNKS), :] for c in range(HD_CHUNKS)]
    return jnp.concatenate(chunks, axis=1).astype(BF16)


def _attn_sample_kernel(q_ref, k_ref, v_ref, o_ref, stage_ref):
    rows = DEC_SEQ * ATT_BB
    keys = ATT_BB * N_MEM
    row_b = lax.broadcasted_iota(jnp.int32, (rows, keys), 0) // DEC_SEQ
    key_b = lax.broadcasted_iota(jnp.int32, (rows, keys), 1) // N_MEM
    own = row_b == key_b
    outs = []
    for h in range(XA_HEADS):
        q = q_ref[:, h * XA_HEAD_DIM:(h + 1) * XA_HEAD_DIM].astype(BF16)
        k = jnp.concatenate([_head_matrix(k_ref, stage_ref, b, h) for b in range(ATT_BB)], axis=0)
        v = jnp.concatenate([_head_matrix(v_ref, stage_ref, b, h) for b in range(ATT_BB)], axis=0)
        s = lax.dot_general(q, k, (((1,), (1,)), ((), ())),
                            preferred_element_type=F32) / math.sqrt(XA_HEAD_DIM)
        s = jnp.where(own, s, -jnp.inf)
        m = jnp.max(s, axis=-1, keepdims=True)
        e = jnp.where(own, jnp.exp(s - m), 0.0)
        p = (e / jnp.sum(e, axis=-1, keepdims=True)).astype(BF16)
        outs.append(_dot(p, v))
    o_ref[...] = jnp.concatenate(outs, axis=1)


def _lane_pieces(cache):
    c = cache.reshape(DEPTH, DEC_BATCH, N_MEM, XA_HEADS, HD_CHUNKS, LANES)
    c = jnp.transpose(c, (0, 1, 2, 4, 3, 5))
    return c.reshape(DEPTH * DEC_BATCH * N_MEM * HD_CHUNKS * XA_HEADS, LANES)


def _attn_sample(q, l, k, v):
    rows = ATT_BB * DEC_SEQ
    blk = ATT_BB * N_MEM * HD_CHUNKS * XA_HEADS
    nb = DEC_BATCH // ATT_BB
    kv_spec = pl.BlockSpec((blk, LANES), lambda i: (l * nb + i, 0))
    return pl.pallas_call(
        _attn_sample_kernel,
        grid=(nb,),
        in_specs=[pl.BlockSpec((rows, D_MODEL), lambda i: (i, 0)), kv_spec, kv_spec],
        out_specs=pl.BlockSpec((rows, D_MODEL), lambda i: (i, 0)),
        out_shape=jax.ShapeDtypeStruct((N_SAMPLE, D_MODEL), F32),
        scratch_shapes=[pltpu.VMEM((N_MEM * HD_CHUNKS, LANES), F32)],
        compiler_params=_params(("arbitrary",)),
        name="attn_sample",
    )(q, k, v)


def _route(logits):
    lane = lax.broadcasted_iota(jnp.int32, logits.shape, 1)
    neg = -jnp.inf
    is_g = lane < N_GROUPS
    lg = jnp.where(is_g, logits, neg)
    mg = jnp.max(lg, axis=-1, keepdims=True)
    eg = jnp.where(is_g, jnp.exp(lg - mg), 0.0)
    p_g = eg / jnp.sum(eg, axis=-1, keepdims=True)
    pg_sel = jnp.max(p_g, axis=-1, keepdims=True)
    g_sel = jnp.min(jnp.where(p_g == pg_sel, lane, LANES), axis=-1, keepdims=True)

    lo = N_GROUPS + g_sel * EXPERTS_PER_GROUP
    in_grp = (lane >= lo) & (lane < lo + EXPERTS_PER_GROUP)
    le = jnp.where(in_grp, logits, neg)
    me = jnp.max(le, axis=-1, keepdims=True)
    ee = jnp.where(in_grp, jnp.exp(le - me), 0.0)
    p_e = ee / jnp.sum(ee, axis=-1, keepdims=True)

    p1 = jnp.max(p_e, axis=-1, keepdims=True)
    i1 = jnp.min(jnp.where(in_grp & (p_e == p1), lane, LANES), axis=-1, keepdims=True)
    rest = in_grp & (lane != i1)
    p2 = jnp.max(jnp.where(rest, p_e, neg), axis=-1, keepdims=True)
    i2 = jnp.min(jnp.where(rest & (p_e == p2), lane, LANES), axis=-1, keepdims=True)
    tot = p1 + p2
    w1 = p1 / tot * pg_sel
    w2 = p2 / tot * pg_sel
    gates = jnp.where(lane == i1, w1, 0.0) + jnp.where(lane == i2, w2, 0.0)

    local = jnp.zeros_like(gates)
    for g in range(N_GROUPS):
        start = N_GROUPS + g * EXPERTS_PER_GROUP
        local = local + jnp.where(lane < EXPERTS_PER_GROUP,
                                  pltpu.roll(gates, LANES - start, axis=1), 0.0)

    ja = jnp.minimum(i1, i2) - lo
    jb = jnp.maximum(i1, i2) - lo
    code = ja * EXPERTS_PER_GROUP + jb
    pair = jnp.full_like(code, N_PAIRS - 1)
    for p in range(N_PAIRS - 1):
        lo_j, hi_j = min(PAIR_A[p], PAIR_B[p]), max(PAIR_A[p], PAIR_B[p])
        pair = jnp.where(code == lo_j * EXPERTS_PER_GROUP + hi_j, p, pair)
    bucket = (g_sel * N_PAIRS + pair).astype(F32)
    return local + jnp.where(lane == BUCKET_LANE, bucket, 0.0)


def _router_kernel(x_ref, g_ref, wr_ref, br_ref, o_ref):
    xn = _rms(x_ref[...], g_ref[...])
    logits = jnp.dot(xn, wr_ref[...], preferred_element_type=F32,
                     precision=lax.Precision.HIGHEST) + br_ref[...]
    o_ref[...] = _route(logits)


def _router_cast_kernel(x_ref, g_ref, wr_ref, br_ref, wg_ref, wu_ref, wd_ref,
                        o_ref, wg_out, wu_out, wd_out):
    _router_kernel(x_ref, g_ref, wr_ref, br_ref, o_ref)
    wg_out[...] = wg_ref[...].astype(BF16)
    wu_out[...] = wu_ref[...].astype(BF16)
    wd_out[...] = wd_ref[...].astype(BF16)


def _router_specs(l):
    return [
        pl.BlockSpec((TM_ROUTE, D_MODEL), lambda i: (i, 0)),
        _layer_block((1, D_MODEL), l),
        _layer_block((D_MODEL, LANES), l),
        _layer_block((1, LANES), l),
    ]


def _router(x, l, pw):
    n_rows = x.shape[0]
    return pl.pallas_call(
        _router_kernel,
        grid=(n_rows // TM_ROUTE,),
        in_specs=_router_specs(l),
        out_specs=pl.BlockSpec((TM_ROUTE, LANES), lambda i: (i, 0)),
        out_shape=jax.ShapeDtypeStruct((n_rows, LANES), F32),
        compiler_params=_params(("arbitrary",)),
        name="router",
    )(x, pw["gmoe"], pw["wr"], pw["br"])


def _router_and_expert_cast(x, l, pw, w_gate, w_up, w_down):
    n_steps = N_PROMPT // TM_ROUTE
    assert n_steps == N_EXPERTS
    up_shape, down_shape = (D_MODEL, D_FF), (D_FF, D_MODEL)

    def w_in_spec(shape):
        return pl.BlockSpec((None, None) + shape, lambda i: (l, i, 0, 0))

    def w_out_spec(shape):
        return pl.BlockSpec((None,) + shape, lambda i: (i, 0, 0))

    return pl.pallas_call(
        _router_cast_kernel,
        grid=(n_steps,),
        in_specs=_router_specs(l) + [w_in_spec(up_shape), w_in_spec(up_shape), w_in_spec(down_shape)],
        out_specs=[pl.BlockSpec((TM_ROUTE, LANES), lambda i: (i, 0)),
                   w_out_spec(up_shape), w_out_spec(up_shape), w_out_spec(down_shape)],
        out_shape=[jax.ShapeDtypeStruct((N_PROMPT, LANES), F32),
                   jax.ShapeDtypeStruct((N_EXPERTS,) + up_shape, BF16),
                   jax.ShapeDtypeStruct((N_EXPERTS,) + up_shape, BF16),
                   jax.ShapeDtypeStruct((N_EXPERTS,) + down_shape, BF16)],
        compiler_params=_params(("arbitrary",)),
        name="router_cast",
    )(x, pw["gmoe"], pw["wr"], pw["br"], w_gate, w_up, w_down)


def _dispatch_tables(rinfo):
    i32 = jnp.int32
    bucket = rinfo[:, BUCKET_LANE].astype(i32)
    bucket_ids = jnp.arange(N_BUCKETS, dtype=i32)
    onehot = (bucket[:, None] == bucket_ids[None, :]).astype(i32)
    csum = jnp.cumsum(onehot, axis=0)
    counts = csum[-1]
    tiles_per = (counts + TG - 1) // TG
    tile_end = jnp.cumsum(tiles_per)
    tile_start = tile_end - tiles_per
    slot = jnp.sum(onehot * (tile_start[None, :] * TG + csum - 1), axis=1)

    tok1 = jnp.arange(1, N_TOK + 1, dtype=i32).astype(F32)
    payload = jnp.concatenate([tok1[:, None], rinfo[:, :EXPERTS_PER_GROUP]], axis=1)
    slots = jnp.zeros((N_SLOTS, 1 + EXPERTS_PER_GROUP), F32).at[slot].set(
        payload, unique_indices=True)
    slot_tok = slots[:, 0].astype(i32) - 1
    gates = slots[:, 1:]

    tile = jnp.arange(N_TILES, dtype=i32)
    n_valid = tile_end[-1]
    tq = jnp.minimum(tile, n_valid - 1)
    tile_bucket = jnp.sum((tile_end[None, :] <= tq[:, None]).astype(i32), axis=1)
    tile_oh = (tile_bucket[:, None] == bucket_ids[None, :]).astype(i32)
    in_bucket = tile - jnp.sum(tile_oh * tile_start[None, :], axis=1)
    tile_cnt = jnp.clip(jnp.sum(tile_oh * counts[None, :], axis=1) - in_bucket * TG, 0, TG)
    tile_cnt = jnp.where(tile < n_valid, tile_cnt, 0)
    pair_a = jnp.asarray([(b // N_PAIRS) * EXPERTS_PER_GROUP + PAIR_A[b % N_PAIRS]
                          for b in range(N_BUCKETS)], i32)
    pair_b = jnp.asarray([(b // N_PAIRS) * EXPERTS_PER_GROUP + PAIR_B[b % N_PAIRS]
                          for b in range(N_BUCKETS)], i32)
    tile_ea = jnp.sum(tile_oh * pair_a[None, :], axis=1)
    tile_eb = jnp.sum(tile_oh * pair_b[None, :], axis=1)
    is_prompt = (slot_tok >= 0) & (slot_tok < N_PROMPT)
    tile_np = jnp.sum(is_prompt.reshape(N_TILES, TG).astype(i32), axis=1)
    return tile_ea, tile_eb, tile_cnt, tile_np, slot_tok.reshape(N_TILES, 1, TG), gates


def _experts_kernel(ea_ref, eb_ref, cnt_ref, np_ref,
                    xp_hbm, xs_hbm, tokp_ref, tokc_ref, tokn_ref, gates_ref, g_ref,
                    wga_ref, wua_ref, wda_ref, wgb_ref, wub_ref, wdb_ref, gfin_ref,
                    op_hbm, os_hbm, xbuf, obuf, gsem, ssem, *, final_norm, sample_batch_major):
    t = pl.program_id(0)
    nt = pl.num_programs(0)
    buf = lax.rem(t, 2)
    t_prev = jnp.maximum(t - 1, 0)
    t_next = jnp.minimum(t + 1, nt - 1)
    cnt = cnt_ref[t]
    has_next = (t + 1 < nt) & (cnt_ref[t_next] > 0)

    def sample_dst(r):
        if not sample_batch_major:
            return r
        return (r % DEC_BATCH) * DEC_SEQ + r // DEC_BATCH

    def row_slot(r):
        return lax.shift_right_logical(r, 3), lax.bitwise_and(r, SUBLANES - 1)

    def gather_p(tok_ref, b, r, g, s):
        return pltpu.make_async_copy(
            xp_hbm.at[pl.ds(tok_ref[0, r], 1)], xbuf.at[b, g, pl.ds(s, 1)], gsem.at[b])

    def gather_s(tok_ref, b, r, g, s):
        return pltpu.make_async_copy(
            xs_hbm.at[pl.ds(tok_ref[0, r] - N_PROMPT, 1)], xbuf.at[b, g, pl.ds(s, 1)], gsem.at[b])

    def scatter_p(tok_ref, b, r, g, s):
        return pltpu.make_async_copy(
            obuf.at[b, g, pl.ds(s, 1)], op_hbm.at[pl.ds(tok_ref[0, r], 1)], ssem.at[b])

    def scatter_s(tok_ref, b, r, g, s):
        dst = sample_dst(tok_ref[0, r] - N_PROMPT)
        return pltpu.make_async_copy(
            obuf.at[b, g, pl.ds(s, 1)], os_hbm.at[pl.ds(dst, 1)], ssem.at[b])

    def for_range(lo, hi, make_copy, op):
        full = (hi - lo) // SUBLANES
        aligned = isinstance(lo, int) and lo == 0

        def group(i, c):
            for j in range(SUBLANES):
                r = lo + i * SUBLANES + j
                op(make_copy(r, i, j) if aligned else make_copy(r, *row_slot(r)))
            return c

        def single(r, c):
            op(make_copy(r, *row_slot(r)))
            return c
        lax.fori_loop(0, full, group, 0)
        lax.fori_loop(lo + full * SUBLANES, hi, single, 0)

    def gather(tile, tok_ref, b, op):
        for_range(0, np_ref[tile], functools.partial(gather_p, tok_ref, b), op)
        for_range(np_ref[tile], cnt_ref[tile], functools.partial(gather_s, tok_ref, b), op)

    def scatter(tile, tok_ref, b, op):
        for_range(0, np_ref[tile], functools.partial(scatter_p, tok_ref, b), op)
        for_range(np_ref[tile], cnt_ref[tile], functools.partial(scatter_s, tok_ref, b), op)

    def start(copy):
        copy.start()

    def wait(copy):
        copy.wait()

    @pl.when(t == 0)
    def _():
        xbuf[...] = jnp.zeros_like(xbuf)
        gather(t, tokc_ref, 0, start)

    @pl.when(has_next)
    def _():
        gather(t_next, tokn_ref, 1 - buf, start)

    @pl.when(cnt > 0)
    def _():
        gather(t, tokc_ref, buf, wait)

        ja = ea_ref[t] % EXPERTS_PER_GROUP
        jb = eb_ref[t] % EXPERTS_PER_GROUP
        n_chunks = (cnt + ROW_CHUNK - 1) // ROW_CHUNK

        def compute(m):
            x = xbuf[buf, 0:m // SUBLANES].reshape(m, D_MODEL)
            xn = _rms(x, g_ref[...]).astype(BF16)
            gates = gates_ref[0:m, :]
            lane = lax.broadcasted_iota(jnp.int32, gates.shape, 1)
            ga = jnp.sum(jnp.where(lane == ja, gates, 0.0), axis=-1, keepdims=True)
            gb = jnp.sum(jnp.where(lane == jb, gates, 0.0), axis=-1, keepdims=True)
            act_a = (jax.nn.silu(_dot(xn, wga_ref[...])) * _dot(xn, wua_ref[...]) * ga).astype(BF16)
            act_b = (jax.nn.silu(_dot(xn, wgb_ref[...])) * _dot(xn, wub_ref[...]) * gb).astype(BF16)
            y = x + (_dot(act_a, wda_ref[...]) + _dot(act_b, wdb_ref[...]))
            if final_norm:
                y = _rms(y, gfin_ref[...])
            obuf[buf, 0:m // SUBLANES] = y.reshape(m // SUBLANES, SUBLANES, D_MODEL)

        for k in range(1, TG // ROW_CHUNK + 1):
            pl.when(n_chunks == k)(functools.partial(compute, k * ROW_CHUNK))

        @pl.when(t > 0)
        def _():
            scatter(t_prev, tokp_ref, 1 - buf, wait)

        scatter(t, tokc_ref, buf, start)

        @pl.when(jnp.logical_not(has_next))
        def _():
            scatter(t, tokc_ref, buf, wait)


def _experts(xp, xs, l, pw, expert_w, gfin, tables, final_norm):
    tile_ea, tile_eb, tile_cnt, tile_np, slot_tok, gates = tables
    w_gate, w_up, w_down = expert_w

    def tok_spec(shift):
        return pl.BlockSpec(
            (None, 1, TG),
            lambda t, *_: (jnp.clip(t + shift, 0, N_TILES - 1), 0, 0),
            memory_space=pltpu.SMEM)

    def w_spec(shape, which):
        return pl.BlockSpec(
            (None,) + shape,
            lambda t, ea, eb, *_: ((ea, eb)[which][t], 0, 0))

    any_spec = pl.BlockSpec(memory_space=pl.ANY)
    grid_spec = pltpu.PrefetchScalarGridSpec(
        num_scalar_prefetch=4,
        grid=(N_TILES,),
        in_specs=[
            any_spec, any_spec,
            tok_spec(-1), tok_spec(0), tok_spec(1),
            pl.BlockSpec((TG, EXPERTS_PER_GROUP), lambda t, *_: (t, 0)),
            pl.BlockSpec((None, 1, D_MODEL), lambda t, *_: (l, 0, 0)),
            w_spec((D_MODEL, D_FF), 0), w_spec((D_MODEL, D_FF), 0), w_spec((D_FF, D_MODEL), 0),
            w_spec((D_MODEL, D_FF), 1), w_spec((D_MODEL, D_FF), 1), w_spec((D_FF, D_MODEL), 1),
            pl.BlockSpec((1, D_MODEL), lambda t, *_: (0, 0)),
        ],
        out_specs=[any_spec, any_spec],
        scratch_shapes=[
            pltpu.VMEM((2, TG // SUBLANES, SUBLANES, D_MODEL), F32),
            pltpu.VMEM((2, TG // SUBLANES, SUBLANES, D_MODEL), F32),
            pltpu.SemaphoreType.DMA((2,)),
            pltpu.SemaphoreType.DMA((2,)),
        ],
    )
    return pl.pallas_call(
        functools.partial(_experts_kernel, final_norm=final_norm, sample_batch_major=final_norm),
        grid_spec=grid_spec,
        out_shape=[jax.ShapeDtypeStruct((N_PROMPT, D_MODEL), F32),
                   jax.ShapeDtypeStruct((N_SAMPLE, D_MODEL), F32)],
        compiler_params=_params(("arbitrary",)),
        name="experts",
    )(tile_ea, tile_eb, tile_cnt, tile_np, xp, xs, slot_tok, slot_tok, slot_tok, gates,
      pw["gmoe"], w_gate, w_up, w_down, w_gate, w_up, w_down, gfin)


def _block_diag_gates(w_a, w_x):
    def diag(w):
        w = w.reshape(DEPTH, N_GATE_BLOCKS, HEADS_PER_GATE_BLOCK, RG_HEAD_DIM, RG_HEAD_DIM)
        eye = jnp.eye(HEADS_PER_GATE_BLOCK, dtype=w.dtype)
        full = jnp.einsum("lqhij,hk->lqhikj", w, eye)
        return full.reshape(DEPTH, N_GATE_BLOCKS, MXU_DIM, MXU_DIM)
    return jnp.concatenate([diag(w_a), diag(w_x)], axis=-1).astype(BF16)


def _to_batch_major(a):
    return jnp.transpose(a.reshape(DEC_SEQ, DEC_BATCH, -1), (1, 0, 2)).reshape(N_SAMPLE, -1)


def _to_time_major(a):
    return jnp.transpose(a.reshape(DEC_BATCH, DEC_SEQ, -1), (1, 0, 2)).reshape(N_SAMPLE, -1)


def _stack_rows(rows, n):
    width = rows[0].shape[-1]
    rows = [r.reshape(DEPTH, -1, width) for r in rows]
    have = sum(r.shape[1] for r in rows)
    if have < n:
        rows.append(jnp.zeros((DEPTH, n - have, width), F32))
    return jnp.concatenate(rows, axis=1)


def kernel(x_prompt, x_sample, state_rglru_h, state_rglru_conv, state_sconv, cache_mem_k,
           cache_mem_v, mem_prompt, norm_mix, w_in, rg_conv_w, rg_conv_b, rg_w_a, rg_b_a,
           rg_w_x, rg_b_x, rg_lambda, sc_conv_w, norm_rg_out, norm_sc_out, w_out, norm_xattn,
           norm_mem, xa_w_q, xa_w_k, xa_w_v, xa_w_o, norm_moe, router_group_w, router_group_b,
           router_expert_w, router_expert_b, expert_w_gate, expert_w_up, expert_w_down,
           norm_final):
    wr = jnp.concatenate([router_group_w, router_expert_w], axis=2)
    wr = jnp.pad(wr, ((0, 0), (0, 0), (0, LANES - wr.shape[2])))
    br = jnp.concatenate([router_group_b, router_expert_b], axis=1)
    br = jnp.pad(br, ((0, 0), (0, LANES - br.shape[1]))).reshape(DEPTH, 1, LANES)
    pw = dict(
        gmix=norm_mix.reshape(DEPTH, 1, D_MODEL),
        w_in=w_in.astype(BF16),
        wg=_block_diag_gates(rg_w_a, rg_w_x),
        rgv=_stack_rows([rg_conv_w, rg_conv_b, rg_b_a, rg_b_x, rg_lambda], 8),
        scv=_stack_rows([sc_conv_w, norm_rg_out, norm_sc_out], 8),
        w_out=w_out.astype(BF16),
        gxa=norm_xattn.reshape(DEPTH, 1, D_MODEL),
        gmem=norm_mem.reshape(DEPTH, 1, D_MODEL),
        wq=xa_w_q.astype(BF16),
        wo=xa_w_o.astype(BF16),
        gmoe=norm_moe.reshape(DEPTH, 1, D_MODEL),
        wr=wr, br=br,
    )
    gfin = norm_final.reshape(1, D_MODEL)

    conv_in = state_rglru_conv.reshape(DEPTH, DEC_BATCH, (RG_CONV_W - 1) * D_RG)
    sc_in = state_sconv.reshape(DEPTH, DEC_BATCH, (SC_CONV_W - 1) * D_SC)
    cache_k = _lane_pieces(cache_mem_k)
    cache_v = _lane_pieces(cache_mem_v)

    mem = mem_prompt.reshape(BATCH * N_MEM, D_MODEL)
    p_k, p_v = _mem_kv(mem, pw["gmem"], xa_w_k, xa_w_v)

    xp = x_prompt.reshape(N_PROMPT, D_MODEL)
    xs = jnp.transpose(x_sample, (1, 0, 2)).reshape(N_SAMPLE, D_MODEL)

    p_conv, p_h, p_sc, s_conv, s_h, s_sc = [], [], [], [], [], []
    for l in range(DEPTH):
        xp, c, hh, sc = _mixer_prompt(xp, l, pw)
        xs, cs, hs, scs = _mixer_sample(xs, l, pw, conv_in, state_rglru_h, sc_in)
        p_conv.append(c)
        p_h.append(hh.reshape(BATCH, D_RG))
        p_sc.append(sc)
        s_conv.append(cs.reshape(DEC_BATCH, RG_CONV_W - 1, D_RG))
        s_h.append(hs)
        s_sc.append(scs.reshape(DEC_BATCH, SC_CONV_W - 1, D_SC))

        xp = _attn_prompt(xp, l, pw, p_k, p_v)
        q_s = _to_batch_major(_q_sample(xs, l, pw))
        o_s = _attn_sample(q_s, l, cache_k, cache_v)
        xs = _oproj_sample(_to_time_major(o_s), l, pw, xs)

        rinfo_p, *expert_w = _router_and_expert_cast(
            xp, l, pw, expert_w_gate, expert_w_up, expert_w_down)
        rinfo = jnp.concatenate([rinfo_p, _router(xs, l, pw)], axis=0)
        xp, xs = _experts(xp, xs, l, pw, expert_w, gfin, _dispatch_tables(rinfo),
                          final_norm=(l == DEPTH - 1))

    y_prompt = xp.reshape(BATCH, SEQ, D_MODEL)
    y_sample = xs.reshape(DEC_BATCH, DEC_SEQ, D_MODEL)
    mem_shape = (DEPTH, BATCH, N_MEM, XA_HEADS, XA_HEAD_DIM)
    return (y_prompt, y_sample,
            jnp.stack(p_h), jnp.stack(p_conv), jnp.stack(p_sc),
            p_k.reshape(mem_shape), p_v.reshape(mem_shape),
            jnp.stack(s_h), jnp.stack(s_conv), jnp.stack(s_sc))
```

```python
import functools
import math

import jax
import jax.numpy as jnp
from jax import lax
from jax.experimental import pallas as pl
from jax.experimental.pallas import tpu as pltpu

D_MODEL = 2048
BATCH = 4
SEQ = 2048
DEPTH = 2
DEC_BATCH = 128
DEC_SEQ = 4
D_RG = 1024
D_SC = 1024
RG_HEADS = 16
RG_HEAD_DIM = 64
RG_CONV_W = 4
RG_C = 8.0
SC_CONV_W = 3
D_IN = 2 * D_RG + 3 * D_SC
N_MEM = 256
XA_HEADS = 4
XA_HEAD_DIM = 512
N_GROUPS = 4
EXPERTS_PER_GROUP = 4
N_EXPERTS = 16
D_FF = 512
EPS = 1e-6

N_PROMPT = BATCH * SEQ
N_SAMPLE = DEC_BATCH * DEC_SEQ
N_TOK = N_PROMPT + N_SAMPLE

V7X_VMEM_LIMIT_BYTES = 56 * 1024 * 1024
SUBLANES = 8
LANES = 128
MXU_DIM = 256

HEADS_PER_GATE_BLOCK = MXU_DIM // RG_HEAD_DIM
N_GATE_BLOCKS = D_RG // MXU_DIM

TM_MIX = 256
TM_ATT = 512
TM_ROUTE = 512
ATT_BB = 4
HD_CHUNKS = XA_HEAD_DIM // LANES

PAIR_A = (0, 0, 0, 1, 2, 2)
PAIR_B = (1, 2, 3, 3, 3, 1)
N_PAIRS = len(PAIR_A)
N_BUCKETS = N_GROUPS * N_PAIRS
TG = 512
ROW_CHUNK = 128
N_TILES = -(-(N_TOK + N_BUCKETS * (TG - 1)) // TG)
N_SLOTS = N_TILES * TG
BUCKET_LANE = EXPERTS_PER_GROUP

BF16 = jnp.bfloat16
F32 = jnp.float32


def _params(sem, vmem=V7X_VMEM_LIMIT_BYTES):
    return pltpu.CompilerParams(dimension_semantics=sem, vmem_limit_bytes=vmem)


def _layer_resident(shape, l):
    nd = len(shape)
    return pl.BlockSpec((None,) + shape, lambda *_: (l,) + (0,) * nd, pipeline_mode=pl.Buffered(1))


def _layer_block(shape, l):
    nd = len(shape)
    return pl.BlockSpec((None,) + shape, lambda *_: (l,) + (0,) * nd)


def _whole_resident(shape):
    nd = len(shape)
    return pl.BlockSpec(shape, lambda *_: (0,) * nd, pipeline_mode=pl.Buffered(1))


def _rider_specs(srcs, l, n_steps, step_of):
    in_specs, out_specs, out_shapes = [], [], []
    for a in srcs:
        _, r, c = a.shape
        rows = r // n_steps
        assert rows * n_steps == r and rows % 16 == 0
        in_specs.append(pl.BlockSpec((None, rows, c), lambda *g: (l, step_of(*g), 0)))
        out_specs.append(pl.BlockSpec((rows, c), lambda *g: (step_of(*g), 0)))
        out_shapes.append(jax.ShapeDtypeStruct((r, c), BF16))
    return in_specs, out_specs, out_shapes


def _round_riders(src_refs, dst_refs):
    for src, dst in zip(src_refs, dst_refs):
        dst[...] = src[...].astype(BF16)


def _rms(x, g):
    return x * lax.rsqrt(jnp.mean(x * x, axis=-1, keepdims=True) + EPS) * g


def _dot(a, b):
    return jnp.dot(a, b, preferred_element_type=F32)


def _rg_gate_inputs(xc, wg_ref, rgv_ref):
    xcb = xc.astype(BF16)
    r_parts, i_parts = [], []
    for q in range(N_GATE_BLOCKS):
        g = _dot(xcb[:, q * MXU_DIM:(q + 1) * MXU_DIM], wg_ref[q])
        r_parts.append(g[:, :MXU_DIM])
        i_parts.append(g[:, MXU_DIM:])
    r = jax.nn.sigmoid(jnp.concatenate(r_parts, axis=1) + rgv_ref[5:6, :])
    i = jax.nn.sigmoid(jnp.concatenate(i_parts, axis=1) + rgv_ref[6:7, :])
    log_a = -RG_C * r * jax.nn.softplus(-rgv_ref[7:8, :])
    a = jnp.exp(log_a)
    mult = jnp.sqrt(1.0 - a * a)
    return a, mult * (i * xc)


def _mix_out(hs, rg_gate, sc_b, uc, scv_ref, w_out_ref):
    rg_out = _rms(hs * jax.nn.gelu(rg_gate), scv_ref[3:4, :]).astype(BF16)
    sc_out = _rms(sc_b * uc, scv_ref[4:5, :]).astype(BF16)
    return _dot(rg_out, w_out_ref[0:D_RG, :]) + _dot(sc_out, w_out_ref[D_RG:, :])


def _shift_rows(x, d, fill):
    m = x.shape[0]
    if d % SUBLANES == 0:
        head = jnp.full((d, x.shape[1]), fill, x.dtype)
        return jnp.concatenate([head, x[:m - d]], axis=0)
    rolled = pltpu.roll(x, d, axis=0)
    row = lax.broadcasted_iota(jnp.int32, x.shape, 0)
    return jnp.where(row >= d, rolled, fill)


def _scan_rows(a, b, h0):
    m = a.shape[0]
    d = 1
    while d < m:
        a_sh = _shift_rows(a, d, 1.0)
        b_sh = _shift_rows(b, d, 0.0)
        b = a * b_sh + b
        a = a * a_sh
        d *= 2
    return a * h0 + b


def _mixer_prompt_kernel(x_ref, gmix_ref, w_in_ref, wg_ref, rgv_ref, scv_ref, w_out_ref, *rest,
                         n_riders):
    rider_src, rest = rest[:n_riders], rest[n_riders:]
    o_ref, conv_ref, h_ref, sc_ref = rest[:4]
    rider_dst, (rgx, usc, hcar) = rest[4:4 + n_riders], rest[4 + n_riders:]
    _round_riders(rider_src, rider_dst)
    t = pl.program_id(1)
    tm = x_ref.shape[0]

    @pl.when(t == 0)
    def _():
        rgx[0:SUBLANES, :] = jnp.zeros((SUBLANES, D_RG), F32)
        usc[0:SUBLANES, :] = jnp.zeros((SUBLANES, D_SC), F32)
        hcar[...] = jnp.zeros_like(hcar)

    x = x_ref[...]
    xn = _rms(x, gmix_ref[...]).astype(BF16)

    rg_x = _dot(xn, w_in_ref[:, 0:D_RG])
    rgx[SUBLANES:SUBLANES + tm, :] = rg_x
    xc = rgx[5:5 + tm, :] * rgv_ref[0:1, :]
    xc = xc + rgx[6:6 + tm, :] * rgv_ref[1:2, :]
    xc = xc + rgx[7:7 + tm, :] * rgv_ref[2:3, :]
    xc = xc + rg_x * rgv_ref[3:4, :]
    xc = xc + rgv_ref[4:5, :]
    a, b = _rg_gate_inputs(xc, wg_ref, rgv_ref)
    hs = _scan_rows(a, b, hcar[0:1, :])
    hcar[0:1, :] = hs[tm - 1:tm, :]

    rg_gate = _dot(xn, w_in_ref[:, D_RG:2 * D_RG])
    sc_b = _dot(xn, w_in_ref[:, 2 * D_RG:2 * D_RG + D_SC])
    sc_c = _dot(xn, w_in_ref[:, 2 * D_RG + D_SC:2 * D_RG + 2 * D_SC])
    sc_x = _dot(xn, w_in_ref[:, 2 * D_RG + 2 * D_SC:])
    u = sc_c * sc_x
    usc[SUBLANES:SUBLANES + tm, :] = u
    uc = usc[6:6 + tm, :] * scv_ref[0:1, :]
    uc = uc + usc[7:7 + tm, :] * scv_ref[1:2, :]
    uc = uc + u * scv_ref[2:3, :]

    o_ref[...] = x + _mix_out(hs, rg_gate, sc_b, uc, scv_ref, w_out_ref)

    @pl.when(t == pl.num_programs(1) - 1)
    def _():
        conv_ref[0] = rgx[tm + 5:tm + 8, :]
        sc_ref[0] = usc[tm + 6:tm + 8, :]
        h_ref[0] = hs[tm - 1:tm, :]

    rgx[0:SUBLANES, :] = rgx[tm:tm + SUBLANES, :]
    usc[0:SUBLANES, :] = usc[tm:tm + SUBLANES, :]


def _mixer_weight_specs(l):
    return [
        _layer_resident((1, D_MODEL), l),
        _whole_resident((D_MODEL, D_IN)),
        _layer_resident((N_GATE_BLOCKS, MXU_DIM, 2 * MXU_DIM), l),
        _layer_resident((8, D_RG), l),
        _layer_resident((8, D_SC), l),
        _layer_resident((D_MODEL, D_MODEL), l),
    ]


def _mixer_prompt(x, l, pw, w_in, riders):
    nt = SEQ // TM_MIX
    r_in, r_out, r_shape = _rider_specs(riders, l, BATCH * nt, lambda b, t: b * nt + t)
    return pl.pallas_call(
        functools.partial(_mixer_prompt_kernel, n_riders=len(riders)),
        grid=(BATCH, nt),
        in_specs=[pl.BlockSpec((TM_MIX, D_MODEL), lambda b, t: (b * nt + t, 0))]
        + _mixer_weight_specs(l) + r_in,
        out_specs=[
            pl.BlockSpec((TM_MIX, D_MODEL), lambda b, t: (b * nt + t, 0)),
            pl.BlockSpec((1, RG_CONV_W - 1, D_RG), lambda b, t: (b, 0, 0)),
            pl.BlockSpec((1, 1, D_RG), lambda b, t: (b, 0, 0)),
            pl.BlockSpec((1, SC_CONV_W - 1, D_SC), lambda b, t: (b, 0, 0)),
        ] + r_out,
        out_shape=[
            jax.ShapeDtypeStruct((N_PROMPT, D_MODEL), F32),
            jax.ShapeDtypeStruct((BATCH, RG_CONV_W - 1, D_RG), F32),
            jax.ShapeDtypeStruct((BATCH, 1, D_RG), F32),
            jax.ShapeDtypeStruct((BATCH, SC_CONV_W - 1, D_SC), F32),
        ] + r_shape,
        scratch_shapes=[
            pltpu.VMEM((SUBLANES + TM_MIX, D_RG), F32),
            pltpu.VMEM((SUBLANES + TM_MIX, D_SC), F32),
            pltpu.VMEM((SUBLANES, D_RG), F32),
        ],
        compiler_params=_params(("arbitrary", "arbitrary")),
        name="mixer_prompt",
    )(x, pw["gmix"], w_in, pw["wg"], pw["rgv"], pw["scv"], pw["w_out"], *riders)


def _mixer_sample_kernel(x_ref, gmix_ref, w_in_ref, wg_ref, rgv_ref, scv_ref, w_out_ref,
                         conv_in_ref, h_in_ref, sc_in_ref,
                         o_ref, conv_ref, h_ref, sc_ref, xcs, hss, ucs):
    nb = DEC_BATCH
    x = x_ref[...]
    xn = _rms(x, gmix_ref[...]).astype(BF16)

    rg_x = _dot(xn, w_in_ref[:, 0:D_RG])
    seq = [conv_in_ref[k] for k in range(RG_CONV_W - 1)]
    seq += [rg_x[t * nb:(t + 1) * nb, :] for t in range(DEC_SEQ)]
    for t in range(DEC_SEQ):
        xc_t = seq[t] * rgv_ref[0:1, :]
        for k in range(1, RG_CONV_W):
            xc_t = xc_t + seq[t + k] * rgv_ref[k:k + 1, :]
        xcs[t * nb:(t + 1) * nb, :] = xc_t + rgv_ref[4:5, :]
    for k in range(RG_CONV_W - 1):
        conv_ref[k] = seq[DEC_SEQ + k]

    a, b = _rg_gate_inputs(xcs[...], wg_ref, rgv_ref)
    h = h_in_ref[...]
    for t in range(DEC_SEQ):
        h = a[t * nb:(t + 1) * nb, :] * h + b[t * nb:(t + 1) * nb, :]
        hss[t * nb:(t + 1) * nb, :] = h
    h_ref[...] = h

    rg_gate = _dot(xn, w_in_ref[:, D_RG:2 * D_RG])
    sc_b = _dot(xn, w_in_ref[:, 2 * D_RG:2 * D_RG + D_SC])
    sc_c = _dot(xn, w_in_ref[:, 2 * D_RG + D_SC:2 * D_RG + 2 * D_SC])
    sc_x = _dot(xn, w_in_ref[:, 2 * D_RG + 2 * D_SC:])
    u = sc_c * sc_x
    useq = [sc_in_ref[:, k * D_SC:(k + 1) * D_SC] for k in range(SC_CONV_W - 1)]
    useq += [u[t * nb:(t + 1) * nb, :] for t in range(DEC_SEQ)]
    for t in range(DEC_SEQ):
        uc_t = useq[t] * scv_ref[0:1, :]
        for k in range(1, SC_CONV_W):
            uc_t = uc_t + useq[t + k] * scv_ref[k:k + 1, :]
        ucs[t * nb:(t + 1) * nb, :] = uc_t
    for k in range(SC_CONV_W - 1):
        sc_ref[:, k * D_SC:(k + 1) * D_SC] = useq[DEC_SEQ + k]

    o_ref[...] = x + _mix_out(hss[...], rg_gate, sc_b, ucs[...], scv_ref, w_out_ref)


def _mixer_sample(x, l, pw, w_in, conv_in, h_in, sc_in):
    conv_shape = (RG_CONV_W - 1, DEC_BATCH, D_RG)
    sc_w = (SC_CONV_W - 1) * D_SC
    return pl.pallas_call(
        _mixer_sample_kernel,
        grid=(1,),
        in_specs=[
            pl.BlockSpec((N_SAMPLE, D_MODEL), lambda i: (0, 0)),
        ] + _mixer_weight_specs(l) + [
            _layer_resident(conv_shape, l),
            _layer_resident((DEC_BATCH, D_RG), l),
            _layer_resident((DEC_BATCH, sc_w), l),
        ],
        out_specs=[
            pl.BlockSpec((N_SAMPLE, D_MODEL), lambda i: (0, 0)),
            pl.BlockSpec(conv_shape, lambda i: (0, 0, 0)),
            pl.BlockSpec((DEC_BATCH, D_RG), lambda i: (0, 0)),
            pl.BlockSpec((DEC_BATCH, sc_w), lambda i: (0, 0)),
        ],
        out_shape=[
            jax.ShapeDtypeStruct((N_SAMPLE, D_MODEL), F32),
            jax.ShapeDtypeStruct(conv_shape, F32),
            jax.ShapeDtypeStruct((DEC_BATCH, D_RG), F32),
            jax.ShapeDtypeStruct((DEC_BATCH, sc_w), F32),
        ],
        scratch_shapes=[
            pltpu.VMEM((N_SAMPLE, D_RG), F32),
            pltpu.VMEM((N_SAMPLE, D_RG), F32),
            pltpu.VMEM((N_SAMPLE, D_SC), F32),
        ],
        compiler_params=_params(("arbitrary",)),
        name="mixer_sample",
    )(x, pw["gmix"], w_in, pw["wg"], pw["rgv"], pw["scv"], pw["w_out"],
      conv_in, h_in, sc_in)


def _mem_kv_kernel(m_ref, g_ref, wk_ref, wv_ref, k_ref, v_ref, mn_s):
    @pl.when(pl.program_id(1) == 0)
    def _():
        mn_s[...] = _rms(m_ref[...], g_ref[...]).astype(BF16)

    mn = mn_s[...]
    k_ref[...] = _dot(mn, wk_ref[...].astype(BF16))
    v_ref[...] = _dot(mn, wv_ref[...].astype(BF16))


def _mem_kv(mem, gmem, w_k, w_v, tn=512):
    rows = BATCH * N_MEM
    w_spec = pl.BlockSpec((None, D_MODEL, tn), lambda l, j: (l, 0, j))
    o_spec = pl.BlockSpec((None, rows, tn), lambda l, j: (l, 0, j))
    shape = jax.ShapeDtypeStruct((DEPTH, rows, D_MODEL), F32)
    return pl.pallas_call(
        _mem_kv_kernel,
        grid=(DEPTH, D_MODEL // tn),
        in_specs=[
            pl.BlockSpec((rows, D_MODEL), lambda l, j: (0, 0), pipeline_mode=pl.Buffered(1)),
            pl.BlockSpec((None, 1, D_MODEL), lambda l, j: (l, 0, 0)),
            w_spec, w_spec,
        ],
        out_specs=[o_spec, o_spec],
        out_shape=[shape, shape],
        scratch_shapes=[pltpu.VMEM((rows, D_MODEL), BF16)],
        compiler_params=_params(("arbitrary",) * 2),
        name="mem_kv",
    )(mem, gmem, w_k, w_v)


def _q_sample_kernel(x_ref, g_ref, w_ref, o_ref):
    xn = _rms(x_ref[...], g_ref[...]).astype(BF16)
    o_ref[...] = _dot(xn, w_ref[...])


def _q_sample(x, l, pw, wq, tn=1024):
    return pl.pallas_call(
        _q_sample_kernel,
        grid=(D_MODEL // tn,),
        in_specs=[
            pl.BlockSpec((N_SAMPLE, D_MODEL), lambda j: (0, 0)),
            _layer_block((1, D_MODEL), l),
            pl.BlockSpec((D_MODEL, tn), lambda j: (0, j)),
        ],
        out_specs=pl.BlockSpec((N_SAMPLE, tn), lambda j: (0, j)),
        out_shape=jax.ShapeDtypeStruct((N_SAMPLE, D_MODEL), F32),
        compiler_params=_params(("arbitrary",)),
        name="q_sample",
    )(x, pw["gxa"], wq)


def _oproj_sample_kernel(a_ref, w_ref, res_ref, o_ref):
    o_ref[...] = res_ref[...] + _dot(a_ref[...].astype(BF16), w_ref[...])


def _oproj_sample(a, wo, res, tn=1024):
    return pl.pallas_call(
        _oproj_sample_kernel,
        grid=(D_MODEL // tn,),
        in_specs=[
            pl.BlockSpec((N_SAMPLE, D_MODEL), lambda j: (0, 0)),
            pl.BlockSpec((D_MODEL, tn), lambda j: (0, j)),
            pl.BlockSpec((N_SAMPLE, tn), lambda j: (0, j)),
        ],
        out_specs=pl.BlockSpec((N_SAMPLE, tn), lambda j: (0, j)),
        out_shape=jax.ShapeDtypeStruct((N_SAMPLE, D_MODEL), F32),
        compiler_params=_params(("arbitrary",)),
        name="oproj_sample",
    )(a, wo, res)


def _softmax_rows(s):
    m = jnp.max(s, axis=-1, keepdims=True)
    e = jnp.exp(s - m)
    return e / jnp.sum(e, axis=-1, keepdims=True)


def _attn_prompt_kernel(x_ref, g_ref, wq_ref, k_ref, v_ref, wo_ref, *rest, n_riders):
    rider_src, o_ref, rider_dst = rest[:n_riders], rest[n_riders], rest[n_riders + 1:]
    _round_riders(rider_src, rider_dst)
    x = x_ref[...]
    xn = _rms(x, g_ref[...]).astype(BF16)
    q = _dot(xn, wq_ref[...]).astype(BF16)
    k = k_ref[...].astype(BF16)
    v = v_ref[...].astype(BF16)
    outs = []
    for h in range(XA_HEADS):
        sl = slice(h * XA_HEAD_DIM, (h + 1) * XA_HEAD_DIM)
        s = lax.dot_general(q[:, sl], k[:, sl], (((1,), (1,)), ((), ())),
                            preferred_element_type=F32) / math.sqrt(XA_HEAD_DIM)
        p = _softmax_rows(s).astype(BF16)
        outs.append(_dot(p, v[:, sl]).astype(BF16))
    o = jnp.concatenate(outs, axis=1)
    o_ref[...] = x + _dot(o, wo_ref[...])


def _attn_prompt(x, l, pw, wq, wo, k, v, riders, rider_layer):
    nt = SEQ // TM_ATT
    kv_spec = pl.BlockSpec((None, N_MEM, D_MODEL), lambda b, t: (l, b, 0))
    r_in, r_out, r_shape = _rider_specs(riders, rider_layer, BATCH * nt, lambda b, t: b * nt + t)
    return pl.pallas_call(
        functools.partial(_attn_prompt_kernel, n_riders=len(riders)),
        grid=(BATCH, nt),
        in_specs=[
            pl.BlockSpec((TM_ATT, D_MODEL), lambda b, t: (b * nt + t, 0)),
            _layer_resident((1, D_MODEL), l),
            _whole_resident((D_MODEL, D_MODEL)),
            kv_spec, kv_spec,
            _whole_resident((D_MODEL, D_MODEL)),
        ] + r_in,
        out_specs=[pl.BlockSpec((TM_ATT, D_MODEL), lambda b, t: (b * nt + t, 0))] + r_out,
        out_shape=[jax.ShapeDtypeStruct((N_PROMPT, D_MODEL), F32)] + r_shape,
        compiler_params=_params(("arbitrary", "arbitrary")),
        name="attn_prompt",
    )(x, pw["gxa"], wq, k, v, wo, *riders)


def _head_matrix(mem_ref, stage_ref, b, h):
    n = N_MEM * HD_CHUNKS
    stage_ref[...] = mem_ref[pl.ds(b * n * XA_HEADS + h, n, stride=XA_HEADS), :]
    chunks = [stage_ref[pl.ds(c, N_MEM, stride=HD_CHUNKS), :] for c in range(HD_CHUNKS)]
    return jnp.concatenate(chunks, axis=1).astype(BF16)


def _attn_sample_kernel(q_ref, k_ref, v_ref, o_ref, stage_ref):
    rows = DEC_SEQ * ATT_BB
    keys = ATT_BB * N_MEM
    row_b = lax.broadcasted_iota(jnp.int32, (rows, keys), 0) // DEC_SEQ
    key_b = lax.broadcasted_iota(jnp.int32, (rows, keys), 1) // N_MEM
    own = row_b == key_b
    outs = []
    for h in range(XA_HEADS):
        q = q_ref[:, h * XA_HEAD_DIM:(h + 1) * XA_HEAD_DIM].astype(BF16)
        k = jnp.concatenate([_head_matrix(k_ref, stage_ref, b, h) for b in range(ATT_BB)], axis=0)
        v = jnp.concatenate([_head_matrix(v_ref, stage_ref, b, h) for b in range(ATT_BB)], axis=0)
        s = lax.dot_general(q, k, (((1,), (1,)), ((), ())),
                            preferred_element_type=F32) / math.sqrt(XA_HEAD_DIM)
        s = jnp.where(own, s, -jnp.inf)
        m = jnp.max(s, axis=-1, keepdims=True)
        e = jnp.where(own, jnp.exp(s - m), 0.0)
        p = (e / jnp.sum(e, axis=-1, keepdims=True)).astype(BF16)
        outs.append(_dot(p, v))
    o_ref[...] = jnp.concatenate(outs, axis=1)


def _lane_pieces(cache):
    c = cache.reshape(DEPTH, DEC_BATCH, N_MEM, XA_HEADS, HD_CHUNKS, LANES)
    c = jnp.transpose(c, (0, 1, 2, 4, 3, 5))
    return c.reshape(DEPTH * DEC_BATCH * N_MEM * HD_CHUNKS * XA_HEADS, LANES)


def _attn_sample(q, l, k, v):
    rows = ATT_BB * DEC_SEQ
    blk = ATT_BB * N_MEM * HD_CHUNKS * XA_HEADS
    nb = DEC_BATCH // ATT_BB
    kv_spec = pl.BlockSpec((blk, LANES), lambda i: (l * nb + i, 0))
    return pl.pallas_call(
        _attn_sample_kernel,
        grid=(nb,),
        in_specs=[pl.BlockSpec((rows, D_MODEL), lambda i: (i, 0)), kv_spec, kv_spec],
        out_specs=pl.BlockSpec((rows, D_MODEL), lambda i: (i, 0)),
        out_shape=jax.ShapeDtypeStruct((N_SAMPLE, D_MODEL), F32),
        scratch_shapes=[pltpu.VMEM((N_MEM * HD_CHUNKS, LANES), F32)],
        compiler_params=_params(("arbitrary",)),
        name="attn_sample",
    )(q, k, v)


def _route(logits):
    lane = lax.broadcasted_iota(jnp.int32, logits.shape, 1)
    neg = -jnp.inf
    is_g = lane < N_GROUPS
    lg = jnp.where(is_g, logits, neg)
    mg = jnp.max(lg, axis=-1, keepdims=True)
    eg = jnp.where(is_g, jnp.exp(lg - mg), 0.0)
    p_g = eg / jnp.sum(eg, axis=-1, keepdims=True)
    pg_sel = jnp.max(p_g, axis=-1, keepdims=True)
    g_sel = jnp.min(jnp.where(p_g == pg_sel, lane, LANES), axis=-1, keepdims=True)

    lo = N_GROUPS + g_sel * EXPERTS_PER_GROUP
    in_grp = (lane >= lo) & (lane < lo + EXPERTS_PER_GROUP)
    le = jnp.where(in_grp, logits, neg)
    me = jnp.max(le, axis=-1, keepdims=True)
    ee = jnp.where(in_grp, jnp.exp(le - me), 0.0)
    p_e = ee / jnp.sum(ee, axis=-1, keepdims=True)

    p1 = jnp.max(p_e, axis=-1, keepdims=True)
    i1 = jnp.min(jnp.where(in_grp & (p_e == p1), lane, LANES), axis=-1, keepdims=True)
    rest = in_grp & (lane != i1)
    p2 = jnp.max(jnp.where(rest, p_e, neg), axis=-1, keepdims=True)
    i2 = jnp.min(jnp.where(rest & (p_e == p2), lane, LANES), axis=-1, keepdims=True)
    tot = p1 + p2
    w1 = p1 / tot * pg_sel
    w2 = p2 / tot * pg_sel
    gates = jnp.where(lane == i1, w1, 0.0) + jnp.where(lane == i2, w2, 0.0)

    local = jnp.zeros_like(gates)
    for g in range(N_GROUPS):
        start = N_GROUPS + g * EXPERTS_PER_GROUP
        local = local + jnp.where(lane < EXPERTS_PER_GROUP,
                                  pltpu.roll(gates, LANES - start, axis=1), 0.0)

    ja = jnp.minimum(i1, i2) - lo
    jb = jnp.maximum(i1, i2) - lo
    code = ja * EXPERTS_PER_GROUP + jb
    pair = jnp.full_like(code, N_PAIRS - 1)
    for p in range(N_PAIRS - 1):
        lo_j, hi_j = min(PAIR_A[p], PAIR_B[p]), max(PAIR_A[p], PAIR_B[p])
        pair = jnp.where(code == lo_j * EXPERTS_PER_GROUP + hi_j, p, pair)
    bucket = (g_sel * N_PAIRS + pair).astype(F32)
    return local + jnp.where(lane == BUCKET_LANE, bucket, 0.0)


def _router_kernel(x_ref, g_ref, wr_ref, br_ref, o_ref):
    xn = _rms(x_ref[...], g_ref[...])
    logits = jnp.dot(xn, wr_ref[...], preferred_element_type=F32,
                     precision=lax.Precision.HIGHEST) + br_ref[...]
    o_ref[...] = _route(logits)


def _router_riders_kernel(x_ref, g_ref, wr_ref, br_ref, *rest, n_riders):
    rider_src, o_ref, rider_dst = rest[:n_riders], rest[n_riders], rest[n_riders + 1:]
    _round_riders(rider_src, rider_dst)
    _router_kernel(x_ref, g_ref, wr_ref, br_ref, o_ref)


def _router_specs(l):
    return [
        pl.BlockSpec((TM_ROUTE, D_MODEL), lambda i: (i, 0)),
        _layer_block((1, D_MODEL), l),
        _layer_block((D_MODEL, LANES), l),
        _layer_block((1, LANES), l),
    ]


def _router(x, l, pw):
    n_rows = x.shape[0]
    return pl.pallas_call(
        _router_kernel,
        grid=(n_rows // TM_ROUTE,),
        in_specs=_router_specs(l),
        out_specs=pl.BlockSpec((TM_ROUTE, LANES), lambda i: (i, 0)),
        out_shape=jax.ShapeDtypeStruct((n_rows, LANES), F32),
        compiler_params=_params(("arbitrary",)),
        name="router",
    )(x, pw["gmoe"], pw["wr"], pw["br"])


def _router_prompt(x, l, pw, riders):
    n_steps = N_PROMPT // TM_ROUTE
    r_in, r_out, r_shape = _rider_specs(riders, l, n_steps, lambda i: i)
    return pl.pallas_call(
        functools.partial(_router_riders_kernel, n_riders=len(riders)),
        grid=(n_steps,),
        in_specs=_router_specs(l) + r_in,
        out_specs=[pl.BlockSpec((TM_ROUTE, LANES), lambda i: (i, 0))] + r_out,
        out_shape=[jax.ShapeDtypeStruct((N_PROMPT, LANES), F32)] + r_shape,
        compiler_params=_params(("arbitrary",)),
        name="router_prompt",
    )(x, pw["gmoe"], pw["wr"], pw["br"], *riders)


def _dispatch_tables(rinfo):
    i32 = jnp.int32
    bucket = rinfo[:, BUCKET_LANE].astype(i32)
    bucket_ids = jnp.arange(N_BUCKETS, dtype=i32)
    onehot = (bucket[:, None] == bucket_ids[None, :]).astype(i32)
    csum = jnp.cumsum(onehot, axis=0)
    counts = csum[-1]
    tiles_per = (counts + TG - 1) // TG
    tile_end = jnp.cumsum(tiles_per)
    tile_start = tile_end - tiles_per
    slot = jnp.sum(onehot * (tile_start[None, :] * TG + csum - 1), axis=1)

    tok1 = jnp.arange(1, N_TOK + 1, dtype=i32).astype(F32)
    payload = jnp.concatenate([tok1[:, None], rinfo[:, :EXPERTS_PER_GROUP]], axis=1)
    slots = jnp.zeros((N_SLOTS, 1 + EXPERTS_PER_GROUP), F32).at[slot].set(
        payload, unique_indices=True)
    slot_tok = slots[:, 0].astype(i32) - 1
    gates = slots[:, 1:]

    tile = jnp.arange(N_TILES, dtype=i32)
    n_valid = tile_end[-1]
    tq = jnp.minimum(tile, n_valid - 1)
    tile_bucket = jnp.sum((tile_end[None, :] <= tq[:, None]).astype(i32), axis=1)
    tile_oh = (tile_bucket[:, None] == bucket_ids[None, :]).astype(i32)
    in_bucket = tile - jnp.sum(tile_oh * tile_start[None, :], axis=1)
    tile_cnt = jnp.clip(jnp.sum(tile_oh * counts[None, :], axis=1) - in_bucket * TG, 0, TG)
    tile_cnt = jnp.where(tile < n_valid, tile_cnt, 0)
    pair_a = jnp.asarray([(b // N_PAIRS) * EXPERTS_PER_GROUP + PAIR_A[b % N_PAIRS]
                          for b in range(N_BUCKETS)], i32)
    pair_b = jnp.asarray([(b // N_PAIRS) * EXPERTS_PER_GROUP + PAIR_B[b % N_PAIRS]
                          for b in range(N_BUCKETS)], i32)
    tile_ea = jnp.sum(tile_oh * pair_a[None, :], axis=1)
    tile_eb = jnp.sum(tile_oh * pair_b[None, :], axis=1)
    is_prompt = (slot_tok >= 0) & (slot_tok < N_PROMPT)
    tile_np = jnp.sum(is_prompt.reshape(N_TILES, TG).astype(i32), axis=1)
    return tile_ea, tile_eb, tile_cnt, tile_np, slot_tok.reshape(N_TILES, 1, TG), gates


def _experts_kernel(ea_ref, eb_ref, cnt_ref, np_ref,
                    xp_hbm, xs_hbm, tokp_ref, tokc_ref, tokn_ref, gates_ref, g_ref,
                    wga_ref, wua_ref, wda_ref, wgb_ref, wub_ref, wdb_ref, gfin_ref,
                    op_hbm, os_hbm, xbuf, obuf, gsem, ssem, *, final_norm, sample_batch_major):
    t = pl.program_id(0)
    nt = pl.num_programs(0)
    buf = lax.rem(t, 2)
    t_prev = jnp.maximum(t - 1, 0)
    t_next = jnp.minimum(t + 1, nt - 1)
    cnt = cnt_ref[t]
    has_next = (t + 1 < nt) & (cnt_ref[t_next] > 0)

    def sample_dst(r):
        if not sample_batch_major:
            return r
        return (r % DEC_BATCH) * DEC_SEQ + r // DEC_BATCH

    def row_slot(r):
        return lax.shift_right_logical(r, 3), lax.bitwise_and(r, SUBLANES - 1)

    def gather_p(tok_ref, b, r, g, s):
        return pltpu.make_async_copy(
            xp_hbm.at[pl.ds(tok_ref[0, r], 1)], xbuf.at[b, g, pl.ds(s, 1)], gsem.at[b])

    def gather_s(tok_ref, b, r, g, s):
        return pltpu.make_async_copy(
            xs_hbm.at[pl.ds(tok_ref[0, r] - N_PROMPT, 1)], xbuf.at[b, g, pl.ds(s, 1)], gsem.at[b])

    def scatter_p(tok_ref, b, r, g, s):
        return pltpu.make_async_copy(
            obuf.at[b, g, pl.ds(s, 1)], op_hbm.at[pl.ds(tok_ref[0, r], 1)], ssem.at[b])

    def scatter_s(tok_ref, b, r, g, s):
        dst = sample_dst(tok_ref[0, r] - N_PROMPT)
        return pltpu.make_async_copy(
            obuf.at[b, g, pl.ds(s, 1)], os_hbm.at[pl.ds(dst, 1)], ssem.at[b])

    def for_range(lo, hi, make_copy, op):
        full = (hi - lo) // SUBLANES
        aligned = isinstance(lo, int) and lo == 0

        def group(i, c):
            for j in range(SUBLANES):
                r = lo + i * SUBLANES + j
                op(make_copy(r, i, j) if aligned else make_copy(r, *row_slot(r)))
            return c

        def single(r, c):
            op(make_copy(r, *row_slot(r)))
            return c
        lax.fori_loop(0, full, group, 0)
        lax.fori_loop(lo + full * SUBLANES, hi, single, 0)

    def gather(tile, tok_ref, b, op):
        for_range(0, np_ref[tile], functools.partial(gather_p, tok_ref, b), op)
        for_range(np_ref[tile], cnt_ref[tile], functools.partial(gather_s, tok_ref, b), op)

    def scatter(tile, tok_ref, b, op):
        for_range(0, np_ref[tile], functools.partial(scatter_p, tok_ref, b), op)
        for_range(np_ref[tile], cnt_ref[tile], functools.partial(scatter_s, tok_ref, b), op)

    def start(copy):
        copy.start()

    def wait(copy):
        copy.wait()

    @pl.when(t == 0)
    def _():
        xbuf[...] = jnp.zeros_like(xbuf)
        gather(t, tokc_ref, 0, start)

    @pl.when(has_next)
    def _():
        gather(t_next, tokn_ref, 1 - buf, start)

    @pl.when(cnt > 0)
    def _():
        gather(t, tokc_ref, buf, wait)

        ja = ea_ref[t] % EXPERTS_PER_GROUP
        jb = eb_ref[t] % EXPERTS_PER_GROUP
        n_chunks = (cnt + ROW_CHUNK - 1) // ROW_CHUNK

        def compute(m):
            x = xbuf[buf, 0:m // SUBLANES].reshape(m, D_MODEL)
            xn = _rms(x, g_ref[...]).astype(BF16)
            gates = gates_ref[0:m, :]
            lane = lax.broadcasted_iota(jnp.int32, gates.shape, 1)
            ga = jnp.sum(jnp.where(lane == ja, gates, 0.0), axis=-1, keepdims=True)
            gb = jnp.sum(jnp.where(lane == jb, gates, 0.0), axis=-1, keepdims=True)
            act_a = (jax.nn.silu(_dot(xn, wga_ref[...])) * _dot(xn, wua_ref[...]) * ga).astype(BF16)
            act_b = (jax.nn.silu(_dot(xn, wgb_ref[...])) * _dot(xn, wub_ref[...]) * gb).astype(BF16)
            y = x + (_dot(act_a, wda_ref[...]) + _dot(act_b, wdb_ref[...]))
            if final_norm:
                y = _rms(y, gfin_ref[...])
            obuf[buf, 0:m // SUBLANES] = y.reshape(m // SUBLANES, SUBLANES, D_MODEL)

        for k in range(1, TG // ROW_CHUNK + 1):
            pl.when(n_chunks == k)(functools.partial(compute, k * ROW_CHUNK))

        @pl.when(t > 0)
        def _():
            scatter(t_prev, tokp_ref, 1 - buf, wait)

        scatter(t, tokc_ref, buf, start)

        @pl.when(jnp.logical_not(has_next))
        def _():
            scatter(t, tokc_ref, buf, wait)


def _experts(xp, xs, l, pw, expert_w, gfin, tables, final_norm):
    tile_ea, tile_eb, tile_cnt, tile_np, slot_tok, gates = tables
    w_gate, w_up, w_down = expert_w

    def tok_spec(shift):
        return pl.BlockSpec(
            (None, 1, TG),
            lambda t, *_: (jnp.clip(t + shift, 0, N_TILES - 1), 0, 0),
            memory_space=pltpu.SMEM)

    def w_spec(shape, which):
        return pl.BlockSpec(
            (None,) + shape,
            lambda t, ea, eb, *_: ((ea, eb)[which][t], 0, 0))

    any_spec = pl.BlockSpec(memory_space=pl.ANY)
    grid_spec = pltpu.PrefetchScalarGridSpec(
        num_scalar_prefetch=4,
        grid=(N_TILES,),
        in_specs=[
            any_spec, any_spec,
            tok_spec(-1), tok_spec(0), tok_spec(1),
            pl.BlockSpec((TG, EXPERTS_PER_GROUP), lambda t, *_: (t, 0)),
            pl.BlockSpec((None, 1, D_MODEL), lambda t, *_: (l, 0, 0)),
            w_spec((D_MODEL, D_FF), 0), w_spec((D_MODEL, D_FF), 0), w_spec((D_FF, D_MODEL), 0),
            w_spec((D_MODEL, D_FF), 1), w_spec((D_MODEL, D_FF), 1), w_spec((D_FF, D_MODEL), 1),
            pl.BlockSpec((1, D_MODEL), lambda t, *_: (0, 0)),
        ],
        out_specs=[any_spec, any_spec],
        scratch_shapes=[
            pltpu.VMEM((2, TG // SUBLANES, SUBLANES, D_MODEL), F32),
            pltpu.VMEM((2, TG // SUBLANES, SUBLANES, D_MODEL), F32),
            pltpu.SemaphoreType.DMA((2,)),
            pltpu.SemaphoreType.DMA((2,)),
        ],
    )
    return pl.pallas_call(
        functools.partial(_experts_kernel, final_norm=final_norm, sample_batch_major=final_norm),
        grid_spec=grid_spec,
        out_shape=[jax.ShapeDtypeStruct((N_PROMPT, D_MODEL), F32),
                   jax.ShapeDtypeStruct((N_SAMPLE, D_MODEL), F32)],
        compiler_params=_params(("arbitrary",)),
        name="experts",
    )(tile_ea, tile_eb, tile_cnt, tile_np, xp, xs, slot_tok, slot_tok, slot_tok, gates,
      pw["gmoe"], w_gate, w_up, w_down, w_gate, w_up, w_down, gfin)


def _block_diag_gates(w_a, w_x):
    def diag(w):
        w = w.reshape(DEPTH, N_GATE_BLOCKS, HEADS_PER_GATE_BLOCK, RG_HEAD_DIM, RG_HEAD_DIM)
        eye = jnp.eye(HEADS_PER_GATE_BLOCK, dtype=w.dtype)
        full = jnp.einsum("lqhij,hk->lqhikj", w, eye)
        return full.reshape(DEPTH, N_GATE_BLOCKS, MXU_DIM, MXU_DIM)
    return jnp.concatenate([diag(w_a), diag(w_x)], axis=-1).astype(BF16)


def _to_batch_major(a):
    return jnp.transpose(a.reshape(DEC_SEQ, DEC_BATCH, -1), (1, 0, 2)).reshape(N_SAMPLE, -1)


def _to_time_major(a):
    return jnp.transpose(a.reshape(DEC_BATCH, DEC_SEQ, -1), (1, 0, 2)).reshape(N_SAMPLE, -1)


def _stack_rows(rows, n):
    width = rows[0].shape[-1]
    rows = [r.reshape(DEPTH, -1, width) for r in rows]
    have = sum(r.shape[1] for r in rows)
    if have < n:
        rows.append(jnp.zeros((DEPTH, n - have, width), F32))
    return jnp.concatenate(rows, axis=1)


def kernel(x_prompt, x_sample, state_rglru_h, state_rglru_conv, state_sconv, cache_mem_k,
           cache_mem_v, mem_prompt, norm_mix, w_in, rg_conv_w, rg_conv_b, rg_w_a, rg_b_a,
           rg_w_x, rg_b_x, rg_lambda, sc_conv_w, norm_rg_out, norm_sc_out, w_out, norm_xattn,
           norm_mem, xa_w_q, xa_w_k, xa_w_v, xa_w_o, norm_moe, router_group_w, router_group_b,
           router_expert_w, router_expert_b, expert_w_gate, expert_w_up, expert_w_down,
           norm_final):
    wr = jnp.concatenate([router_group_w, router_expert_w], axis=2)
    wr = jnp.pad(wr, ((0, 0), (0, 0), (0, LANES - wr.shape[2])))
    br = jnp.concatenate([router_group_b, router_expert_b], axis=1)
    br = jnp.pad(br, ((0, 0), (0, LANES - br.shape[1]))).reshape(DEPTH, 1, LANES)
    pw = dict(
        gmix=norm_mix.reshape(DEPTH, 1, D_MODEL),
        wg=_block_diag_gates(rg_w_a, rg_w_x),
        rgv=_stack_rows([rg_conv_w, rg_conv_b, rg_b_a, rg_b_x, rg_lambda], 8),
        scv=_stack_rows([sc_conv_w, norm_rg_out, norm_sc_out], 8),
        w_out=w_out.astype(BF16),
        gxa=norm_xattn.reshape(DEPTH, 1, D_MODEL),
        gmem=norm_mem.reshape(DEPTH, 1, D_MODEL),
        gmoe=norm_moe.reshape(DEPTH, 1, D_MODEL),
        wr=wr, br=br,
    )
    gfin = norm_final.reshape(1, D_MODEL)

    conv_in = jnp.transpose(state_rglru_conv, (0, 2, 1, 3))
    sc_in = state_sconv.reshape(DEPTH, DEC_BATCH, (SC_CONV_W - 1) * D_SC)
    w_gate_rows = expert_w_gate.reshape(DEPTH, N_EXPERTS * D_MODEL, D_FF)
    w_up_rows = expert_w_up.reshape(DEPTH, N_EXPERTS * D_MODEL, D_FF)
    w_down_rows = expert_w_down.reshape(DEPTH, N_EXPERTS * D_FF, D_MODEL)
    w_in_l = w_in[0].astype(BF16)
    cache_k = _lane_pieces(cache_mem_k)
    cache_v = _lane_pieces(cache_mem_v)

    mem = mem_prompt.reshape(BATCH * N_MEM, D_MODEL)
    p_k, p_v = _mem_kv(mem, pw["gmem"], xa_w_k, xa_w_v)

    xp = x_prompt.reshape(N_PROMPT, D_MODEL)
    xs = jnp.transpose(x_sample, (1, 0, 2)).reshape(N_SAMPLE, D_MODEL)

    p_conv, p_h, p_sc, s_conv, s_h, s_sc = [], [], [], [], [], []
    for l in range(DEPTH):
        xp, c, hh, sc, w_down_l, wq_l, wo_l = _mixer_prompt(
            xp, l, pw, w_in_l, [w_down_rows, xa_w_q, xa_w_o])
        xs, cs, hs, scs = _mixer_sample(xs, l, pw, w_in_l, conv_in, state_rglru_h, sc_in)
        p_conv.append(c)
        p_h.append(hh.reshape(BATCH, D_RG))
        p_sc.append(sc)
        s_conv.append(cs)
        s_h.append(hs)
        s_sc.append(scs.reshape(DEC_BATCH, SC_CONV_W - 1, D_SC))

        next_w_in = [w_in] if l + 1 < DEPTH else []
        xp, *w_in_next = _attn_prompt(xp, l, pw, wq_l, wo_l, p_k, p_v, next_w_in, l + 1)
        q_s = _to_batch_major(_q_sample(xs, l, pw, wq_l))
        o_s = _attn_sample(q_s, l, cache_k, cache_v)
        xs = _oproj_sample(_to_time_major(o_s), wo_l, xs)
        if w_in_next:
            w_in_l = w_in_next[0]

        rinfo_p, w_gate_l, w_up_l = _router_prompt(xp, l, pw, [w_gate_rows, w_up_rows])
        rinfo = jnp.concatenate([rinfo_p, _router(xs, l, pw)], axis=0)
        expert_w = (w_gate_l.reshape(N_EXPERTS, D_MODEL, D_FF),
                    w_up_l.reshape(N_EXPERTS, D_MODEL, D_FF),
                    w_down_l.reshape(N_EXPERTS, D_FF, D_MODEL))
        xp, xs = _experts(xp, xs, l, pw, expert_w, gfin, _dispatch_tables(rinfo),
                          final_norm=(l == DEPTH - 1))

    y_prompt = xp.reshape(BATCH, SEQ, D_MODEL)
    y_sample = xs.reshape(DEC_BATCH, DEC_SEQ, D_MODEL)
    mem_shape = (DEPTH, BATCH, N_MEM, XA_HEADS, XA_HEAD_DIM)
    return (y_prompt, y_sample,
            jnp.stack(p_h), jnp.stack(p_conv), jnp.stack(p_sc),
            p_k.reshape(mem_shape), p_v.reshape(mem_shape),
            jnp.stack(s_h), jnp.transpose(jnp.stack(s_conv), (0, 2, 1, 3)), jnp.stack(s_sc))
```

```python
import functools
import math

import jax
import jax.numpy as jnp
from jax import lax
from jax.experimental import pallas as pl
from jax.experimental.pallas import tpu as pltpu

D_MODEL = 2048
BATCH = 4
SEQ = 2048
DEPTH = 2
DEC_BATCH = 128
DEC_SEQ = 4
D_RG = 1024
D_SC = 1024
RG_HEADS = 16
RG_HEAD_DIM = 64
RG_CONV_W = 4
RG_C = 8.0
SC_CONV_W = 3
D_IN = 2 * D_RG + 3 * D_SC
N_MEM = 256
XA_HEADS = 4
XA_HEAD_DIM = 512
N_GROUPS = 4
EXPERTS_PER_GROUP = 4
N_EXPERTS = 16
D_FF = 512
EPS = 1e-6

N_PROMPT = BATCH * SEQ
N_SAMPLE = DEC_BATCH * DEC_SEQ
N_TOK = N_PROMPT + N_SAMPLE

V7X_VMEM_LIMIT_BYTES = 56 * 1024 * 1024
SUBLANES = 8
LANES = 128
MXU_DIM = 256

HEADS_PER_GATE_BLOCK = MXU_DIM // RG_HEAD_DIM
N_GATE_BLOCKS = D_RG // MXU_DIM

TM_MIX = 256
TM_ATT = 512
TM_ROUTE = 512
ATT_BB = 4
HD_CHUNKS = XA_HEAD_DIM // LANES

PAIR_A = (0, 0, 0, 1, 2, 2)
PAIR_B = (1, 2, 3, 3, 3, 1)
N_PAIRS = len(PAIR_A)
N_BUCKETS = N_GROUPS * N_PAIRS
TG = 512
ROW_CHUNK = 128
N_TILES = -(-(N_TOK + N_BUCKETS * (TG - 1)) // TG)
N_SLOTS = N_TILES * TG
BUCKET_LANE = EXPERTS_PER_GROUP
SLOT_GATE_LANE = 1

BF16 = jnp.bfloat16
F32 = jnp.float32


def _params(sem, vmem=V7X_VMEM_LIMIT_BYTES):
    return pltpu.CompilerParams(dimension_semantics=sem, vmem_limit_bytes=vmem)


def _layer_resident(shape, l):
    nd = len(shape)
    return pl.BlockSpec((None,) + shape, lambda *_: (l,) + (0,) * nd, pipeline_mode=pl.Buffered(1))


def _layer_block(shape, l):
    nd = len(shape)
    return pl.BlockSpec((None,) + shape, lambda *_: (l,) + (0,) * nd)


def _whole_resident(shape):
    nd = len(shape)
    return pl.BlockSpec(shape, lambda *_: (0,) * nd, pipeline_mode=pl.Buffered(1))


def _rider_specs(srcs, l, n_steps, step_of):
    in_specs, out_specs, out_shapes = [], [], []
    for a in srcs:
        _, r, c = a.shape
        rows = r // n_steps
        assert rows * n_steps == r and rows % 16 == 0
        in_specs.append(pl.BlockSpec((None, rows, c), lambda *g: (l, step_of(*g), 0)))
        out_specs.append(pl.BlockSpec((rows, c), lambda *g: (step_of(*g), 0)))
        out_shapes.append(jax.ShapeDtypeStruct((r, c), BF16))
    return in_specs, out_specs, out_shapes


def _round_riders(src_refs, dst_refs):
    for src, dst in zip(src_refs, dst_refs):
        dst[...] = src[...].astype(BF16)


def _rms(x, g):
    return x * lax.rsqrt(jnp.mean(x * x, axis=-1, keepdims=True) + EPS) * g


def _dot(a, b):
    return jnp.dot(a, b, preferred_element_type=F32)


def _rg_gate_inputs(xc, wg_ref, rgv_ref):
    xcb = xc.astype(BF16)
    r_parts, i_parts = [], []
    for q in range(N_GATE_BLOCKS):
        g = _dot(xcb[:, q * MXU_DIM:(q + 1) * MXU_DIM], wg_ref[q])
        r_parts.append(g[:, :MXU_DIM])
        i_parts.append(g[:, MXU_DIM:])
    r = jax.nn.sigmoid(jnp.concatenate(r_parts, axis=1) + rgv_ref[5:6, :])
    i = jax.nn.sigmoid(jnp.concatenate(i_parts, axis=1) + rgv_ref[6:7, :])
    log_a = -RG_C * r * jax.nn.softplus(-rgv_ref[7:8, :])
    a = jnp.exp(log_a)
    mult = jnp.sqrt(1.0 - a * a)
    return a, mult * (i * xc)


def _mix_out(hs, rg_gate, sc_b, uc, scv_ref, w_out_ref):
    rg_out = _rms(hs * jax.nn.gelu(rg_gate), scv_ref[3:4, :]).astype(BF16)
    sc_out = _rms(sc_b * uc, scv_ref[4:5, :]).astype(BF16)
    return _dot(rg_out, w_out_ref[0:D_RG, :]) + _dot(sc_out, w_out_ref[D_RG:, :])


def _shift_rows(x, d, fill):
    m = x.shape[0]
    if d % SUBLANES == 0:
        head = jnp.full((d, x.shape[1]), fill, x.dtype)
        return jnp.concatenate([head, x[:m - d]], axis=0)
    rolled = pltpu.roll(x, d, axis=0)
    row = lax.broadcasted_iota(jnp.int32, x.shape, 0)
    return jnp.where(row >= d, rolled, fill)


def _scan_rows(a, b, h0):
    m = a.shape[0]
    d = 1
    while d < m:
        a_sh = _shift_rows(a, d, 1.0)
        b_sh = _shift_rows(b, d, 0.0)
        b = a * b_sh + b
        a = a * a_sh
        d *= 2
    return a * h0 + b


def _mixer_prompt_kernel(x_ref, gmix_ref, w_in_ref, wg_ref, rgv_ref, scv_ref, w_out_ref, *rest,
                         n_riders):
    rider_src, rest = rest[:n_riders], rest[n_riders:]
    o_ref, conv_ref, h_ref, sc_ref = rest[:4]
    rider_dst, (rgx, usc, hcar) = rest[4:4 + n_riders], rest[4 + n_riders:]
    _round_riders(rider_src, rider_dst)
    t = pl.program_id(1)
    tm = x_ref.shape[0]

    @pl.when(t == 0)
    def _():
        rgx[0:SUBLANES, :] = jnp.zeros((SUBLANES, D_RG), F32)
        usc[0:SUBLANES, :] = jnp.zeros((SUBLANES, D_SC), F32)
        hcar[...] = jnp.zeros_like(hcar)

    x = x_ref[...]
    xn = _rms(x, gmix_ref[...]).astype(BF16)

    rg_x = _dot(xn, w_in_ref[:, 0:D_RG])
    rgx[SUBLANES:SUBLANES + tm, :] = rg_x
    xc = rgx[5:5 + tm, :] * rgv_ref[0:1, :]
    xc = xc + rgx[6:6 + tm, :] * rgv_ref[1:2, :]
    xc = xc + rgx[7:7 + tm, :] * rgv_ref[2:3, :]
    xc = xc + rg_x * rgv_ref[3:4, :]
    xc = xc + rgv_ref[4:5, :]
    a, b = _rg_gate_inputs(xc, wg_ref, rgv_ref)
    hs = _scan_rows(a, b, hcar[0:1, :])
    hcar[0:1, :] = hs[tm - 1:tm, :]

    rg_gate = _dot(xn, w_in_ref[:, D_RG:2 * D_RG])
    sc_b = _dot(xn, w_in_ref[:, 2 * D_RG:2 * D_RG + D_SC])
    sc_c = _dot(xn, w_in_ref[:, 2 * D_RG + D_SC:2 * D_RG + 2 * D_SC])
    sc_x = _dot(xn, w_in_ref[:, 2 * D_RG + 2 * D_SC:])
    u = sc_c * sc_x
    usc[SUBLANES:SUBLANES + tm, :] = u
    uc = usc[6:6 + tm, :] * scv_ref[0:1, :]
    uc = uc + usc[7:7 + tm, :] * scv_ref[1:2, :]
    uc = uc + u * scv_ref[2:3, :]

    o_ref[...] = x + _mix_out(hs, rg_gate, sc_b, uc, scv_ref, w_out_ref)

    @pl.when(t == pl.num_programs(1) - 1)
    def _():
        conv_ref[0] = rgx[tm + 5:tm + 8, :]
        sc_ref[0] = usc[tm + 6:tm + 8, :]
        h_ref[0] = hs[tm - 1:tm, :]

    rgx[0:SUBLANES, :] = rgx[tm:tm + SUBLANES, :]
    usc[0:SUBLANES, :] = usc[tm:tm + SUBLANES, :]


def _mixer_weight_specs(l):
    return [
        _layer_resident((1, D_MODEL), l),
        _whole_resident((D_MODEL, D_IN)),
        _layer_resident((N_GATE_BLOCKS, MXU_DIM, 2 * MXU_DIM), l),
        _layer_resident((8, D_RG), l),
        _layer_resident((8, D_SC), l),
        _layer_resident((D_MODEL, D_MODEL), l),
    ]


def _mixer_prompt(x, l, pw, w_in, riders):
    nt = SEQ // TM_MIX
    r_in, r_out, r_shape = _rider_specs(riders, l, BATCH * nt, lambda b, t: b * nt + t)
    return pl.pallas_call(
        functools.partial(_mixer_prompt_kernel, n_riders=len(riders)),
        grid=(BATCH, nt),
        in_specs=[pl.BlockSpec((TM_MIX, D_MODEL), lambda b, t: (b * nt + t, 0))]
        + _mixer_weight_specs(l) + r_in,
        out_specs=[
            pl.BlockSpec((TM_MIX, D_MODEL), lambda b, t: (b * nt + t, 0)),
            pl.BlockSpec((1, RG_CONV_W - 1, D_RG), lambda b, t: (b, 0, 0)),
            pl.BlockSpec((1, 1, D_RG), lambda b, t: (b, 0, 0)),
            pl.BlockSpec((1, SC_CONV_W - 1, D_SC), lambda b, t: (b, 0, 0)),
        ] + r_out,
        out_shape=[
            jax.ShapeDtypeStruct((N_PROMPT, D_MODEL), F32),
            jax.ShapeDtypeStruct((BATCH, RG_CONV_W - 1, D_RG), F32),
            jax.ShapeDtypeStruct((BATCH, 1, D_RG), F32),
            jax.ShapeDtypeStruct((BATCH, SC_CONV_W - 1, D_SC), F32),
        ] + r_shape,
        scratch_shapes=[
            pltpu.VMEM((SUBLANES + TM_MIX, D_RG), F32),
            pltpu.VMEM((SUBLANES + TM_MIX, D_SC), F32),
            pltpu.VMEM((SUBLANES, D_RG), F32),
        ],
        compiler_params=_params(("arbitrary", "arbitrary")),
        name="mixer_prompt",
    )(x, pw["gmix"], w_in, pw["wg"], pw["rgv"], pw["scv"], pw["w_out"], *riders)


def _mixer_sample_kernel(x_ref, gmix_ref, w_in_ref, wg_ref, rgv_ref, scv_ref, w_out_ref,
                         conv_in_ref, h_in_ref, sc_in_ref,
                         o_ref, conv_ref, h_ref, sc_ref, xcs, hss, ucs):
    nb = DEC_BATCH
    x = x_ref[...]
    xn = _rms(x, gmix_ref[...]).astype(BF16)

    rg_x = _dot(xn, w_in_ref[:, 0:D_RG])
    seq = [conv_in_ref[k] for k in range(RG_CONV_W - 1)]
    seq += [rg_x[t * nb:(t + 1) * nb, :] for t in range(DEC_SEQ)]
    for t in range(DEC_SEQ):
        xc_t = seq[t] * rgv_ref[0:1, :]
        for k in range(1, RG_CONV_W):
            xc_t = xc_t + seq[t + k] * rgv_ref[k:k + 1, :]
        xcs[t * nb:(t + 1) * nb, :] = xc_t + rgv_ref[4:5, :]
    for k in range(RG_CONV_W - 1):
        conv_ref[k] = seq[DEC_SEQ + k]

    a, b = _rg_gate_inputs(xcs[...], wg_ref, rgv_ref)
    h = h_in_ref[...]
    for t in range(DEC_SEQ):
        h = a[t * nb:(t + 1) * nb, :] * h + b[t * nb:(t + 1) * nb, :]
        hss[t * nb:(t + 1) * nb, :] = h
    h_ref[...] = h

    rg_gate = _dot(xn, w_in_ref[:, D_RG:2 * D_RG])
    sc_b = _dot(xn, w_in_ref[:, 2 * D_RG:2 * D_RG + D_SC])
    sc_c = _dot(xn, w_in_ref[:, 2 * D_RG + D_SC:2 * D_RG + 2 * D_SC])
    sc_x = _dot(xn, w_in_ref[:, 2 * D_RG + 2 * D_SC:])
    u = sc_c * sc_x
    useq = [sc_in_ref[:, k * D_SC:(k + 1) * D_SC] for k in range(SC_CONV_W - 1)]
    useq += [u[t * nb:(t + 1) * nb, :] for t in range(DEC_SEQ)]
    for t in range(DEC_SEQ):
        uc_t = useq[t] * scv_ref[0:1, :]
        for k in range(1, SC_CONV_W):
            uc_t = uc_t + useq[t + k] * scv_ref[k:k + 1, :]
        ucs[t * nb:(t + 1) * nb, :] = uc_t
    for k in range(SC_CONV_W - 1):
        sc_ref[:, k * D_SC:(k + 1) * D_SC] = useq[DEC_SEQ + k]

    o_ref[...] = x + _mix_out(hss[...], rg_gate, sc_b, ucs[...], scv_ref, w_out_ref)


def _mixer_sample(x, l, pw, w_in, conv_in, h_in, sc_in):
    conv_shape = (RG_CONV_W - 1, DEC_BATCH, D_RG)
    sc_w = (SC_CONV_W - 1) * D_SC
    return pl.pallas_call(
        _mixer_sample_kernel,
        grid=(1,),
        in_specs=[
            pl.BlockSpec((N_SAMPLE, D_MODEL), lambda i: (0, 0)),
        ] + _mixer_weight_specs(l) + [
            _layer_resident(conv_shape, l),
            _layer_resident((DEC_BATCH, D_RG), l),
            _layer_resident((DEC_BATCH, sc_w), l),
        ],
        out_specs=[
            pl.BlockSpec((N_SAMPLE, D_MODEL), lambda i: (0, 0)),
            pl.BlockSpec(conv_shape, lambda i: (0, 0, 0)),
            pl.BlockSpec((DEC_BATCH, D_RG), lambda i: (0, 0)),
            pl.BlockSpec((DEC_BATCH, sc_w), lambda i: (0, 0)),
        ],
        out_shape=[
            jax.ShapeDtypeStruct((N_SAMPLE, D_MODEL), F32),
            jax.ShapeDtypeStruct(conv_shape, F32),
            jax.ShapeDtypeStruct((DEC_BATCH, D_RG), F32),
            jax.ShapeDtypeStruct((DEC_BATCH, sc_w), F32),
        ],
        scratch_shapes=[
            pltpu.VMEM((N_SAMPLE, D_RG), F32),
            pltpu.VMEM((N_SAMPLE, D_RG), F32),
            pltpu.VMEM((N_SAMPLE, D_SC), F32),
        ],
        compiler_params=_params(("arbitrary",)),
        name="mixer_sample",
    )(x, pw["gmix"], w_in, pw["wg"], pw["rgv"], pw["scv"], pw["w_out"],
      conv_in, h_in, sc_in)


def _mem_kv_kernel(m_ref, g_ref, wk_ref, wv_ref, *rest, n_riders):
    rider_src, rest = rest[:n_riders], rest[n_riders:]
    k_ref, v_ref = rest[:2]
    rider_dst, (mn_s,) = rest[2:2 + n_riders], rest[2 + n_riders:]
    _round_riders(rider_src, rider_dst)

    @pl.when(pl.program_id(1) == 0)
    def _():
        mn_s[...] = _rms(m_ref[...], g_ref[...]).astype(BF16)

    mn = mn_s[...]
    k_ref[...] = _dot(mn, wk_ref[...].astype(BF16))
    v_ref[...] = _dot(mn, wv_ref[...].astype(BF16))


def _mem_kv(mem, gmem, w_k, w_v, riders, tn=512):
    rows = BATCH * N_MEM
    nj = D_MODEL // tn
    w_spec = pl.BlockSpec((None, D_MODEL, tn), lambda l, j: (l, 0, j))
    o_spec = pl.BlockSpec((None, rows, tn), lambda l, j: (l, 0, j))
    shape = jax.ShapeDtypeStruct((DEPTH, rows, D_MODEL), F32)
    r_in, r_out, r_shape = _rider_specs(riders, 0, DEPTH * nj, lambda l, j: l * nj + j)
    return pl.pallas_call(
        functools.partial(_mem_kv_kernel, n_riders=len(riders)),
        grid=(DEPTH, nj),
        in_specs=[
            pl.BlockSpec((rows, D_MODEL), lambda l, j: (0, 0), pipeline_mode=pl.Buffered(1)),
            pl.BlockSpec((None, 1, D_MODEL), lambda l, j: (l, 0, 0)),
            w_spec, w_spec,
        ] + r_in,
        out_specs=[o_spec, o_spec] + r_out,
        out_shape=[shape, shape] + r_shape,
        scratch_shapes=[pltpu.VMEM((rows, D_MODEL), BF16)],
        compiler_params=_params(("arbitrary",) * 2),
        name="mem_kv",
    )(mem, gmem, w_k, w_v, *riders)


def _q_sample_kernel(x_ref, g_ref, w_ref, o_ref):
    xn = _rms(x_ref[...], g_ref[...]).astype(BF16)
    o_ref[...] = _dot(xn, w_ref[...])


def _q_sample(x, l, pw, wq, tn=1024):
    return pl.pallas_call(
        _q_sample_kernel,
        grid=(D_MODEL // tn,),
        in_specs=[
            pl.BlockSpec((N_SAMPLE, D_MODEL), lambda j: (0, 0)),
            _layer_block((1, D_MODEL), l),
            pl.BlockSpec((D_MODEL, tn), lambda j: (0, j)),
        ],
        out_specs=pl.BlockSpec((N_SAMPLE, tn), lambda j: (0, j)),
        out_shape=jax.ShapeDtypeStruct((N_SAMPLE, D_MODEL), F32),
        compiler_params=_params(("arbitrary",)),
        name="q_sample",
    )(x, pw["gxa"], wq)


def _oproj_sample_kernel(a_ref, w_ref, res_ref, o_ref):
    o_ref[...] = res_ref[...] + _dot(a_ref[...].astype(BF16), w_ref[...])


def _oproj_sample(a, wo, res, tn=1024):
    return pl.pallas_call(
        _oproj_sample_kernel,
        grid=(D_MODEL // tn,),
        in_specs=[
            pl.BlockSpec((N_SAMPLE, D_MODEL), lambda j: (0, 0)),
            pl.BlockSpec((D_MODEL, tn), lambda j: (0, j)),
            pl.BlockSpec((N_SAMPLE, tn), lambda j: (0, j)),
        ],
        out_specs=pl.BlockSpec((N_SAMPLE, tn), lambda j: (0, j)),
        out_shape=jax.ShapeDtypeStruct((N_SAMPLE, D_MODEL), F32),
        compiler_params=_params(("arbitrary",)),
        name="oproj_sample",
    )(a, wo, res)


def _softmax_rows(s):
    m = jnp.max(s, axis=-1, keepdims=True)
    e = jnp.exp(s - m)
    return e / jnp.sum(e, axis=-1, keepdims=True)


def _attn_prompt_kernel(x_ref, g_ref, wq_ref, k_ref, v_ref, wo_ref, *rest, n_riders):
    rider_src, o_ref, rider_dst = rest[:n_riders], rest[n_riders], rest[n_riders + 1:]
    _round_riders(rider_src, rider_dst)
    x = x_ref[...]
    xn = _rms(x, g_ref[...]).astype(BF16)
    q = _dot(xn, wq_ref[...]).astype(BF16)
    k = k_ref[...].astype(BF16)
    v = v_ref[...].astype(BF16)
    outs = []
    for h in range(XA_HEADS):
        sl = slice(h * XA_HEAD_DIM, (h + 1) * XA_HEAD_DIM)
        s = lax.dot_general(q[:, sl], k[:, sl], (((1,), (1,)), ((), ())),
                            preferred_element_type=F32) / math.sqrt(XA_HEAD_DIM)
        p = _softmax_rows(s).astype(BF16)
        outs.append(_dot(p, v[:, sl]).astype(BF16))
    o = jnp.concatenate(outs, axis=1)
    o_ref[...] = x + _dot(o, wo_ref[...])


def _attn_prompt(x, l, pw, wq, wo, k, v, riders, rider_layer):
    nt = SEQ // TM_ATT
    kv_spec = pl.BlockSpec((None, N_MEM, D_MODEL), lambda b, t: (l, b, 0))
    r_in, r_out, r_shape = _rider_specs(riders, rider_layer, BATCH * nt, lambda b, t: b * nt + t)
    return pl.pallas_call(
        functools.partial(_attn_prompt_kernel, n_riders=len(riders)),
        grid=(BATCH, nt),
        in_specs=[
            pl.BlockSpec((TM_ATT, D_MODEL), lambda b, t: (b * nt + t, 0)),
            _layer_resident((1, D_MODEL), l),
            _whole_resident((D_MODEL, D_MODEL)),
            kv_spec, kv_spec,
            _whole_resident((D_MODEL, D_MODEL)),
        ] + r_in,
        out_specs=[pl.BlockSpec((TM_ATT, D_MODEL), lambda b, t: (b * nt + t, 0))] + r_out,
        out_shape=[jax.ShapeDtypeStruct((N_PROMPT, D_MODEL), F32)] + r_shape,
        compiler_params=_params(("arbitrary", "arbitrary")),
        name="attn_prompt",
    )(x, pw["gxa"], wq, k, v, wo, *riders)


def _head_matrix(mem_ref, stage_ref, b, h):
    n = N_MEM * HD_CHUNKS
    stage_ref[...] = mem_ref[pl.ds(b * n * XA_HEADS + h, n, stride=XA_HEADS), :]
    chunks = [stage_ref[pl.ds(c, N_MEM, stride=HD_CHUNKS), :] for c in range(HD_CHUNKS)]
    return jnp.concatenate(chunks, axis=1).astype(BF16)


def _attn_sample_kernel(q_ref, k_ref, v_ref, o_ref, stage_ref):
    rows = DEC_SEQ * ATT_BB
    keys = ATT_BB * N_MEM
    row_b = lax.broadcasted_iota(jnp.int32, (rows, keys), 0) // DEC_SEQ
    key_b = lax.broadcasted_iota(jnp.int32, (rows, keys), 1) // N_MEM
    own = row_b == key_b
    outs = []
    for h in range(XA_HEADS):
        q = q_ref[:, h * XA_HEAD_DIM:(h + 1) * XA_HEAD_DIM].astype(BF16)
        k = jnp.concatenate([_head_matrix(k_ref, stage_ref, b, h) for b in range(ATT_BB)], axis=0)
        v = jnp.concatenate([_head_matrix(v_ref, stage_ref, b, h) for b in range(ATT_BB)], axis=0)
        s = lax.dot_general(q, k, (((1,), (1,)), ((), ())),
                            preferred_element_type=F32) / math.sqrt(XA_HEAD_DIM)
        s = jnp.where(own, s, -jnp.inf)
        m = jnp.max(s, axis=-1, keepdims=True)
        e = jnp.where(own, jnp.exp(s - m), 0.0)
        p = (e / jnp.sum(e, axis=-1, keepdims=True)).astype(BF16)
        outs.append(_dot(p, v))
    o_ref[...] = jnp.concatenate(outs, axis=1)


def _lane_pieces(cache):
    c = cache.reshape(DEPTH, DEC_BATCH, N_MEM, XA_HEADS, HD_CHUNKS, LANES)
    c = jnp.transpose(c, (0, 1, 2, 4, 3, 5))
    return c.reshape(DEPTH * DEC_BATCH * N_MEM * HD_CHUNKS * XA_HEADS, LANES)


def _attn_sample(q, l, k, v):
    rows = ATT_BB * DEC_SEQ
    blk = ATT_BB * N_MEM * HD_CHUNKS * XA_HEADS
    nb = DEC_BATCH // ATT_BB
    kv_spec = pl.BlockSpec((blk, LANES), lambda i: (l * nb + i, 0))
    return pl.pallas_call(
        _attn_sample_kernel,
        grid=(nb,),
        in_specs=[pl.BlockSpec((rows, D_MODEL), lambda i: (i, 0)), kv_spec, kv_spec],
        out_specs=pl.BlockSpec((rows, D_MODEL), lambda i: (i, 0)),
        out_shape=jax.ShapeDtypeStruct((N_SAMPLE, D_MODEL), F32),
        scratch_shapes=[pltpu.VMEM((N_MEM * HD_CHUNKS, LANES), F32)],
        compiler_params=_params(("arbitrary",)),
        name="attn_sample",
    )(q, k, v)


def _route(logits):
    lane = lax.broadcasted_iota(jnp.int32, logits.shape, 1)
    neg = -jnp.inf
    is_g = lane < N_GROUPS
    lg = jnp.where(is_g, logits, neg)
    mg = jnp.max(lg, axis=-1, keepdims=True)
    eg = jnp.where(is_g, jnp.exp(lg - mg), 0.0)
    p_g = eg / jnp.sum(eg, axis=-1, keepdims=True)
    pg_sel = jnp.max(p_g, axis=-1, keepdims=True)
    g_sel = jnp.min(jnp.where(p_g == pg_sel, lane, LANES), axis=-1, keepdims=True)

    lo = N_GROUPS + g_sel * EXPERTS_PER_GROUP
    in_grp = (lane >= lo) & (lane < lo + EXPERTS_PER_GROUP)
    le = jnp.where(in_grp, logits, neg)
    me = jnp.max(le, axis=-1, keepdims=True)
    ee = jnp.where(in_grp, jnp.exp(le - me), 0.0)
    p_e = ee / jnp.sum(ee, axis=-1, keepdims=True)

    p1 = jnp.max(p_e, axis=-1, keepdims=True)
    i1 = jnp.min(jnp.where(in_grp & (p_e == p1), lane, LANES), axis=-1, keepdims=True)
    rest = in_grp & (lane != i1)
    p2 = jnp.max(jnp.where(rest, p_e, neg), axis=-1, keepdims=True)
    i2 = jnp.min(jnp.where(rest & (p_e == p2), lane, LANES), axis=-1, keepdims=True)
    tot = p1 + p2
    w1 = p1 / tot * pg_sel
    w2 = p2 / tot * pg_sel
    gates = jnp.where(lane == i1, w1, 0.0) + jnp.where(lane == i2, w2, 0.0)

    local = jnp.zeros_like(gates)
    for g in range(N_GROUPS):
        start = N_GROUPS + g * EXPERTS_PER_GROUP
        local = local + jnp.where(lane < EXPERTS_PER_GROUP,
                                  pltpu.roll(gates, LANES - start, axis=1), 0.0)

    ja = jnp.minimum(i1, i2) - lo
    jb = jnp.maximum(i1, i2) - lo
    code = ja * EXPERTS_PER_GROUP + jb
    pair = jnp.full_like(code, N_PAIRS - 1)
    for p in range(N_PAIRS - 1):
        lo_j, hi_j = min(PAIR_A[p], PAIR_B[p]), max(PAIR_A[p], PAIR_B[p])
        pair = jnp.where(code == lo_j * EXPERTS_PER_GROUP + hi_j, p, pair)
    bucket = (g_sel * N_PAIRS + pair).astype(F32)
    return local + jnp.where(lane == BUCKET_LANE, bucket, 0.0)


def _router_kernel(x_ref, g_ref, whi_ref, wlo_ref, br_ref, o_ref):
    xn = _rms(x_ref[...], g_ref[...])
    hi = xn.astype(BF16)
    lo = (xn - hi.astype(F32)).astype(BF16)
    w_hi = whi_ref[...]
    logits = _dot(hi, w_hi) + _dot(lo, w_hi) + _dot(hi, wlo_ref[...]) + br_ref[...]
    o_ref[...] = _route(logits)


def _router_riders_kernel(x_ref, g_ref, whi_ref, wlo_ref, br_ref, *rest, n_riders):
    rider_src, o_ref, rider_dst = rest[:n_riders], rest[n_riders], rest[n_riders + 1:]
    _round_riders(rider_src, rider_dst)
    _router_kernel(x_ref, g_ref, whi_ref, wlo_ref, br_ref, o_ref)


def _router_specs(l):
    return [
        pl.BlockSpec((TM_ROUTE, D_MODEL), lambda i: (i, 0)),
        _layer_block((1, D_MODEL), l),
        _layer_block((D_MODEL, LANES), l),
        _layer_block((D_MODEL, LANES), l),
        _layer_block((1, LANES), l),
    ]


def _router(x, l, pw):
    n_rows = x.shape[0]
    return pl.pallas_call(
        _router_kernel,
        grid=(n_rows // TM_ROUTE,),
        in_specs=_router_specs(l),
        out_specs=pl.BlockSpec((TM_ROUTE, LANES), lambda i: (i, 0)),
        out_shape=jax.ShapeDtypeStruct((n_rows, LANES), F32),
        compiler_params=_params(("arbitrary",)),
        name="router",
    )(x, pw["gmoe"], pw["wr_hi"], pw["wr_lo"], pw["br"])


def _router_prompt(x, l, pw, riders):
    n_steps = N_PROMPT // TM_ROUTE
    r_in, r_out, r_shape = _rider_specs(riders, l, n_steps, lambda i: i)
    return pl.pallas_call(
        functools.partial(_router_riders_kernel, n_riders=len(riders)),
        grid=(n_steps,),
        in_specs=_router_specs(l) + r_in,
        out_specs=[pl.BlockSpec((TM_ROUTE, LANES), lambda i: (i, 0))] + r_out,
        out_shape=[jax.ShapeDtypeStruct((N_PROMPT, LANES), F32)] + r_shape,
        compiler_params=_params(("arbitrary",)),
        name="router_prompt",
    )(x, pw["gmoe"], pw["wr_hi"], pw["wr_lo"], pw["br"], *riders)


def _dispatch_tables(rinfo):
    i32 = jnp.int32
    bucket = rinfo[:, BUCKET_LANE].astype(i32)
    bucket_ids = jnp.arange(N_BUCKETS, dtype=i32)
    onehot = (bucket[:, None] == bucket_ids[None, :]).astype(i32)
    csum = jnp.cumsum(onehot, axis=0)
    counts = csum[-1]
    tiles_per = (counts + TG - 1) // TG
    tile_end = jnp.cumsum(tiles_per)
    tile_start = tile_end - tiles_per
    slot = jnp.sum(onehot * (tile_start[None, :] * TG + csum - 1), axis=1)

    tok1 = jnp.arange(1, N_TOK + 1, dtype=i32).astype(F32)
    payload = jnp.concatenate([tok1[:, None], rinfo[:, :EXPERTS_PER_GROUP]], axis=1)
    slots = jnp.zeros((N_SLOTS, 1 + EXPERTS_PER_GROUP), F32).at[slot].set(
        payload, unique_indices=True)
    slot_tok = slots[:, 0].astype(i32) - 1

    tile = jnp.arange(N_TILES, dtype=i32)
    n_valid = tile_end[-1]
    tq = jnp.minimum(tile, n_valid - 1)
    tile_bucket = jnp.sum((tile_end[None, :] <= tq[:, None]).astype(i32), axis=1)
    tile_oh = (tile_bucket[:, None] == bucket_ids[None, :]).astype(i32)
    in_bucket = tile - jnp.sum(tile_oh * tile_start[None, :], axis=1)
    tile_cnt = jnp.clip(jnp.sum(tile_oh * counts[None, :], axis=1) - in_bucket * TG, 0, TG)
    tile_cnt = jnp.where(tile < n_valid, tile_cnt, 0)
    pair_a = jnp.asarray([(b // N_PAIRS) * EXPERTS_PER_GROUP + PAIR_A[b % N_PAIRS]
                          for b in range(N_BUCKETS)], i32)
    pair_b = jnp.asarray([(b // N_PAIRS) * EXPERTS_PER_GROUP + PAIR_B[b % N_PAIRS]
                          for b in range(N_BUCKETS)], i32)
    tile_ea = jnp.sum(tile_oh * pair_a[None, :], axis=1)
    tile_eb = jnp.sum(tile_oh * pair_b[None, :], axis=1)
    is_prompt = (slot_tok >= 0) & (slot_tok < N_PROMPT)
    tile_np = jnp.sum(is_prompt.reshape(N_TILES, TG).astype(i32), axis=1)
    return tile_ea, tile_eb, tile_cnt, tile_np, slot_tok.reshape(N_TILES, 1, TG), slots


def _experts_kernel(ea_ref, eb_ref, cnt_ref, np_ref,
                    xp_hbm, xs_hbm, tokp_ref, tokc_ref, tokn_ref, gates_ref, g_ref,
                    wga_ref, wua_ref, wda_ref, wgb_ref, wub_ref, wdb_ref, gfin_ref,
                    op_hbm, os_hbm, xbuf, obuf, gsem, ssem, *, final_norm, sample_batch_major):
    t = pl.program_id(0)
    nt = pl.num_programs(0)
    buf = lax.rem(t, 2)
    t_prev = jnp.maximum(t - 1, 0)
    t_next = jnp.minimum(t + 1, nt - 1)
    cnt = cnt_ref[t]
    has_next = (t + 1 < nt) & (cnt_ref[t_next] > 0)

    def sample_dst(r):
        if not sample_batch_major:
            return r
        return (r % DEC_BATCH) * DEC_SEQ + r // DEC_BATCH

    def row_slot(r):
        return lax.shift_right_logical(r, 3), lax.bitwise_and(r, SUBLANES - 1)

    def gather_p(tok_ref, b, r, g, s):
        return pltpu.make_async_copy(
            xp_hbm.at[pl.ds(tok_ref[0, r], 1)], xbuf.at[b, g, pl.ds(s, 1)], gsem.at[b])

    def gather_s(tok_ref, b, r, g, s):
        return pltpu.make_async_copy(
            xs_hbm.at[pl.ds(tok_ref[0, r] - N_PROMPT, 1)], xbuf.at[b, g, pl.ds(s, 1)], gsem.at[b])

    def scatter_p(tok_ref, b, r, g, s):
        return pltpu.make_async_copy(
            obuf.at[b, g, pl.ds(s, 1)], op_hbm.at[pl.ds(tok_ref[0, r], 1)], ssem.at[b])

    def scatter_s(tok_ref, b, r, g, s):
        dst = sample_dst(tok_ref[0, r] - N_PROMPT)
        return pltpu.make_async_copy(
            obuf.at[b, g, pl.ds(s, 1)], os_hbm.at[pl.ds(dst, 1)], ssem.at[b])

    def for_range(lo, hi, make_copy, op):
        full = (hi - lo) // SUBLANES
        aligned = isinstance(lo, int) and lo == 0

        def group(i, c):
            for j in range(SUBLANES):
                r = lo + i * SUBLANES + j
                op(make_copy(r, i, j) if aligned else make_copy(r, *row_slot(r)))
            return c

        def single(r, c):
            op(make_copy(r, *row_slot(r)))
            return c
        lax.fori_loop(0, full, group, 0)
        lax.fori_loop(lo + full * SUBLANES, hi, single, 0)

    def gather(tile, tok_ref, b, op):
        for_range(0, np_ref[tile], functools.partial(gather_p, tok_ref, b), op)
        for_range(np_ref[tile], cnt_ref[tile], functools.partial(gather_s, tok_ref, b), op)

    def scatter(tile, tok_ref, b, op):
        for_range(0, np_ref[tile], functools.partial(scatter_p, tok_ref, b), op)
        for_range(np_ref[tile], cnt_ref[tile], functools.partial(scatter_s, tok_ref, b), op)

    def start(copy):
        copy.start()

    def wait(copy):
        copy.wait()

    @pl.when(t == 0)
    def _():
        xbuf[...] = jnp.zeros_like(xbuf)
        gather(t, tokc_ref, 0, start)

    @pl.when(has_next)
    def _():
        gather(t_next, tokn_ref, 1 - buf, start)

    @pl.when(cnt > 0)
    def _():
        gather(t, tokc_ref, buf, wait)

        ja = ea_ref[t] % EXPERTS_PER_GROUP + SLOT_GATE_LANE
        jb = eb_ref[t] % EXPERTS_PER_GROUP + SLOT_GATE_LANE
        n_chunks = (cnt + ROW_CHUNK - 1) // ROW_CHUNK

        def compute(m):
            x = xbuf[buf, 0:m // SUBLANES].reshape(m, D_MODEL)
            xn = _rms(x, g_ref[...]).astype(BF16)
            gates = gates_ref[0:m, :]
            lane = lax.broadcasted_iota(jnp.int32, gates.shape, 1)
            ga = jnp.sum(jnp.where(lane == ja, gates, 0.0), axis=-1, keepdims=True)
            gb = jnp.sum(jnp.where(lane == jb, gates, 0.0), axis=-1, keepdims=True)
            act_a = (jax.nn.silu(_dot(xn, wga_ref[...])) * _dot(xn, wua_ref[...]) * ga).astype(BF16)
            act_b = (jax.nn.silu(_dot(xn, wgb_ref[...])) * _dot(xn, wub_ref[...]) * gb).astype(BF16)
            y = x + (_dot(act_a, wda_ref[...]) + _dot(act_b, wdb_ref[...]))
            if final_norm:
                y = _rms(y, gfin_ref[...])
            obuf[buf, 0:m // SUBLANES] = y.reshape(m // SUBLANES, SUBLANES, D_MODEL)

        for k in range(1, TG // ROW_CHUNK + 1):
            pl.when(n_chunks == k)(functools.partial(compute, k * ROW_CHUNK))

        @pl.when(t > 0)
        def _():
            scatter(t_prev, tokp_ref, 1 - buf, wait)

        scatter(t, tokc_ref, buf, start)

        @pl.when(jnp.logical_not(has_next))
        def _():
            scatter(t, tokc_ref, buf, wait)


def _experts(xp, xs, l, pw, expert_w, gfin, tables, final_norm):
    tile_ea, tile_eb, tile_cnt, tile_np, slot_tok, gates = tables
    w_gate, w_up, w_down = expert_w

    def tok_spec(shift):
        return pl.BlockSpec(
            (None, 1, TG),
            lambda t, *_: (jnp.clip(t + shift, 0, N_TILES - 1), 0, 0),
            memory_space=pltpu.SMEM)

    def w_spec(shape, which):
        return pl.BlockSpec(
            (None,) + shape,
            lambda t, ea, eb, *_: ((ea, eb)[which][t], 0, 0))

    any_spec = pl.BlockSpec(memory_space=pl.ANY)
    grid_spec = pltpu.PrefetchScalarGridSpec(
        num_scalar_prefetch=4,
        grid=(N_TILES,),
        in_specs=[
            any_spec, any_spec,
            tok_spec(-1), tok_spec(0), tok_spec(1),
            pl.BlockSpec((TG, SLOT_GATE_LANE + EXPERTS_PER_GROUP), lambda t, *_: (t, 0)),
            pl.BlockSpec((None, 1, D_MODEL), lambda t, *_: (l, 0, 0)),
            w_spec((D_MODEL, D_FF), 0), w_spec((D_MODEL, D_FF), 0), w_spec((D_FF, D_MODEL), 0),
            w_spec((D_MODEL, D_FF), 1), w_spec((D_MODEL, D_FF), 1), w_spec((D_FF, D_MODEL), 1),
            pl.BlockSpec((1, D_MODEL), lambda t, *_: (0, 0)),
        ],
        out_specs=[any_spec, any_spec],
        scratch_shapes=[
            pltpu.VMEM((2, TG // SUBLANES, SUBLANES, D_MODEL), F32),
            pltpu.VMEM((2, TG // SUBLANES, SUBLANES, D_MODEL), F32),
            pltpu.SemaphoreType.DMA((2,)),
            pltpu.SemaphoreType.DMA((2,)),
        ],
    )
    return pl.pallas_call(
        functools.partial(_experts_kernel, final_norm=final_norm, sample_batch_major=final_norm),
        grid_spec=grid_spec,
        out_shape=[jax.ShapeDtypeStruct((N_PROMPT, D_MODEL), F32),
                   jax.ShapeDtypeStruct((N_SAMPLE, D_MODEL), F32)],
        compiler_params=_params(("arbitrary",)),
        name="experts",
    )(tile_ea, tile_eb, tile_cnt, tile_np, xp, xs, slot_tok, slot_tok, slot_tok, gates,
      pw["gmoe"], w_gate, w_up, w_down, w_gate, w_up, w_down, gfin)


def _block_diag_gates(w_a, w_x):
    def diag(w):
        w = w.reshape(DEPTH, N_GATE_BLOCKS, HEADS_PER_GATE_BLOCK, RG_HEAD_DIM, RG_HEAD_DIM)
        eye = jnp.eye(HEADS_PER_GATE_BLOCK, dtype=w.dtype)
        full = jnp.einsum("lqhij,hk->lqhikj", w, eye)
        return full.reshape(DEPTH, N_GATE_BLOCKS, MXU_DIM, MXU_DIM)
    return jnp.concatenate([diag(w_a), diag(w_x)], axis=-1).astype(BF16)


def _to_batch_major(a):
    return jnp.transpose(a.reshape(DEC_SEQ, DEC_BATCH, -1), (1, 0, 2)).reshape(N_SAMPLE, -1)


def _to_time_major(a):
    return jnp.transpose(a.reshape(DEC_BATCH, DEC_SEQ, -1), (1, 0, 2)).reshape(N_SAMPLE, -1)


def _stack_rows(rows, n):
    width = rows[0].shape[-1]
    rows = [r.reshape(DEPTH, -1, width) for r in rows]
    have = sum(r.shape[1] for r in rows)
    if have < n:
        rows.append(jnp.zeros((DEPTH, n - have, width), F32))
    return jnp.concatenate(rows, axis=1)


def kernel(x_prompt, x_sample, state_rglru_h, state_rglru_conv, state_sconv, cache_mem_k,
           cache_mem_v, mem_prompt, norm_mix, w_in, rg_conv_w, rg_conv_b, rg_w_a, rg_b_a,
           rg_w_x, rg_b_x, rg_lambda, sc_conv_w, norm_rg_out, norm_sc_out, w_out, norm_xattn,
           norm_mem, xa_w_q, xa_w_k, xa_w_v, xa_w_o, norm_moe, router_group_w, router_group_b,
           router_expert_w, router_expert_b, expert_w_gate, expert_w_up, expert_w_down,
           norm_final):
    wr = jnp.concatenate([router_group_w, router_expert_w], axis=2)
    wr = jnp.pad(wr, ((0, 0), (0, 0), (0, LANES - wr.shape[2])))
    wr_hi = wr.astype(BF16)
    wr_lo = (wr - wr_hi.astype(F32)).astype(BF16)
    br = jnp.concatenate([router_group_b, router_expert_b], axis=1)
    br = jnp.pad(br, ((0, 0), (0, LANES - br.shape[1]))).reshape(DEPTH, 1, LANES)
    pw = dict(
        gmix=norm_mix.reshape(DEPTH, 1, D_MODEL),
        wg=_block_diag_gates(rg_w_a, rg_w_x),
        rgv=_stack_rows([rg_conv_w, rg_conv_b, rg_b_a, rg_b_x, rg_lambda], 8),
        scv=_stack_rows([sc_conv_w, norm_rg_out, norm_sc_out], 8),
        gxa=norm_xattn.reshape(DEPTH, 1, D_MODEL),
        gmem=norm_mem.reshape(DEPTH, 1, D_MODEL),
        gmoe=norm_moe.reshape(DEPTH, 1, D_MODEL),
        wr_hi=wr_hi, wr_lo=wr_lo, br=br,
    )
    gfin = norm_final.reshape(1, D_MODEL)

    conv_in = jnp.transpose(state_rglru_conv, (0, 2, 1, 3))
    sc_in = state_sconv.reshape(DEPTH, DEC_BATCH, (SC_CONV_W - 1) * D_SC)
    w_gate_rows = expert_w_gate.reshape(DEPTH, N_EXPERTS * D_MODEL, D_FF)
    w_up_rows = expert_w_up.reshape(DEPTH, N_EXPERTS * D_MODEL, D_FF)
    w_down_rows = expert_w_down.reshape(DEPTH, N_EXPERTS * D_FF, D_MODEL)
    w_in_l = w_in[0].astype(BF16)
    cache_k = _lane_pieces(cache_mem_k)
    cache_v = _lane_pieces(cache_mem_v)

    mem = mem_prompt.reshape(BATCH * N_MEM, D_MODEL)
    p_k, p_v, w_out_bf16 = _mem_kv(mem, pw["gmem"], xa_w_k, xa_w_v,
                                   [w_out.reshape(1, DEPTH * D_MODEL, D_MODEL)])
    pw["w_out"] = w_out_bf16.reshape(DEPTH, D_MODEL, D_MODEL)

    xp = x_prompt.reshape(N_PROMPT, D_MODEL)
    xs = jnp.transpose(x_sample, (1, 0, 2)).reshape(N_SAMPLE, D_MODEL)

    p_conv, p_h, p_sc, s_conv, s_h, s_sc = [], [], [], [], [], []
    for l in range(DEPTH):
        xp, c, hh, sc, w_down_l, wq_l, wo_l = _mixer_prompt(
            xp, l, pw, w_in_l, [w_down_rows, xa_w_q, xa_w_o])
        xs, cs, hs, scs = _mixer_sample(xs, l, pw, w_in_l, conv_in, state_rglru_h, sc_in)
        p_conv.append(c)
        p_h.append(hh.reshape(BATCH, D_RG))
        p_sc.append(sc)
        s_conv.append(cs)
        s_h.append(hs)
        s_sc.append(scs.reshape(DEC_BATCH, SC_CONV_W - 1, D_SC))

        next_w_in = [w_in] if l + 1 < DEPTH else []
        xp, *w_in_next = _attn_prompt(xp, l, pw, wq_l, wo_l, p_k, p_v, next_w_in, l + 1)
        q_s = _to_batch_major(_q_sample(xs, l, pw, wq_l))
        o_s = _attn_sample(q_s, l, cache_k, cache_v)
        xs = _oproj_sample(_to_time_major(o_s), wo_l, xs)
        if w_in_next:
            w_in_l = w_in_next[0]

        rinfo_p, w_gate_l, w_up_l = _router_prompt(xp, l, pw, [w_gate_rows, w_up_rows])
        rinfo = jnp.concatenate([rinfo_p, _router(xs, l, pw)], axis=0)
        expert_w = (w_gate_l.reshape(N_EXPERTS, D_MODEL, D_FF),
                    w_up_l.reshape(N_EXPERTS, D_MODEL, D_FF),
                    w_down_l.reshape(N_EXPERTS, D_FF, D_MODEL))
        xp, xs = _experts(xp, xs, l, pw, expert_w, gfin, _dispatch_tables(rinfo),
                          final_norm=(l == DEPTH - 1))

    y_prompt = xp.reshape(BATCH, SEQ, D_MODEL)
    y_sample = xs.reshape(DEC_BATCH, DEC_SEQ, D_MODEL)
    mem_shape = (DEPTH, BATCH, N_MEM, XA_HEADS, XA_HEAD_DIM)
    return (y_prompt, y_sample,
            jnp.stack(p_h), jnp.stack(p_conv), jnp.stack(p_sc),
            p_k.reshape(mem_shape), p_v.reshape(mem_shape),
            jnp.stack(s_h), jnp.transpose(jnp.stack(s_conv), (0, 2, 1, 3)), jnp.stack(s_sc))
```

```python
import functools
import math

import jax
import jax.numpy as jnp
from jax import lax
from jax.experimental import pallas as pl
from jax.experimental.pallas import tpu as pltpu

D_MODEL = 2048
BATCH = 4
SEQ = 2048
DEPTH = 2
DEC_BATCH = 128
DEC_SEQ = 4
D_RG = 1024
D_SC = 1024
RG_HEADS = 16
RG_HEAD_DIM = 64
RG_CONV_W = 4
RG_C = 8.0
SC_CONV_W = 3
D_IN = 2 * D_RG + 3 * D_SC
N_MEM = 256
XA_HEADS = 4
XA_HEAD_DIM = 512
N_GROUPS = 4
EXPERTS_PER_GROUP = 4
N_EXPERTS = 16
D_FF = 512
EPS = 1e-6

N_PROMPT = BATCH * SEQ
N_SAMPLE = DEC_BATCH * DEC_SEQ
N_TOK = N_PROMPT + N_SAMPLE

V7X_VMEM_LIMIT_BYTES = 56 * 1024 * 1024
SUBLANES = 8
LANES = 128
MXU_DIM = 256

HEADS_PER_GATE_BLOCK = MXU_DIM // RG_HEAD_DIM
N_GATE_BLOCKS = D_RG // MXU_DIM

TM_MIX = 256
TM_ATT = 512
TM_ROUTE = 512
ATT_BB = 4
HD_CHUNKS = XA_HEAD_DIM // LANES

PAIR_A = (0, 0, 0, 1, 2, 2)
PAIR_B = (1, 2, 3, 3, 3, 1)
N_PAIRS = len(PAIR_A)
N_BUCKETS = N_GROUPS * N_PAIRS
TG = 512
ROW_CHUNK = 128
N_TILES = -(-(N_TOK + N_BUCKETS * (TG - 1)) // TG)
N_SLOTS = N_TILES * TG
BUCKET_LANE = EXPERTS_PER_GROUP
SLOT_GATE_LANE = 1

BF16 = jnp.bfloat16
F32 = jnp.float32


def _params(sem, vmem=V7X_VMEM_LIMIT_BYTES):
    return pltpu.CompilerParams(dimension_semantics=sem, vmem_limit_bytes=vmem)


def _layer_resident(shape, l):
    nd = len(shape)
    return pl.BlockSpec((None,) + shape, lambda *_: (l,) + (0,) * nd, pipeline_mode=pl.Buffered(1))


def _layer_block(shape, l):
    nd = len(shape)
    return pl.BlockSpec((None,) + shape, lambda *_: (l,) + (0,) * nd)


def _whole_resident(shape):
    nd = len(shape)
    return pl.BlockSpec(shape, lambda *_: (0,) * nd, pipeline_mode=pl.Buffered(1))


def _rider_specs(srcs, l, n_steps, step_of):
    in_specs, out_specs, out_shapes = [], [], []
    for a in srcs:
        _, r, c = a.shape
        rows = r // n_steps
        assert rows * n_steps == r and rows % 16 == 0
        in_specs.append(pl.BlockSpec((None, rows, c), lambda *g: (l, step_of(*g), 0)))
        out_specs.append(pl.BlockSpec((rows, c), lambda *g: (step_of(*g), 0)))
        out_shapes.append(jax.ShapeDtypeStruct((r, c), BF16))
    return in_specs, out_specs, out_shapes


def _round_riders(src_refs, dst_refs):
    for src, dst in zip(src_refs, dst_refs):
        dst[...] = src[...].astype(BF16)


def _rms(x, g):
    return x * lax.rsqrt(jnp.mean(x * x, axis=-1, keepdims=True) + EPS) * g


def _dot(a, b):
    return jnp.dot(a, b, preferred_element_type=F32)


def _rg_gate_inputs(xc, wg_ref, rgv_ref):
    xcb = xc.astype(BF16)
    r_parts, i_parts = [], []
    for q in range(N_GATE_BLOCKS):
        g = _dot(xcb[:, q * MXU_DIM:(q + 1) * MXU_DIM], wg_ref[q])
        r_parts.append(g[:, :MXU_DIM])
        i_parts.append(g[:, MXU_DIM:])
    r = jax.nn.sigmoid(jnp.concatenate(r_parts, axis=1) + rgv_ref[5:6, :])
    i = jax.nn.sigmoid(jnp.concatenate(i_parts, axis=1) + rgv_ref[6:7, :])
    log_a = -RG_C * r * jax.nn.softplus(-rgv_ref[7:8, :])
    a = jnp.exp(log_a)
    mult = jnp.sqrt(1.0 - a * a)
    return a, mult * (i * xc)


def _mix_out(hs, rg_gate, sc_b, uc, scv_ref, w_out_ref):
    rg_out = _rms(hs * jax.nn.gelu(rg_gate), scv_ref[3:4, :]).astype(BF16)
    sc_out = _rms(sc_b * uc, scv_ref[4:5, :]).astype(BF16)
    return _dot(rg_out, w_out_ref[0:D_RG, :]) + _dot(sc_out, w_out_ref[D_RG:, :])


def _shift_rows(x, d, fill):
    m = x.shape[0]
    if d % SUBLANES == 0:
        head = jnp.full((d, x.shape[1]), fill, x.dtype)
        return jnp.concatenate([head, x[:m - d]], axis=0)
    rolled = pltpu.roll(x, d, axis=0)
    row = lax.broadcasted_iota(jnp.int32, x.shape, 0)
    return jnp.where(row >= d, rolled, fill)


def _scan_rows(a, b, h0):
    m = a.shape[0]
    d = 1
    while d < m:
        a_sh = _shift_rows(a, d, 1.0)
        b_sh = _shift_rows(b, d, 0.0)
        b = a * b_sh + b
        a = a * a_sh
        d *= 2
    return a * h0 + b


def _mixer_prompt_kernel(x_ref, gmix_ref, w_in_ref, wg_ref, rgv_ref, scv_ref, w_out_ref, *rest,
                         n_riders):
    rider_src, rest = rest[:n_riders], rest[n_riders:]
    o_ref, conv_ref, h_ref, sc_ref = rest[:4]
    rider_dst, (rgx, usc, hcar) = rest[4:4 + n_riders], rest[4 + n_riders:]
    _round_riders(rider_src, rider_dst)
    t = pl.program_id(1)
    tm = x_ref.shape[0]

    @pl.when(t == 0)
    def _():
        rgx[0:SUBLANES, :] = jnp.zeros((SUBLANES, D_RG), F32)
        usc[0:SUBLANES, :] = jnp.zeros((SUBLANES, D_SC), F32)
        hcar[...] = jnp.zeros_like(hcar)

    x = x_ref[...]
    xn = _rms(x, gmix_ref[...]).astype(BF16)

    rg_x = _dot(xn, w_in_ref[:, 0:D_RG])
    rgx[SUBLANES:SUBLANES + tm, :] = rg_x
    xc = rgx[5:5 + tm, :] * rgv_ref[0:1, :]
    xc = xc + rgx[6:6 + tm, :] * rgv_ref[1:2, :]
    xc = xc + rgx[7:7 + tm, :] * rgv_ref[2:3, :]
    xc = xc + rg_x * rgv_ref[3:4, :]
    xc = xc + rgv_ref[4:5, :]
    a, b = _rg_gate_inputs(xc, wg_ref, rgv_ref)
    hs = _scan_rows(a, b, hcar[0:1, :])
    hcar[0:1, :] = hs[tm - 1:tm, :]

    rg_gate = _dot(xn, w_in_ref[:, D_RG:2 * D_RG])
    sc_b = _dot(xn, w_in_ref[:, 2 * D_RG:2 * D_RG + D_SC])
    sc_c = _dot(xn, w_in_ref[:, 2 * D_RG + D_SC:2 * D_RG + 2 * D_SC])
    sc_x = _dot(xn, w_in_ref[:, 2 * D_RG + 2 * D_SC:])
    u = sc_c * sc_x
    usc[SUBLANES:SUBLANES + tm, :] = u
    uc = usc[6:6 + tm, :] * scv_ref[0:1, :]
    uc = uc + usc[7:7 + tm, :] * scv_ref[1:2, :]
    uc = uc + u * scv_ref[2:3, :]

    o_ref[...] = x + _mix_out(hs, rg_gate, sc_b, uc, scv_ref, w_out_ref)

    @pl.when(t == pl.num_programs(1) - 1)
    def _():
        conv_ref[0] = rgx[tm + 5:tm + 8, :]
        sc_ref[0] = usc[tm + 6:tm + 8, :]
        h_ref[0] = hs[tm - 1:tm, :]

    rgx[0:SUBLANES, :] = rgx[tm:tm + SUBLANES, :]
    usc[0:SUBLANES, :] = usc[tm:tm + SUBLANES, :]


def _mixer_weight_specs(l):
    return [
        _layer_resident((1, D_MODEL), l),
        _whole_resident((D_MODEL, D_IN)),
        _layer_resident((N_GATE_BLOCKS, MXU_DIM, 2 * MXU_DIM), l),
        _layer_resident((8, D_RG), l),
        _layer_resident((8, D_SC), l),
        _layer_resident((D_MODEL, D_MODEL), l),
    ]


def _mixer_prompt(x, l, pw, w_in, riders):
    nt = SEQ // TM_MIX
    r_in, r_out, r_shape = _rider_specs(riders, l, BATCH * nt, lambda b, t: b * nt + t)
    return pl.pallas_call(
        functools.partial(_mixer_prompt_kernel, n_riders=len(riders)),
        grid=(BATCH, nt),
        in_specs=[pl.BlockSpec((TM_MIX, D_MODEL), lambda b, t: (b * nt + t, 0))]
        + _mixer_weight_specs(l) + r_in,
        out_specs=[
            pl.BlockSpec((TM_MIX, D_MODEL), lambda b, t: (b * nt + t, 0)),
            pl.BlockSpec((1, RG_CONV_W - 1, D_RG), lambda b, t: (b, 0, 0)),
            pl.BlockSpec((1, 1, D_RG), lambda b, t: (b, 0, 0)),
            pl.BlockSpec((1, SC_CONV_W - 1, D_SC), lambda b, t: (b, 0, 0)),
        ] + r_out,
        out_shape=[
            jax.ShapeDtypeStruct((N_PROMPT, D_MODEL), F32),
            jax.ShapeDtypeStruct((BATCH, RG_CONV_W - 1, D_RG), F32),
            jax.ShapeDtypeStruct((BATCH, 1, D_RG), F32),
            jax.ShapeDtypeStruct((BATCH, SC_CONV_W - 1, D_SC), F32),
        ] + r_shape,
        scratch_shapes=[
            pltpu.VMEM((SUBLANES + TM_MIX, D_RG), F32),
            pltpu.VMEM((SUBLANES + TM_MIX, D_SC), F32),
            pltpu.VMEM((SUBLANES, D_RG), F32),
        ],
        compiler_params=_params(("arbitrary", "arbitrary")),
        name="mixer_prompt",
    )(x, pw["gmix"], w_in, pw["wg"], pw["rgv"], pw["scv"], pw["w_out"], *riders)


def _mixer_sample_kernel(x_ref, gmix_ref, w_in_ref, wg_ref, rgv_ref, scv_ref, w_out_ref,
                         conv_in_ref, h_in_ref, sc_in_ref,
                         o_ref, conv_ref, h_ref, sc_ref, xcs, hss, ucs):
    nb = DEC_BATCH
    x = x_ref[...]
    xn = _rms(x, gmix_ref[...]).astype(BF16)

    rg_x = _dot(xn, w_in_ref[:, 0:D_RG])
    seq = [conv_in_ref[k] for k in range(RG_CONV_W - 1)]
    seq += [rg_x[t * nb:(t + 1) * nb, :] for t in range(DEC_SEQ)]
    for t in range(DEC_SEQ):
        xc_t = seq[t] * rgv_ref[0:1, :]
        for k in range(1, RG_CONV_W):
            xc_t = xc_t + seq[t + k] * rgv_ref[k:k + 1, :]
        xcs[t * nb:(t + 1) * nb, :] = xc_t + rgv_ref[4:5, :]
    for k in range(RG_CONV_W - 1):
        conv_ref[k] = seq[DEC_SEQ + k]

    a, b = _rg_gate_inputs(xcs[...], wg_ref, rgv_ref)
    h = h_in_ref[...]
    for t in range(DEC_SEQ):
        h = a[t * nb:(t + 1) * nb, :] * h + b[t * nb:(t + 1) * nb, :]
        hss[t * nb:(t + 1) * nb, :] = h
    h_ref[...] = h

    rg_gate = _dot(xn, w_in_ref[:, D_RG:2 * D_RG])
    sc_b = _dot(xn, w_in_ref[:, 2 * D_RG:2 * D_RG + D_SC])
    sc_c = _dot(xn, w_in_ref[:, 2 * D_RG + D_SC:2 * D_RG + 2 * D_SC])
    sc_x = _dot(xn, w_in_ref[:, 2 * D_RG + 2 * D_SC:])
    u = sc_c * sc_x
    useq = [sc_in_ref[:, k * D_SC:(k + 1) * D_SC] for k in range(SC_CONV_W - 1)]
    useq += [u[t * nb:(t + 1) * nb, :] for t in range(DEC_SEQ)]
    for t in range(DEC_SEQ):
        uc_t = useq[t] * scv_ref[0:1, :]
        for k in range(1, SC_CONV_W):
            uc_t = uc_t + useq[t + k] * scv_ref[k:k + 1, :]
        ucs[t * nb:(t + 1) * nb, :] = uc_t
    for k in range(SC_CONV_W - 1):
        sc_ref[:, k * D_SC:(k + 1) * D_SC] = useq[DEC_SEQ + k]

    o_ref[...] = x + _mix_out(hss[...], rg_gate, sc_b, ucs[...], scv_ref, w_out_ref)


def _mixer_sample(x, l, pw, w_in, conv_in, h_in, sc_in):
    conv_shape = (RG_CONV_W - 1, DEC_BATCH, D_RG)
    sc_w = (SC_CONV_W - 1) * D_SC
    return pl.pallas_call(
        _mixer_sample_kernel,
        grid=(1,),
        in_specs=[
            pl.BlockSpec((N_SAMPLE, D_MODEL), lambda i: (0, 0)),
        ] + _mixer_weight_specs(l) + [
            _layer_resident(conv_shape, l),
            _layer_resident((DEC_BATCH, D_RG), l),
            _layer_resident((DEC_BATCH, sc_w), l),
        ],
        out_specs=[
            pl.BlockSpec((N_SAMPLE, D_MODEL), lambda i: (0, 0)),
            pl.BlockSpec(conv_shape, lambda i: (0, 0, 0)),
            pl.BlockSpec((DEC_BATCH, D_RG), lambda i: (0, 0)),
            pl.BlockSpec((DEC_BATCH, sc_w), lambda i: (0, 0)),
        ],
        out_shape=[
            jax.ShapeDtypeStruct((N_SAMPLE, D_MODEL), F32),
            jax.ShapeDtypeStruct(conv_shape, F32),
            jax.ShapeDtypeStruct((DEC_BATCH, D_RG), F32),
            jax.ShapeDtypeStruct((DEC_BATCH, sc_w), F32),
        ],
        scratch_shapes=[
            pltpu.VMEM((N_SAMPLE, D_RG), F32),
            pltpu.VMEM((N_SAMPLE, D_RG), F32),
            pltpu.VMEM((N_SAMPLE, D_SC), F32),
        ],
        compiler_params=_params(("arbitrary",)),
        name="mixer_sample",
    )(x, pw["gmix"], w_in, pw["wg"], pw["rgv"], pw["scv"], pw["w_out"],
      conv_in, h_in, sc_in)


def _mem_kv_kernel(m_ref, g_ref, wk_ref, wv_ref, *rest, n_riders):
    rider_src, rest = rest[:n_riders], rest[n_riders:]
    k_ref, v_ref = rest[:2]
    rider_dst, (mn_s,) = rest[2:2 + n_riders], rest[2 + n_riders:]
    _round_riders(rider_src, rider_dst)

    @pl.when(pl.program_id(1) == 0)
    def _():
        mn_s[...] = _rms(m_ref[...], g_ref[...]).astype(BF16)

    mn = mn_s[...]
    k_ref[...] = _dot(mn, wk_ref[...].astype(BF16))
    v_ref[...] = _dot(mn, wv_ref[...].astype(BF16))


def _mem_kv(mem, gmem, w_k, w_v, riders, tn=512):
    rows = BATCH * N_MEM
    nj = D_MODEL // tn
    w_spec = pl.BlockSpec((None, D_MODEL, tn), lambda l, j: (l, 0, j))
    o_spec = pl.BlockSpec((None, rows, tn), lambda l, j: (l, 0, j))
    shape = jax.ShapeDtypeStruct((DEPTH, rows, D_MODEL), F32)
    r_in, r_out, r_shape = _rider_specs(riders, 0, DEPTH * nj, lambda l, j: l * nj + j)
    return pl.pallas_call(
        functools.partial(_mem_kv_kernel, n_riders=len(riders)),
        grid=(DEPTH, nj),
        in_specs=[
            pl.BlockSpec((rows, D_MODEL), lambda l, j: (0, 0), pipeline_mode=pl.Buffered(1)),
            pl.BlockSpec((None, 1, D_MODEL), lambda l, j: (l, 0, 0)),
            w_spec, w_spec,
        ] + r_in,
        out_specs=[o_spec, o_spec] + r_out,
        out_shape=[shape, shape] + r_shape,
        scratch_shapes=[pltpu.VMEM((rows, D_MODEL), BF16)],
        compiler_params=_params(("arbitrary",) * 2),
        name="mem_kv",
    )(mem, gmem, w_k, w_v, *riders)


def _q_sample_kernel(x_ref, g_ref, w_ref, o_ref):
    xn = _rms(x_ref[...], g_ref[...]).astype(BF16)
    o_ref[...] = _dot(xn, w_ref[...])


def _q_sample(x, l, pw, wq, tn=1024):
    return pl.pallas_call(
        _q_sample_kernel,
        grid=(D_MODEL // tn,),
        in_specs=[
            pl.BlockSpec((N_SAMPLE, D_MODEL), lambda j: (0, 0)),
            _layer_block((1, D_MODEL), l),
            pl.BlockSpec((D_MODEL, tn), lambda j: (0, j)),
        ],
        out_specs=pl.BlockSpec((N_SAMPLE, tn), lambda j: (0, j)),
        out_shape=jax.ShapeDtypeStruct((N_SAMPLE, D_MODEL), F32),
        compiler_params=_params(("arbitrary",)),
        name="q_sample",
    )(x, pw["gxa"], wq)


def _oproj_sample_kernel(a_ref, w_ref, res_ref, o_ref):
    o_ref[...] = res_ref[...] + _dot(a_ref[...].astype(BF16), w_ref[...])


def _oproj_sample(a, wo, res, tn=1024):
    return pl.pallas_call(
        _oproj_sample_kernel,
        grid=(D_MODEL // tn,),
        in_specs=[
            pl.BlockSpec((N_SAMPLE, D_MODEL), lambda j: (0, 0)),
            pl.BlockSpec((D_MODEL, tn), lambda j: (0, j)),
            pl.BlockSpec((N_SAMPLE, tn), lambda j: (0, j)),
        ],
        out_specs=pl.BlockSpec((N_SAMPLE, tn), lambda j: (0, j)),
        out_shape=jax.ShapeDtypeStruct((N_SAMPLE, D_MODEL), F32),
        compiler_params=_params(("arbitrary",)),
        name="oproj_sample",
    )(a, wo, res)


def _softmax_rows(s):
    m = jnp.max(s, axis=-1, keepdims=True)
    e = jnp.exp(s - m)
    return e / jnp.sum(e, axis=-1, keepdims=True)


def _attn_prompt_kernel(x_ref, g_ref, wq_ref, k_ref, v_ref, wo_ref, *rest, n_riders):
    rider_src, o_ref, rider_dst = rest[:n_riders], rest[n_riders], rest[n_riders + 1:]
    _round_riders(rider_src, rider_dst)
    x = x_ref[...]
    xn = _rms(x, g_ref[...]).astype(BF16)
    q = _dot(xn, wq_ref[...]).astype(BF16)
    k = k_ref[...].astype(BF16)
    v = v_ref[...].astype(BF16)
    outs = []
    for h in range(XA_HEADS):
        sl = slice(h * XA_HEAD_DIM, (h + 1) * XA_HEAD_DIM)
        s = lax.dot_general(q[:, sl], k[:, sl], (((1,), (1,)), ((), ())),
                            preferred_element_type=F32) / math.sqrt(XA_HEAD_DIM)
        p = _softmax_rows(s).astype(BF16)
        outs.append(_dot(p, v[:, sl]).astype(BF16))
    o = jnp.concatenate(outs, axis=1)
    o_ref[...] = x + _dot(o, wo_ref[...])


def _attn_prompt(x, l, pw, wq, wo, k, v, riders, rider_layer):
    nt = SEQ // TM_ATT
    kv_spec = pl.BlockSpec((None, N_MEM, D_MODEL), lambda b, t: (l, b, 0))
    r_in, r_out, r_shape = _rider_specs(riders, rider_layer, BATCH * nt, lambda b, t: b * nt + t)
    return pl.pallas_call(
        functools.partial(_attn_prompt_kernel, n_riders=len(riders)),
        grid=(BATCH, nt),
        in_specs=[
            pl.BlockSpec((TM_ATT, D_MODEL), lambda b, t: (b * nt + t, 0)),
            _layer_resident((1, D_MODEL), l),
            _whole_resident((D_MODEL, D_MODEL)),
            kv_spec, kv_spec,
            _whole_resident((D_MODEL, D_MODEL)),
        ] + r_in,
        out_specs=[pl.BlockSpec((TM_ATT, D_MODEL), lambda b, t: (b * nt + t, 0))] + r_out,
        out_shape=[jax.ShapeDtypeStruct((N_PROMPT, D_MODEL), F32)] + r_shape,
        compiler_params=_params(("arbitrary", "arbitrary")),
        name="attn_prompt",
    )(x, pw["gxa"], wq, k, v, wo, *riders)


def _head_matrix(mem_ref, stage_ref, b, h):
    n = N_MEM * HD_CHUNKS
    stage_ref[...] = mem_ref[pl.ds(b * n * XA_HEADS + h, n, stride=XA_HEADS), :]
    chunks = [stage_ref[pl.ds(c, N_MEM, stride=HD_CHUNKS), :] for c in range(HD_CHUNKS)]
    return jnp.concatenate(chunks, axis=1).astype(BF16)


def _attn_sample_kernel(q_ref, k_ref, v_ref, o_ref, stage_ref):
    rows = DEC_SEQ * ATT_BB
    keys = ATT_BB * N_MEM
    row_b = lax.broadcasted_iota(jnp.int32, (rows, keys), 0) // DEC_SEQ
    key_b = lax.broadcasted_iota(jnp.int32, (rows, keys), 1) // N_MEM
    own = row_b == key_b
    outs = []
    for h in range(XA_HEADS):
        q = q_ref[:, h * XA_HEAD_DIM:(h + 1) * XA_HEAD_DIM].astype(BF16)
        k = jnp.concatenate([_head_matrix(k_ref, stage_ref, b, h) for b in range(ATT_BB)], axis=0)
        v = jnp.concatenate([_head_matrix(v_ref, stage_ref, b, h) for b in range(ATT_BB)], axis=0)
        s = lax.dot_general(q, k, (((1,), (1,)), ((), ())),
                            preferred_element_type=F32) / math.sqrt(XA_HEAD_DIM)
        s = jnp.where(own, s, -jnp.inf)
        m = jnp.max(s, axis=-1, keepdims=True)
        e = jnp.where(own, jnp.exp(s - m), 0.0)
        p = (e / jnp.sum(e, axis=-1, keepdims=True)).astype(BF16)
        outs.append(_dot(p, v))
    o_ref[...] = jnp.concatenate(outs, axis=1)


def _lane_pieces(cache):
    c = cache.reshape(DEPTH, DEC_BATCH, N_MEM, XA_HEADS, HD_CHUNKS, LANES)
    c = jnp.transpose(c, (0, 1, 2, 4, 3, 5))
    return c.reshape(DEPTH * DEC_BATCH * N_MEM * HD_CHUNKS * XA_HEADS, LANES)


def _attn_sample(q, l, k, v):
    rows = ATT_BB * DEC_SEQ
    blk = ATT_BB * N_MEM * HD_CHUNKS * XA_HEADS
    nb = DEC_BATCH // ATT_BB
    kv_spec = pl.BlockSpec((blk, LANES), lambda i: (l * nb + i, 0))
    return pl.pallas_call(
        _attn_sample_kernel,
        grid=(nb,),
        in_specs=[pl.BlockSpec((rows, D_MODEL), lambda i: (i, 0)), kv_spec, kv_spec],
        out_specs=pl.BlockSpec((rows, D_MODEL), lambda i: (i, 0)),
        out_shape=jax.ShapeDtypeStruct((N_SAMPLE, D_MODEL), F32),
        scratch_shapes=[pltpu.VMEM((N_MEM * HD_CHUNKS, LANES), F32)],
        compiler_params=_params(("arbitrary",)),
        name="attn_sample",
    )(q, k, v)


def _route(logits):
    lane = lax.broadcasted_iota(jnp.int32, logits.shape, 1)
    neg = -jnp.inf
    is_g = lane < N_GROUPS
    lg = jnp.where(is_g, logits, neg)
    mg = jnp.max(lg, axis=-1, keepdims=True)
    eg = jnp.where(is_g, jnp.exp(lg - mg), 0.0)
    p_g = eg / jnp.sum(eg, axis=-1, keepdims=True)
    pg_sel = jnp.max(p_g, axis=-1, keepdims=True)
    g_sel = jnp.min(jnp.where(p_g == pg_sel, lane, LANES), axis=-1, keepdims=True)

    lo = N_GROUPS + g_sel * EXPERTS_PER_GROUP
    in_grp = (lane >= lo) & (lane < lo + EXPERTS_PER_GROUP)
    le = jnp.where(in_grp, logits, neg)
    me = jnp.max(le, axis=-1, keepdims=True)
    ee = jnp.where(in_grp, jnp.exp(le - me), 0.0)
    p_e = ee / jnp.sum(ee, axis=-1, keepdims=True)

    p1 = jnp.max(p_e, axis=-1, keepdims=True)
    i1 = jnp.min(jnp.where(in_grp & (p_e == p1), lane, LANES), axis=-1, keepdims=True)
    rest = in_grp & (lane != i1)
    p2 = jnp.max(jnp.where(rest, p_e, neg), axis=-1, keepdims=True)
    i2 = jnp.min(jnp.where(rest & (p_e == p2), lane, LANES), axis=-1, keepdims=True)
    tot = p1 + p2
    w1 = p1 / tot * pg_sel
    w2 = p2 / tot * pg_sel
    gates = jnp.where(lane == i1, w1, 0.0) + jnp.where(lane == i2, w2, 0.0)

    local = jnp.zeros_like(gates)
    for g in range(N_GROUPS):
        start = N_GROUPS + g * EXPERTS_PER_GROUP
        local = local + jnp.where(lane < EXPERTS_PER_GROUP,
                                  pltpu.roll(gates, LANES - start, axis=1), 0.0)

    ja = jnp.minimum(i1, i2) - lo
    jb = jnp.maximum(i1, i2) - lo
    code = ja * EXPERTS_PER_GROUP + jb
    pair = jnp.full_like(code, N_PAIRS - 1)
    for p in range(N_PAIRS - 1):
        lo_j, hi_j = min(PAIR_A[p], PAIR_B[p]), max(PAIR_A[p], PAIR_B[p])
        pair = jnp.where(code == lo_j * EXPERTS_PER_GROUP + hi_j, p, pair)
    bucket = (g_sel * N_PAIRS + pair).astype(F32)
    return local + jnp.where(lane == BUCKET_LANE, bucket, 0.0)


def _router_kernel(x_ref, g_ref, whi_ref, wlo_ref, br_ref, o_ref):
    xn = _rms(x_ref[...], g_ref[...])
    hi = xn.astype(BF16)
    lo = (xn - hi.astype(F32)).astype(BF16)
    w_hi = whi_ref[...]
    logits = _dot(hi, w_hi) + _dot(lo, w_hi) + _dot(hi, wlo_ref[...]) + br_ref[...]
    o_ref[...] = _route(logits)


def _router_riders_kernel(x_ref, g_ref, whi_ref, wlo_ref, br_ref, *rest, n_riders):
    rider_src, o_ref, rider_dst = rest[:n_riders], rest[n_riders], rest[n_riders + 1:]
    _round_riders(rider_src, rider_dst)
    _router_kernel(x_ref, g_ref, whi_ref, wlo_ref, br_ref, o_ref)


def _router_specs(l):
    return [
        pl.BlockSpec((TM_ROUTE, D_MODEL), lambda i: (i, 0)),
        _layer_block((1, D_MODEL), l),
        _layer_block((D_MODEL, LANES), l),
        _layer_block((D_MODEL, LANES), l),
        _layer_block((1, LANES), l),
    ]


def _router(x, l, pw):
    n_rows = x.shape[0]
    return pl.pallas_call(
        _router_kernel,
        grid=(n_rows // TM_ROUTE,),
        in_specs=_router_specs(l),
        out_specs=pl.BlockSpec((TM_ROUTE, LANES), lambda i: (i, 0)),
        out_shape=jax.ShapeDtypeStruct((n_rows, LANES), F32),
        compiler_params=_params(("arbitrary",)),
        name="router",
    )(x, pw["gmoe"], pw["wr_hi"], pw["wr_lo"], pw["br"])


def _router_prompt(x, l, pw, riders):
    n_steps = N_PROMPT // TM_ROUTE
    r_in, r_out, r_shape = _rider_specs(riders, l, n_steps, lambda i: i)
    return pl.pallas_call(
        functools.partial(_router_riders_kernel, n_riders=len(riders)),
        grid=(n_steps,),
        in_specs=_router_specs(l) + r_in,
        out_specs=[pl.BlockSpec((TM_ROUTE, LANES), lambda i: (i, 0))] + r_out,
        out_shape=[jax.ShapeDtypeStruct((N_PROMPT, LANES), F32)] + r_shape,
        compiler_params=_params(("arbitrary",)),
        name="router_prompt",
    )(x, pw["gmoe"], pw["wr_hi"], pw["wr_lo"], pw["br"], *riders)


def _dispatch_tables(rinfo):
    i32 = jnp.int32
    bucket = rinfo[:, BUCKET_LANE].astype(i32)
    bucket_ids = jnp.arange(N_BUCKETS, dtype=i32)
    onehot = (bucket[:, None] == bucket_ids[None, :]).astype(i32)
    csum = jnp.cumsum(onehot, axis=0)
    counts = csum[-1]
    tiles_per = (counts + TG - 1) // TG
    tile_end = jnp.cumsum(tiles_per)
    tile_start = tile_end - tiles_per
    slot = jnp.sum(onehot * (tile_start[None, :] * TG + csum - 1), axis=1)

    tok1 = jnp.arange(1, N_TOK + 1, dtype=i32).astype(F32)
    payload = jnp.concatenate([tok1[:, None], rinfo[:, :EXPERTS_PER_GROUP]], axis=1)
    slots = jnp.zeros((N_SLOTS, 1 + EXPERTS_PER_GROUP), F32).at[slot].set(
        payload, unique_indices=True)
    slot_tok = slots[:, 0].astype(i32) - 1

    tile = jnp.arange(N_TILES, dtype=i32)
    n_valid = tile_end[-1]
    tq = jnp.minimum(tile, n_valid - 1)
    tile_bucket = jnp.sum((tile_end[None, :] <= tq[:, None]).astype(i32), axis=1)
    tile_oh = (tile_bucket[:, None] == bucket_ids[None, :]).astype(i32)
    in_bucket = tile - jnp.sum(tile_oh * tile_start[None, :], axis=1)
    tile_cnt = jnp.clip(jnp.sum(tile_oh * counts[None, :], axis=1) - in_bucket * TG, 0, TG)
    tile_cnt = jnp.where(tile < n_valid, tile_cnt, 0)
    pair_a = jnp.asarray([(b // N_PAIRS) * EXPERTS_PER_GROUP + PAIR_A[b % N_PAIRS]
                          for b in range(N_BUCKETS)], i32)
    pair_b = jnp.asarray([(b // N_PAIRS) * EXPERTS_PER_GROUP + PAIR_B[b % N_PAIRS]
                          for b in range(N_BUCKETS)], i32)
    tile_ea = jnp.sum(tile_oh * pair_a[None, :], axis=1)
    tile_eb = jnp.sum(tile_oh * pair_b[None, :], axis=1)
    is_prompt = (slot_tok >= 0) & (slot_tok < N_PROMPT)
    tile_np = jnp.sum(is_prompt.reshape(N_TILES, TG).astype(i32), axis=1)
    return tile_ea, tile_eb, tile_cnt, tile_np, slot_tok.reshape(N_TILES, 1, TG), slots


def _experts_kernel(ea_ref, eb_ref, cnt_ref, np_ref,
                    xp_hbm, xs_hbm, tokp_ref, tokc_ref, tokn_ref, gates_ref, g_ref,
                    wga_ref, wua_ref, wda_ref, wgb_ref, wub_ref, wdb_ref, gfin_ref,
                    op_hbm, os_hbm, xbuf, obuf, gsem, ssem, *, final_norm, sample_batch_major):
    t = pl.program_id(0)
    nt = pl.num_programs(0)
    buf = lax.rem(t, 2)
    t_prev = jnp.maximum(t - 1, 0)
    t_next = jnp.minimum(t + 1, nt - 1)
    cnt = cnt_ref[t]
    has_next = (t + 1 < nt) & (cnt_ref[t_next] > 0)

    def sample_dst(r):
        if not sample_batch_major:
            return r
        return (r % DEC_BATCH) * DEC_SEQ + r // DEC_BATCH

    def row_slot(r):
        return lax.shift_right_logical(r, 3), lax.bitwise_and(r, SUBLANES - 1)

    def gather_p(tok_ref, b, r, g, s):
        return pltpu.make_async_copy(
            xp_hbm.at[pl.ds(tok_ref[0, r], 1)], xbuf.at[b, g, pl.ds(s, 1)], gsem.at[b])

    def gather_s(tok_ref, b, r, g, s):
        return pltpu.make_async_copy(
            xs_hbm.at[pl.ds(tok_ref[0, r] - N_PROMPT, 1)], xbuf.at[b, g, pl.ds(s, 1)], gsem.at[b])

    def scatter_p(tok_ref, b, r, g, s):
        return pltpu.make_async_copy(
            obuf.at[b, g, pl.ds(s, 1)], op_hbm.at[pl.ds(tok_ref[0, r], 1)], ssem.at[b])

    def scatter_s(tok_ref, b, r, g, s):
        dst = sample_dst(tok_ref[0, r] - N_PROMPT)
        return pltpu.make_async_copy(
            obuf.at[b, g, pl.ds(s, 1)], os_hbm.at[pl.ds(dst, 1)], ssem.at[b])

    def for_range(lo, hi, make_copy, op):
        full = (hi - lo) // SUBLANES
        aligned = isinstance(lo, int) and lo == 0

        def group(i, c):
            for j in range(SUBLANES):
                r = lo + i * SUBLANES + j
                op(make_copy(r, i, j) if aligned else make_copy(r, *row_slot(r)), j)
            return c

        def single(r, c):
            op(make_copy(r, *row_slot(r)), 0)
            return c
        lax.fori_loop(0, full, group, 0)
        lax.fori_loop(lo + full * SUBLANES, hi, single, 0)

    def gather(tile, tok_ref, b, op):
        for_range(0, np_ref[tile], functools.partial(gather_p, tok_ref, b), op)
        for_range(np_ref[tile], cnt_ref[tile], functools.partial(gather_s, tok_ref, b), op)

    def scatter(tile, tok_ref, b, op):
        for_range(0, np_ref[tile], functools.partial(scatter_p, tok_ref, b), op)
        for_range(np_ref[tile], cnt_ref[tile], functools.partial(scatter_s, tok_ref, b), op)

    def start(copy, j):
        copy.start(priority=j % 2)

    def wait(copy, j):
        del j
        copy.wait()

    @pl.when(t == 0)
    def _():
        xbuf[...] = jnp.zeros_like(xbuf)
        gather(t, tokc_ref, 0, start)

    @pl.when(has_next)
    def _():
        gather(t_next, tokn_ref, 1 - buf, start)

    @pl.when(cnt > 0)
    def _():
        gather(t, tokc_ref, buf, wait)

        ja = ea_ref[t] % EXPERTS_PER_GROUP + SLOT_GATE_LANE
        jb = eb_ref[t] % EXPERTS_PER_GROUP + SLOT_GATE_LANE
        n_chunks = (cnt + ROW_CHUNK - 1) // ROW_CHUNK

        def compute(m):
            x = xbuf[buf, 0:m // SUBLANES].reshape(m, D_MODEL)
            xn = _rms(x, g_ref[...]).astype(BF16)
            gates = gates_ref[0:m, :]
            lane = lax.broadcasted_iota(jnp.int32, gates.shape, 1)
            ga = jnp.sum(jnp.where(lane == ja, gates, 0.0), axis=-1, keepdims=True)
            gb = jnp.sum(jnp.where(lane == jb, gates, 0.0), axis=-1, keepdims=True)
            act_a = (jax.nn.silu(_dot(xn, wga_ref[...])) * _dot(xn, wua_ref[...]) * ga).astype(BF16)
            act_b = (jax.nn.silu(_dot(xn, wgb_ref[...])) * _dot(xn, wub_ref[...]) * gb).astype(BF16)
            y = x + (_dot(act_a, wda_ref[...]) + _dot(act_b, wdb_ref[...]))
            if final_norm:
                y = _rms(y, gfin_ref[...])
            obuf[buf, 0:m // SUBLANES] = y.reshape(m // SUBLANES, SUBLANES, D_MODEL)

        for k in range(1, TG // ROW_CHUNK + 1):
            pl.when(n_chunks == k)(functools.partial(compute, k * ROW_CHUNK))

        @pl.when(t > 0)
        def _():
            scatter(t_prev, tokp_ref, 1 - buf, wait)

        scatter(t, tokc_ref, buf, start)

        @pl.when(jnp.logical_not(has_next))
        def _():
            scatter(t, tokc_ref, buf, wait)


def _experts(xp, xs, l, pw, expert_w, gfin, tables, final_norm):
    tile_ea, tile_eb, tile_cnt, tile_np, slot_tok, gates = tables
    w_gate, w_up, w_down = expert_w

    def tok_spec(shift):
        return pl.BlockSpec(
            (None, 1, TG),
            lambda t, *_: (jnp.clip(t + shift, 0, N_TILES - 1), 0, 0),
            memory_space=pltpu.SMEM)

    def w_spec(shape, which):
        return pl.BlockSpec(
            (None,) + shape,
            lambda t, ea, eb, *_: ((ea, eb)[which][t], 0, 0))

    any_spec = pl.BlockSpec(memory_space=pl.ANY)
    grid_spec = pltpu.PrefetchScalarGridSpec(
        num_scalar_prefetch=4,
        grid=(N_TILES,),
        in_specs=[
            any_spec, any_spec,
            tok_spec(-1), tok_spec(0), tok_spec(1),
            pl.BlockSpec((TG, SLOT_GATE_LANE + EXPERTS_PER_GROUP), lambda t, *_: (t, 0)),
            pl.BlockSpec((None, 1, D_MODEL), lambda t, *_: (l, 0, 0)),
            w_spec((D_MODEL, D_FF), 0), w_spec((D_MODEL, D_FF), 0), w_spec((D_FF, D_MODEL), 0),
            w_spec((D_MODEL, D_FF), 1), w_spec((D_MODEL, D_FF), 1), w_spec((D_FF, D_MODEL), 1),
            pl.BlockSpec((1, D_MODEL), lambda t, *_: (0, 0)),
        ],
        out_specs=[any_spec, any_spec],
        scratch_shapes=[
            pltpu.VMEM((2, TG // SUBLANES, SUBLANES, D_MODEL), F32),
            pltpu.VMEM((2, TG // SUBLANES, SUBLANES, D_MODEL), F32),
            pltpu.SemaphoreType.DMA((2,)),
            pltpu.SemaphoreType.DMA((2,)),
        ],
    )
    return pl.pallas_call(
        functools.partial(_experts_kernel, final_norm=final_norm, sample_batch_major=final_norm),
        grid_spec=grid_spec,
        out_shape=[jax.ShapeDtypeStruct((N_PROMPT, D_MODEL), F32),
                   jax.ShapeDtypeStruct((N_SAMPLE, D_MODEL), F32)],
        compiler_params=_params(("arbitrary",)),
        name="experts",
    )(tile_ea, tile_eb, tile_cnt, tile_np, xp, xs, slot_tok, slot_tok, slot_tok, gates,
      pw["gmoe"], w_gate, w_up, w_down, w_gate, w_up, w_down, gfin)


def _block_diag_gates(w_a, w_x):
    def diag(w):
        w = w.reshape(DEPTH, N_GATE_BLOCKS, HEADS_PER_GATE_BLOCK, RG_HEAD_DIM, RG_HEAD_DIM)
        eye = jnp.eye(HEADS_PER_GATE_BLOCK, dtype=w.dtype)
        full = jnp.einsum("lqhij,hk->lqhikj", w, eye)
        return full.reshape(DEPTH, N_GATE_BLOCKS, MXU_DIM, MXU_DIM)
    return jnp.concatenate([diag(w_a), diag(w_x)], axis=-1).astype(BF16)


def _to_batch_major(a):
    return jnp.transpose(a.reshape(DEC_SEQ, DEC_BATCH, -1), (1, 0, 2)).reshape(N_SAMPLE, -1)


def _to_time_major(a):
    return jnp.transpose(a.reshape(DEC_BATCH, DEC_SEQ, -1), (1, 0, 2)).reshape(N_SAMPLE, -1)


def _stack_rows(rows, n):
    width = rows[0].shape[-1]
    rows = [r.reshape(DEPTH, -1, width) for r in rows]
    have = sum(r.shape[1] for r in rows)
    if have < n:
        rows.append(jnp.zeros((DEPTH, n - have, width), F32))
    return jnp.concatenate(rows, axis=1)


def kernel(x_prompt, x_sample, state_rglru_h, state_rglru_conv, state_sconv, cache_mem_k,
           cache_mem_v, mem_prompt, norm_mix, w_in, rg_conv_w, rg_conv_b, rg_w_a, rg_b_a,
           rg_w_x, rg_b_x, rg_lambda, sc_conv_w, norm_rg_out, norm_sc_out, w_out, norm_xattn,
           norm_mem, xa_w_q, xa_w_k, xa_w_v, xa_w_o, norm_moe, router_group_w, router_group_b,
           router_expert_w, router_expert_b, expert_w_gate, expert_w_up, expert_w_down,
           norm_final):
    wr = jnp.concatenate([router_group_w, router_expert_w], axis=2)
    wr = jnp.pad(wr, ((0, 0), (0, 0), (0, LANES - wr.shape[2])))
    wr_hi = wr.astype(BF16)
    wr_lo = (wr - wr_hi.astype(F32)).astype(BF16)
    br = jnp.concatenate([router_group_b, router_expert_b], axis=1)
    br = jnp.pad(br, ((0, 0), (0, LANES - br.shape[1]))).reshape(DEPTH, 1, LANES)
    pw = dict(
        gmix=norm_mix.reshape(DEPTH, 1, D_MODEL),
        wg=_block_diag_gates(rg_w_a, rg_w_x),
        rgv=_stack_rows([rg_conv_w, rg_conv_b, rg_b_a, rg_b_x, rg_lambda], 8),
        scv=_stack_rows([sc_conv_w, norm_rg_out, norm_sc_out], 8),
        gxa=norm_xattn.reshape(DEPTH, 1, D_MODEL),
        gmem=norm_mem.reshape(DEPTH, 1, D_MODEL),
        gmoe=norm_moe.reshape(DEPTH, 1, D_MODEL),
        wr_hi=wr_hi, wr_lo=wr_lo, br=br,
    )
    gfin = norm_final.reshape(1, D_MODEL)

    conv_in = jnp.transpose(state_rglru_conv, (0, 2, 1, 3))
    sc_in = state_sconv.reshape(DEPTH, DEC_BATCH, (SC_CONV_W - 1) * D_SC)
    w_gate_rows = expert_w_gate.reshape(DEPTH, N_EXPERTS * D_MODEL, D_FF)
    w_up_rows = expert_w_up.reshape(DEPTH, N_EXPERTS * D_MODEL, D_FF)
    w_down_rows = expert_w_down.reshape(DEPTH, N_EXPERTS * D_FF, D_MODEL)
    w_in_l = w_in[0].astype(BF16)
    cache_k = _lane_pieces(cache_mem_k)
    cache_v = _lane_pieces(cache_mem_v)

    mem = mem_prompt.reshape(BATCH * N_MEM, D_MODEL)
    p_k, p_v, w_out_bf16 = _mem_kv(mem, pw["gmem"], xa_w_k, xa_w_v,
                                   [w_out.reshape(1, DEPTH * D_MODEL, D_MODEL)])
    pw["w_out"] = w_out_bf16.reshape(DEPTH, D_MODEL, D_MODEL)

    xp = x_prompt.reshape(N_PROMPT, D_MODEL)
    xs = jnp.transpose(x_sample, (1, 0, 2)).reshape(N_SAMPLE, D_MODEL)

    p_conv, p_h, p_sc, s_conv, s_h, s_sc = [], [], [], [], [], []
    for l in range(DEPTH):
        xp, c, hh, sc, w_down_l, wq_l, wo_l = _mixer_prompt(
            xp, l, pw, w_in_l, [w_down_rows, xa_w_q, xa_w_o])
        xs, cs, hs, scs = _mixer_sample(xs, l, pw, w_in_l, conv_in, state_rglru_h, sc_in)
        p_conv.append(c)
        p_h.append(hh.reshape(BATCH, D_RG))
        p_sc.append(sc)
        s_conv.append(cs)
        s_h.append(hs)
        s_sc.append(scs.reshape(DEC_BATCH, SC_CONV_W - 1, D_SC))

        next_w_in = [w_in] if l + 1 < DEPTH else []
        xp, *w_in_next = _attn_prompt(xp, l, pw, wq_l, wo_l, p_k, p_v, next_w_in, l + 1)
        q_s = _to_batch_major(_q_sample(xs, l, pw, wq_l))
        o_s = _attn_sample(q_s, l, cache_k, cache_v)
        xs = _oproj_sample(_to_time_major(o_s), wo_l, xs)
        if w_in_next:
            w_in_l = w_in_next[0]

        rinfo_p, w_gate_l, w_up_l = _router_prompt(xp, l, pw, [w_gate_rows, w_up_rows])
        rinfo = jnp.concatenate([rinfo_p, _router(xs, l, pw)], axis=0)
        expert_w = (w_gate_l.reshape(N_EXPERTS, D_MODEL, D_FF),
                    w_up_l.reshape(N_EXPERTS, D_MODEL, D_FF),
                    w_down_l.reshape(N_EXPERTS, D_FF, D_MODEL))
        xp, xs = _experts(xp, xs, l, pw, expert_w, gfin, _dispatch_tables(rinfo),
                          final_norm=(l == DEPTH - 1))

    y_prompt = xp.reshape(BATCH, SEQ, D_MODEL)
    y_sample = xs.reshape(DEC_BATCH, DEC_SEQ, D_MODEL)
    mem_shape = (DEPTH, BATCH, N_MEM, XA_HEADS, XA_HEAD_DIM)
    return (y_prompt, y_sample,
            jnp.stack(p_h), jnp.stack(p_conv), jnp.stack(p_sc),
            p_k.reshape(mem_shape), p_v.reshape(mem_shape),
            jnp.stack(s_h), jnp.transpose(jnp.stack(s_conv), (0, 2, 1, 3)), jnp.stack(s_sc))
```

```python
import functools
import math

import jax
import jax.numpy as jnp
from jax import lax
from jax.experimental import pallas as pl
from jax.experimental.pallas import tpu as pltpu

D_MODEL = 2048
BATCH = 4
SEQ = 2048
DEPTH = 2
DEC_BATCH = 128
DEC_SEQ = 4
D_RG = 1024
D_SC = 1024
RG_HEADS = 16
RG_HEAD_DIM = 64
RG_CONV_W = 4
RG_C = 8.0
SC_CONV_W = 3
D_IN = 2 * D_RG + 3 * D_SC
N_MEM = 256
XA_HEADS = 4
XA_HEAD_DIM = 512
N_GROUPS = 4
EXPERTS_PER_GROUP = 4
N_EXPERTS = 16
D_FF = 512
EPS = 1e-6

N_PROMPT = BATCH * SEQ
N_SAMPLE = DEC_BATCH * DEC_SEQ
N_TOK = N_PROMPT + N_SAMPLE

V7X_VMEM_LIMIT_BYTES = 56 * 1024 * 1024
SUBLANES = 8
LANES = 128
MXU_DIM = 256

HEADS_PER_GATE_BLOCK = MXU_DIM // RG_HEAD_DIM
N_GATE_BLOCKS = D_RG // MXU_DIM

TM_MIX = 256
TM_ATT = 512
TM_ROUTE = 512
ATT_BB = 4
HD_CHUNKS = XA_HEAD_DIM // LANES

PAIR_A = (0, 0, 0, 1, 2, 2)
PAIR_B = (1, 2, 3, 3, 3, 1)
N_PAIRS = len(PAIR_A)
N_BUCKETS = N_GROUPS * N_PAIRS
TG = 512
ROW_CHUNK = 128
N_TILES = -(-(N_TOK + N_BUCKETS * (TG - 1)) // TG)
N_SLOTS = N_TILES * TG
BUCKET_LANE = EXPERTS_PER_GROUP
SLOT_GATE_LANE = 1

BF16 = jnp.bfloat16
F32 = jnp.float32


def _params(sem, vmem=V7X_VMEM_LIMIT_BYTES):
    return pltpu.CompilerParams(dimension_semantics=sem, vmem_limit_bytes=vmem)


def _layer_resident(shape, l):
    nd = len(shape)
    return pl.BlockSpec((None,) + shape, lambda *_: (l,) + (0,) * nd, pipeline_mode=pl.Buffered(1))


def _layer_block(shape, l):
    nd = len(shape)
    return pl.BlockSpec((None,) + shape, lambda *_: (l,) + (0,) * nd)


def _whole_resident(shape):
    nd = len(shape)
    return pl.BlockSpec(shape, lambda *_: (0,) * nd, pipeline_mode=pl.Buffered(1))


def _rider_specs(srcs, l, n_steps, step_of):
    in_specs, out_specs, out_shapes = [], [], []
    for a in srcs:
        _, r, c = a.shape
        rows = r // n_steps
        assert rows * n_steps == r and rows % 16 == 0
        in_specs.append(pl.BlockSpec((None, rows, c), lambda *g: (l, step_of(*g), 0)))
        out_specs.append(pl.BlockSpec((rows, c), lambda *g: (step_of(*g), 0)))
        out_shapes.append(jax.ShapeDtypeStruct((r, c), BF16))
    return in_specs, out_specs, out_shapes


def _round_riders(src_refs, dst_refs):
    for src, dst in zip(src_refs, dst_refs):
        dst[...] = src[...].astype(BF16)


def _rms(x, g):
    return x * lax.rsqrt(jnp.mean(x * x, axis=-1, keepdims=True) + EPS) * g


def _dot(a, b):
    return jnp.dot(a, b, preferred_element_type=F32)


def _rg_gate_block(xc_q, wg_ref, rgv_ref, q):
    sl = slice(q * MXU_DIM, (q + 1) * MXU_DIM)
    g = _dot(xc_q.astype(BF16), wg_ref[q])
    r = jax.nn.sigmoid(g[:, :MXU_DIM] + rgv_ref[5:6, sl])
    i = jax.nn.sigmoid(g[:, MXU_DIM:] + rgv_ref[6:7, sl])
    log_a = -RG_C * r * jax.nn.softplus(-rgv_ref[7:8, sl])
    a = jnp.exp(log_a)
    mult = jnp.sqrt(1.0 - a * a)
    return a, mult * (i * xc_q)


def _rg_gate_inputs(xc, wg_ref, rgv_ref):
    parts = [_rg_gate_block(xc[:, q * MXU_DIM:(q + 1) * MXU_DIM], wg_ref, rgv_ref, q)
             for q in range(N_GATE_BLOCKS)]
    return (jnp.concatenate([p[0] for p in parts], axis=1),
            jnp.concatenate([p[1] for p in parts], axis=1))


def _mix_out(hs, rg_gate, sc_b, uc, scv_ref, w_out_ref):
    rg_out = _rms(hs * jax.nn.gelu(rg_gate), scv_ref[3:4, :]).astype(BF16)
    sc_out = _rms(sc_b * uc, scv_ref[4:5, :]).astype(BF16)
    return _dot(rg_out, w_out_ref[0:D_RG, :]) + _dot(sc_out, w_out_ref[D_RG:, :])


def _shift_rows(x, d, fill):
    m = x.shape[0]
    if d % SUBLANES == 0:
        head = jnp.full((d, x.shape[1]), fill, x.dtype)
        return jnp.concatenate([head, x[:m - d]], axis=0)
    rolled = pltpu.roll(x, d, axis=0)
    row = lax.broadcasted_iota(jnp.int32, x.shape, 0)
    return jnp.where(row >= d, rolled, fill)


def _scan_rows(a, b, h0):
    m = a.shape[0]
    d = 1
    while d < m:
        a_sh = _shift_rows(a, d, 1.0)
        b_sh = _shift_rows(b, d, 0.0)
        b = a * b_sh + b
        a = a * a_sh
        d *= 2
    return a * h0 + b


def _mixer_prompt_kernel(x_ref, gmix_ref, w_in_ref, wg_ref, rgv_ref, scv_ref, w_out_ref, *rest,
                         n_riders):
    rider_src, rest = rest[:n_riders], rest[n_riders:]
    o_ref, conv_ref, h_ref, sc_ref = rest[:4]
    rider_dst, (rgx, usc, hcar) = rest[4:4 + n_riders], rest[4 + n_riders:]
    _round_riders(rider_src, rider_dst)
    t = pl.program_id(1)
    tm = x_ref.shape[0]

    @pl.when(t == 0)
    def _():
        rgx[0:SUBLANES, :] = jnp.zeros((SUBLANES, D_RG), F32)
        usc[0:SUBLANES, :] = jnp.zeros((SUBLANES, D_SC), F32)
        hcar[...] = jnp.zeros_like(hcar)

    x = x_ref[...]
    xn = _rms(x, gmix_ref[...]).astype(BF16)

    rg_x = _dot(xn, w_in_ref[:, 0:D_RG])
    rgx[SUBLANES:SUBLANES + tm, :] = rg_x

    w_cols = (slice(2 * D_RG + D_SC, 2 * D_RG + 2 * D_SC), slice(2 * D_RG + 2 * D_SC, D_IN),
              slice(2 * D_RG, 2 * D_RG + D_SC), slice(D_RG, 2 * D_RG))
    hs_parts, proj = [], []
    for q in range(N_GATE_BLOCKS):
        sl = slice(q * MXU_DIM, (q + 1) * MXU_DIM)
        xc = rgx[5:5 + tm, sl] * rgv_ref[0:1, sl]
        xc = xc + rgx[6:6 + tm, sl] * rgv_ref[1:2, sl]
        xc = xc + rgx[7:7 + tm, sl] * rgv_ref[2:3, sl]
        xc = xc + rg_x[:, sl] * rgv_ref[3:4, sl]
        xc = xc + rgv_ref[4:5, sl]
        a, b = _rg_gate_block(xc, wg_ref, rgv_ref, q)
        hs_parts.append(_scan_rows(a, b, hcar[0:1, sl]))
        proj.append(_dot(xn, w_in_ref[:, w_cols[q]]))
        if q == 1:
            sc_c, sc_x = proj
            u = sc_c * sc_x
            usc[SUBLANES:SUBLANES + tm, :] = u
            uc = usc[6:6 + tm, :] * scv_ref[0:1, :]
            uc = uc + usc[7:7 + tm, :] * scv_ref[1:2, :]
            uc = uc + u * scv_ref[2:3, :]
    hs = jnp.concatenate(hs_parts, axis=1)
    hcar[0:1, :] = hs[tm - 1:tm, :]
    sc_b, rg_gate = proj[2], proj[3]

    o_ref[...] = x + _mix_out(hs, rg_gate, sc_b, uc, scv_ref, w_out_ref)

    @pl.when(t == pl.num_programs(1) - 1)
    def _():
        conv_ref[0] = rgx[tm + 5:tm + 8, :]
        sc_ref[0] = usc[tm + 6:tm + 8, :]
        h_ref[0] = hs[tm - 1:tm, :]

    rgx[0:SUBLANES, :] = rgx[tm:tm + SUBLANES, :]
    usc[0:SUBLANES, :] = usc[tm:tm + SUBLANES, :]


def _mixer_weight_specs(l):
    return [
        _layer_resident((1, D_MODEL), l),
        _whole_resident((D_MODEL, D_IN)),
        _layer_resident((N_GATE_BLOCKS, MXU_DIM, 2 * MXU_DIM), l),
        _layer_resident((8, D_RG), l),
        _layer_resident((8, D_SC), l),
        _layer_resident((D_MODEL, D_MODEL), l),
    ]


def _mixer_prompt(x, l, pw, w_in, riders):
    nt = SEQ // TM_MIX
    r_in, r_out, r_shape = _rider_specs(riders, l, BATCH * nt, lambda b, t: b * nt + t)
    return pl.pallas_call(
        functools.partial(_mixer_prompt_kernel, n_riders=len(riders)),
        grid=(BATCH, nt),
        in_specs=[pl.BlockSpec((TM_MIX, D_MODEL), lambda b, t: (b * nt + t, 0))]
        + _mixer_weight_specs(l) + r_in,
        out_specs=[
            pl.BlockSpec((TM_MIX, D_MODEL), lambda b, t: (b * nt + t, 0)),
            pl.BlockSpec((1, RG_CONV_W - 1, D_RG), lambda b, t: (b, 0, 0)),
            pl.BlockSpec((1, 1, D_RG), lambda b, t: (b, 0, 0)),
            pl.BlockSpec((1, SC_CONV_W - 1, D_SC), lambda b, t: (b, 0, 0)),
        ] + r_out,
        out_shape=[
            jax.ShapeDtypeStruct((N_PROMPT, D_MODEL), F32),
            jax.ShapeDtypeStruct((BATCH, RG_CONV_W - 1, D_RG), F32),
            jax.ShapeDtypeStruct((BATCH, 1, D_RG), F32),
            jax.ShapeDtypeStruct((BATCH, SC_CONV_W - 1, D_SC), F32),
        ] + r_shape,
        scratch_shapes=[
            pltpu.VMEM((SUBLANES + TM_MIX, D_RG), F32),
            pltpu.VMEM((SUBLANES + TM_MIX, D_SC), F32),
            pltpu.VMEM((SUBLANES, D_RG), F32),
        ],
        compiler_params=_params(("arbitrary", "arbitrary")),
        name="mixer_prompt",
    )(x, pw["gmix"], w_in, pw["wg"], pw["rgv"], pw["scv"], pw["w_out"], *riders)


def _mixer_sample_kernel(x_ref, gmix_ref, w_in_ref, wg_ref, rgv_ref, scv_ref, w_out_ref,
                         conv_in_ref, h_in_ref, sc_in_ref,
                         o_ref, conv_ref, h_ref, sc_ref, xcs, hss, ucs):
    nb = DEC_BATCH
    x = x_ref[...]
    xn = _rms(x, gmix_ref[...]).astype(BF16)

    rg_x = _dot(xn, w_in_ref[:, 0:D_RG])
    seq = [conv_in_ref[k] for k in range(RG_CONV_W - 1)]
    seq += [rg_x[t * nb:(t + 1) * nb, :] for t in range(DEC_SEQ)]
    for t in range(DEC_SEQ):
        xc_t = seq[t] * rgv_ref[0:1, :]
        for k in range(1, RG_CONV_W):
            xc_t = xc_t + seq[t + k] * rgv_ref[k:k + 1, :]
        xcs[t * nb:(t + 1) * nb, :] = xc_t + rgv_ref[4:5, :]
    for k in range(RG_CONV_W - 1):
        conv_ref[k] = seq[DEC_SEQ + k]

    a, b = _rg_gate_inputs(xcs[...], wg_ref, rgv_ref)
    h = h_in_ref[...]
    for t in range(DEC_SEQ):
        h = a[t * nb:(t + 1) * nb, :] * h + b[t * nb:(t + 1) * nb, :]
        hss[t * nb:(t + 1) * nb, :] = h
    h_ref[...] = h

    rg_gate = _dot(xn, w_in_ref[:, D_RG:2 * D_RG])
    sc_b = _dot(xn, w_in_ref[:, 2 * D_RG:2 * D_RG + D_SC])
    sc_c = _dot(xn, w_in_ref[:, 2 * D_RG + D_SC:2 * D_RG + 2 * D_SC])
    sc_x = _dot(xn, w_in_ref[:, 2 * D_RG + 2 * D_SC:])
    u = sc_c * sc_x
    useq = [sc_in_ref[:, k * D_SC:(k + 1) * D_SC] for k in range(SC_CONV_W - 1)]
    useq += [u[t * nb:(t + 1) * nb, :] for t in range(DEC_SEQ)]
    for t in range(DEC_SEQ):
        uc_t = useq[t] * scv_ref[0:1, :]
        for k in range(1, SC_CONV_W):
            uc_t = uc_t + useq[t + k] * scv_ref[k:k + 1, :]
        ucs[t * nb:(t + 1) * nb, :] = uc_t
    for k in range(SC_CONV_W - 1):
        sc_ref[:, k * D_SC:(k + 1) * D_SC] = useq[DEC_SEQ + k]

    o_ref[...] = x + _mix_out(hss[...], rg_gate, sc_b, ucs[...], scv_ref, w_out_ref)


def _mixer_sample(x, l, pw, w_in, conv_in, h_in, sc_in):
    conv_shape = (RG_CONV_W - 1, DEC_BATCH, D_RG)
    sc_w = (SC_CONV_W - 1) * D_SC
    return pl.pallas_call(
        _mixer_sample_kernel,
        grid=(1,),
        in_specs=[
            pl.BlockSpec((N_SAMPLE, D_MODEL), lambda i: (0, 0)),
        ] + _mixer_weight_specs(l) + [
            _layer_resident(conv_shape, l),
            _layer_resident((DEC_BATCH, D_RG), l),
            _layer_resident((DEC_BATCH, sc_w), l),
        ],
        out_specs=[
            pl.BlockSpec((N_SAMPLE, D_MODEL), lambda i: (0, 0)),
            pl.BlockSpec(conv_shape, lambda i: (0, 0, 0)),
            pl.BlockSpec((DEC_BATCH, D_RG), lambda i: (0, 0)),
            pl.BlockSpec((DEC_BATCH, sc_w), lambda i: (0, 0)),
        ],
        out_shape=[
            jax.ShapeDtypeStruct((N_SAMPLE, D_MODEL), F32),
            jax.ShapeDtypeStruct(conv_shape, F32),
            jax.ShapeDtypeStruct((DEC_BATCH, D_RG), F32),
            jax.ShapeDtypeStruct((DEC_BATCH, sc_w), F32),
        ],
        scratch_shapes=[
            pltpu.VMEM((N_SAMPLE, D_RG), F32),
            pltpu.VMEM((N_SAMPLE, D_RG), F32),
            pltpu.VMEM((N_SAMPLE, D_SC), F32),
        ],
        compiler_params=_params(("arbitrary",)),
        name="mixer_sample",
    )(x, pw["gmix"], w_in, pw["wg"], pw["rgv"], pw["scv"], pw["w_out"],
      conv_in, h_in, sc_in)


def _mem_kv_kernel(m_ref, g_ref, wk_ref, wv_ref, *rest, n_riders):
    rider_src, rest = rest[:n_riders], rest[n_riders:]
    k_ref, v_ref = rest[:2]
    rider_dst, (mn_s,) = rest[2:2 + n_riders], rest[2 + n_riders:]
    _round_riders(rider_src, rider_dst)

    @pl.when(pl.program_id(1) == 0)
    def _():
        mn_s[...] = _rms(m_ref[...], g_ref[...]).astype(BF16)

    mn = mn_s[...]
    k_ref[...] = _dot(mn, wk_ref[...].astype(BF16))
    v_ref[...] = _dot(mn, wv_ref[...].astype(BF16))


def _mem_kv(mem, gmem, w_k, w_v, riders, tn=512):
    rows = BATCH * N_MEM
    nj = D_MODEL // tn
    w_spec = pl.BlockSpec((None, D_MODEL, tn), lambda l, j: (l, 0, j))
    o_spec = pl.BlockSpec((None, rows, tn), lambda l, j: (l, 0, j))
    shape = jax.ShapeDtypeStruct((DEPTH, rows, D_MODEL), F32)
    r_in, r_out, r_shape = _rider_specs(riders, 0, DEPTH * nj, lambda l, j: l * nj + j)
    return pl.pallas_call(
        functools.partial(_mem_kv_kernel, n_riders=len(riders)),
        grid=(DEPTH, nj),
        in_specs=[
            pl.BlockSpec((rows, D_MODEL), lambda l, j: (0, 0), pipeline_mode=pl.Buffered(1)),
            pl.BlockSpec((None, 1, D_MODEL), lambda l, j: (l, 0, 0)),
            w_spec, w_spec,
        ] + r_in,
        out_specs=[o_spec, o_spec] + r_out,
        out_shape=[shape, shape] + r_shape,
        scratch_shapes=[pltpu.VMEM((rows, D_MODEL), BF16)],
        compiler_params=_params(("arbitrary",) * 2),
        name="mem_kv",
    )(mem, gmem, w_k, w_v, *riders)


def _q_sample_kernel(x_ref, g_ref, w_ref, o_ref):
    xn = _rms(x_ref[...], g_ref[...]).astype(BF16)
    o_ref[...] = _dot(xn, w_ref[...])


def _q_sample(x, l, pw, wq, tn=1024):
    return pl.pallas_call(
        _q_sample_kernel,
        grid=(D_MODEL // tn,),
        in_specs=[
            pl.BlockSpec((N_SAMPLE, D_MODEL), lambda j: (0, 0)),
            _layer_block((1, D_MODEL), l),
            pl.BlockSpec((D_MODEL, tn), lambda j: (0, j)),
        ],
        out_specs=pl.BlockSpec((N_SAMPLE, tn), lambda j: (0, j)),
        out_shape=jax.ShapeDtypeStruct((N_SAMPLE, D_MODEL), F32),
        compiler_params=_params(("arbitrary",)),
        name="q_sample",
    )(x, pw["gxa"], wq)


def _oproj_sample_kernel(a_ref, w_ref, res_ref, o_ref):
    o_ref[...] = res_ref[...] + _dot(a_ref[...].astype(BF16), w_ref[...])


def _oproj_sample(a, wo, res, tn=1024):
    return pl.pallas_call(
        _oproj_sample_kernel,
        grid=(D_MODEL // tn,),
        in_specs=[
            pl.BlockSpec((N_SAMPLE, D_MODEL), lambda j: (0, 0)),
            pl.BlockSpec((D_MODEL, tn), lambda j: (0, j)),
            pl.BlockSpec((N_SAMPLE, tn), lambda j: (0, j)),
        ],
        out_specs=pl.BlockSpec((N_SAMPLE, tn), lambda j: (0, j)),
        out_shape=jax.ShapeDtypeStruct((N_SAMPLE, D_MODEL), F32),
        compiler_params=_params(("arbitrary",)),
        name="oproj_sample",
    )(a, wo, res)


def _softmax_rows(s):
    m = jnp.max(s, axis=-1, keepdims=True)
    e = jnp.exp(s - m)
    return e / jnp.sum(e, axis=-1, keepdims=True)


def _attn_prompt_kernel(x_ref, g_ref, wq_ref, k_ref, v_ref, wo_ref, *rest, n_riders):
    rider_src, o_ref, rider_dst = rest[:n_riders], rest[n_riders], rest[n_riders + 1:]
    _round_riders(rider_src, rider_dst)
    x = x_ref[...]
    xn = _rms(x, g_ref[...]).astype(BF16)
    q = _dot(xn, wq_ref[...]).astype(BF16)
    k = k_ref[...].astype(BF16)
    v = v_ref[...].astype(BF16)
    outs = []
    for h in range(XA_HEADS):
        sl = slice(h * XA_HEAD_DIM, (h + 1) * XA_HEAD_DIM)
        s = lax.dot_general(q[:, sl], k[:, sl], (((1,), (1,)), ((), ())),
                            preferred_element_type=F32) / math.sqrt(XA_HEAD_DIM)
        p = _softmax_rows(s).astype(BF16)
        outs.append(_dot(p, v[:, sl]).astype(BF16))
    o = jnp.concatenate(outs, axis=1)
    o_ref[...] = x + _dot(o, wo_ref[...])


def _attn_prompt(x, l, pw, wq, wo, k, v, riders, rider_layer):
    nt = SEQ // TM_ATT
    kv_spec = pl.BlockSpec((None, N_MEM, D_MODEL), lambda b, t: (l, b, 0))
    r_in, r_out, r_shape = _rider_specs(riders, rider_layer, BATCH * nt, lambda b, t: b * nt + t)
    return pl.pallas_call(
        functools.partial(_attn_prompt_kernel, n_riders=len(riders)),
        grid=(BATCH, nt),
        in_specs=[
            pl.BlockSpec((TM_ATT, D_MODEL), lambda b, t: (b * nt + t, 0)),
            _layer_resident((1, D_MODEL), l),
            _whole_resident((D_MODEL, D_MODEL)),
            kv_spec, kv_spec,
            _whole_resident((D_MODEL, D_MODEL)),
        ] + r_in,
        out_specs=[pl.BlockSpec((TM_ATT, D_MODEL), lambda b, t: (b * nt + t, 0))] + r_out,
        out_shape=[jax.ShapeDtypeStruct((N_PROMPT, D_MODEL), F32)] + r_shape,
        compiler_params=_params(("arbitrary", "arbitrary")),
        name="attn_prompt",
    )(x, pw["gxa"], wq, k, v, wo, *riders)


def _head_matrix(mem_ref, stage_ref, b, h):
    n = N_MEM * HD_CHUNKS
    stage_ref[...] = mem_ref[pl.ds(b * n * XA_HEADS + h, n, stride=XA_HEADS), :]
    chunks = [stage_ref[pl.ds(c, N_MEM, stride=HD_CHUNKS), :] for c in range(HD_CHUNKS)]
    return jnp.concatenate(chunks, axis=1).astype(BF16)


def _attn_sample_kernel(q_ref, k_ref, v_ref, o_ref, stage_ref):
    rows = DEC_SEQ * ATT_BB
    keys = ATT_BB * N_MEM
    row_b = lax.broadcasted_iota(jnp.int32, (rows, keys), 0) // DEC_SEQ
    key_b = lax.broadcasted_iota(jnp.int32, (rows, keys), 1) // N_MEM
    own = row_b == key_b
    outs = []
    for h in range(XA_HEADS):
        q = q_ref[:, h * XA_HEAD_DIM:(h + 1) * XA_HEAD_DIM].astype(BF16)
        k = jnp.concatenate([_head_matrix(k_ref, stage_ref, b, h) for b in range(ATT_BB)], axis=0)
        v = jnp.concatenate([_head_matrix(v_ref, stage_ref, b, h) for b in range(ATT_BB)], axis=0)
        s = lax.dot_general(q, k, (((1,), (1,)), ((), ())),
                            preferred_element_type=F32) / math.sqrt(XA_HEAD_DIM)
        s = jnp.where(own, s, -jnp.inf)
        m = jnp.max(s, axis=-1, keepdims=True)
        e = jnp.where(own, jnp.exp(s - m), 0.0)
        p = (e / jnp.sum(e, axis=-1, keepdims=True)).astype(BF16)
        outs.append(_dot(p, v))
    o_ref[...] = jnp.concatenate(outs, axis=1)


def _lane_pieces(cache):
    c = cache.reshape(DEPTH, DEC_BATCH, N_MEM, XA_HEADS, HD_CHUNKS, LANES)
    c = jnp.transpose(c, (0, 1, 2, 4, 3, 5))
    return c.reshape(DEPTH * DEC_BATCH * N_MEM * HD_CHUNKS * XA_HEADS, LANES)


def _attn_sample(q, l, k, v):
    rows = ATT_BB * DEC_SEQ
    blk = ATT_BB * N_MEM * HD_CHUNKS * XA_HEADS
    nb = DEC_BATCH // ATT_BB
    kv_spec = pl.BlockSpec((blk, LANES), lambda i: (l * nb + i, 0))
    return pl.pallas_call(
        _attn_sample_kernel,
        grid=(nb,),
        in_specs=[pl.BlockSpec((rows, D_MODEL), lambda i: (i, 0)), kv_spec, kv_spec],
        out_specs=pl.BlockSpec((rows, D_MODEL), lambda i: (i, 0)),
        out_shape=jax.ShapeDtypeStruct((N_SAMPLE, D_MODEL), F32),
        scratch_shapes=[pltpu.VMEM((N_MEM * HD_CHUNKS, LANES), F32)],
        compiler_params=_params(("arbitrary",)),
        name="attn_sample",
    )(q, k, v)


def _route(logits):
    lane = lax.broadcasted_iota(jnp.int32, logits.shape, 1)
    neg = -jnp.inf
    is_g = lane < N_GROUPS
    lg = jnp.where(is_g, logits, neg)
    mg = jnp.max(lg, axis=-1, keepdims=True)
    eg = jnp.where(is_g, jnp.exp(lg - mg), 0.0)
    p_g = eg / jnp.sum(eg, axis=-1, keepdims=True)
    pg_sel = jnp.max(p_g, axis=-1, keepdims=True)
    g_sel = jnp.min(jnp.where(p_g == pg_sel, lane, LANES), axis=-1, keepdims=True)

    lo = N_GROUPS + g_sel * EXPERTS_PER_GROUP
    in_grp = (lane >= lo) & (lane < lo + EXPERTS_PER_GROUP)
    le = jnp.where(in_grp, logits, neg)
    me = jnp.max(le, axis=-1, keepdims=True)
    ee = jnp.where(in_grp, jnp.exp(le - me), 0.0)
    p_e = ee / jnp.sum(ee, axis=-1, keepdims=True)

    p1 = jnp.max(p_e, axis=-1, keepdims=True)
    i1 = jnp.min(jnp.where(in_grp & (p_e == p1), lane, LANES), axis=-1, keepdims=True)
    rest = in_grp & (lane != i1)
    p2 = jnp.max(jnp.where(rest, p_e, neg), axis=-1, keepdims=True)
    i2 = jnp.min(jnp.where(rest & (p_e == p2), lane, LANES), axis=-1, keepdims=True)
    tot = p1 + p2
    w1 = p1 / tot * pg_sel
    w2 = p2 / tot * pg_sel
    gates = jnp.where(lane == i1, w1, 0.0) + jnp.where(lane == i2, w2, 0.0)

    local = jnp.zeros_like(gates)
    for g in range(N_GROUPS):
        start = N_GROUPS + g * EXPERTS_PER_GROUP
        local = local + jnp.where(lane < EXPERTS_PER_GROUP,
                                  pltpu.roll(gates, LANES - start, axis=1), 0.0)

    ja = jnp.minimum(i1, i2) - lo
    jb = jnp.maximum(i1, i2) - lo
    code = ja * EXPERTS_PER_GROUP + jb
    pair = jnp.full_like(code, N_PAIRS - 1)
    for p in range(N_PAIRS - 1):
        lo_j, hi_j = min(PAIR_A[p], PAIR_B[p]), max(PAIR_A[p], PAIR_B[p])
        pair = jnp.where(code == lo_j * EXPERTS_PER_GROUP + hi_j, p, pair)
    bucket = (g_sel * N_PAIRS + pair).astype(F32)
    return local + jnp.where(lane == BUCKET_LANE, bucket, 0.0)


def _router_kernel(x_ref, g_ref, whi_ref, wlo_ref, br_ref, o_ref):
    xn = _rms(x_ref[...], g_ref[...])
    hi = xn.astype(BF16)
    lo = (xn - hi.astype(F32)).astype(BF16)
    w_hi = whi_ref[...]
    logits = _dot(hi, w_hi) + _dot(lo, w_hi) + _dot(hi, wlo_ref[...]) + br_ref[...]
    o_ref[...] = _route(logits)


def _router_riders_kernel(x_ref, g_ref, whi_ref, wlo_ref, br_ref, *rest, n_riders):
    rider_src, o_ref, rider_dst = rest[:n_riders], rest[n_riders], rest[n_riders + 1:]
    _round_riders(rider_src, rider_dst)
    _router_kernel(x_ref, g_ref, whi_ref, wlo_ref, br_ref, o_ref)


def _router_specs(l):
    return [
        pl.BlockSpec((TM_ROUTE, D_MODEL), lambda i: (i, 0)),
        _layer_block((1, D_MODEL), l),
        _layer_block((D_MODEL, LANES), l),
        _layer_block((D_MODEL, LANES), l),
        _layer_block((1, LANES), l),
    ]


def _router(x, l, pw):
    n_rows = x.shape[0]
    return pl.pallas_call(
        _router_kernel,
        grid=(n_rows // TM_ROUTE,),
        in_specs=_router_specs(l),
        out_specs=pl.BlockSpec((TM_ROUTE, LANES), lambda i: (i, 0)),
        out_shape=jax.ShapeDtypeStruct((n_rows, LANES), F32),
        compiler_params=_params(("arbitrary",)),
        name="router",
    )(x, pw["gmoe"], pw["wr_hi"], pw["wr_lo"], pw["br"])


def _router_prompt(x, l, pw, riders):
    n_steps = N_PROMPT // TM_ROUTE
    r_in, r_out, r_shape = _rider_specs(riders, l, n_steps, lambda i: i)
    return pl.pallas_call(
        functools.partial(_router_riders_kernel, n_riders=len(riders)),
        grid=(n_steps,),
        in_specs=_router_specs(l) + r_in,
        out_specs=[pl.BlockSpec((TM_ROUTE, LANES), lambda i: (i, 0))] + r_out,
        out_shape=[jax.ShapeDtypeStruct((N_PROMPT, LANES), F32)] + r_shape,
        compiler_params=_params(("arbitrary",)),
        name="router_prompt",
    )(x, pw["gmoe"], pw["wr_hi"], pw["wr_lo"], pw["br"], *riders)


def _dispatch_tables(rinfo):
    i32 = jnp.int32
    bucket = rinfo[:, BUCKET_LANE].astype(i32)
    bucket_ids = jnp.arange(N_BUCKETS, dtype=i32)
    onehot = (bucket[:, None] == bucket_ids[None, :]).astype(i32)
    csum = jnp.cumsum(onehot, axis=0)
    counts = csum[-1]
    tiles_per = (counts + TG - 1) // TG
    tile_end = jnp.cumsum(tiles_per)
    tile_start = tile_end - tiles_per
    slot = jnp.sum(onehot * (tile_start[None, :] * TG + csum - 1), axis=1)

    tok1 = jnp.arange(1, N_TOK + 1, dtype=i32).astype(F32)
    payload = jnp.concatenate([tok1[:, None], rinfo[:, :EXPERTS_PER_GROUP]], axis=1)
    slots = jnp.zeros((N_SLOTS, 1 + EXPERTS_PER_GROUP), F32).at[slot].set(
        payload, unique_indices=True)
    slot_tok = slots[:, 0].astype(i32) - 1

    tile = jnp.arange(N_TILES, dtype=i32)
    n_valid = tile_end[-1]
    tq = jnp.minimum(tile, n_valid - 1)
    tile_bucket = jnp.sum((tile_end[None, :] <= tq[:, None]).astype(i32), axis=1)
    tile_oh = (tile_bucket[:, None] == bucket_ids[None, :]).astype(i32)
    in_bucket = tile - jnp.sum(tile_oh * tile_start[None, :], axis=1)
    tile_cnt = jnp.clip(jnp.sum(tile_oh * counts[None, :], axis=1) - in_bucket * TG, 0, TG)
    tile_cnt = jnp.where(tile < n_valid, tile_cnt, 0)
    pair_a = jnp.asarray([(b // N_PAIRS) * EXPERTS_PER_GROUP + PAIR_A[b % N_PAIRS]
                          for b in range(N_BUCKETS)], i32)
    pair_b = jnp.asarray([(b // N_PAIRS) * EXPERTS_PER_GROUP + PAIR_B[b % N_PAIRS]
                          for b in range(N_BUCKETS)], i32)
    tile_ea = jnp.sum(tile_oh * pair_a[None, :], axis=1)
    tile_eb = jnp.sum(tile_oh * pair_b[None, :], axis=1)
    is_prompt = (slot_tok >= 0) & (slot_tok < N_PROMPT)
    tile_np = jnp.sum(is_prompt.reshape(N_TILES, TG).astype(i32), axis=1)
    return tile_ea, tile_eb, tile_cnt, tile_np, slot_tok.reshape(N_TILES, 1, TG), slots


def _experts_kernel(ea_ref, eb_ref, cnt_ref, np_ref,
                    xp_hbm, xs_hbm, tokp_ref, tokc_ref, tokn_ref, gates_ref, g_ref,
                    wga_ref, wua_ref, wda_ref, wgb_ref, wub_ref, wdb_ref, gfin_ref,
                    op_hbm, os_hbm, xbuf, obuf, gsem, ssem, *, final_norm, sample_batch_major):
    t = pl.program_id(0)
    nt = pl.num_programs(0)
    buf = lax.rem(t, 2)
    t_prev = jnp.maximum(t - 1, 0)
    t_next = jnp.minimum(t + 1, nt - 1)
    cnt = cnt_ref[t]
    has_next = (t + 1 < nt) & (cnt_ref[t_next] > 0)

    def sample_dst(r):
        if not sample_batch_major:
            return r
        return (r % DEC_BATCH) * DEC_SEQ + r // DEC_BATCH

    def row_slot(r):
        return lax.shift_right_logical(r, 3), lax.bitwise_and(r, SUBLANES - 1)

    def gather_p(tok_ref, b, r, g, s):
        return pltpu.make_async_copy(
            xp_hbm.at[pl.ds(tok_ref[0, r], 1)], xbuf.at[b, g, pl.ds(s, 1)], gsem.at[b])

    def gather_s(tok_ref, b, r, g, s):
        return pltpu.make_async_copy(
            xs_hbm.at[pl.ds(tok_ref[0, r] - N_PROMPT, 1)], xbuf.at[b, g, pl.ds(s, 1)], gsem.at[b])

    def scatter_p(tok_ref, b, r, g, s):
        return pltpu.make_async_copy(
            obuf.at[b, g, pl.ds(s, 1)], op_hbm.at[pl.ds(tok_ref[0, r], 1)], ssem.at[b])

    def scatter_s(tok_ref, b, r, g, s):
        dst = sample_dst(tok_ref[0, r] - N_PROMPT)
        return pltpu.make_async_copy(
            obuf.at[b, g, pl.ds(s, 1)], os_hbm.at[pl.ds(dst, 1)], ssem.at[b])

    def for_range(lo, hi, make_copy, op):
        full = (hi - lo) // SUBLANES
        aligned = isinstance(lo, int) and lo == 0

        def group(i, c):
            for j in range(SUBLANES):
                r = lo + i * SUBLANES + j
                op(make_copy(r, i, j) if aligned else make_copy(r, *row_slot(r)))
            return c

        def single(r, c):
            op(make_copy(r, *row_slot(r)))
            return c
        lax.fori_loop(0, full, group, 0)
        lax.fori_loop(lo + full * SUBLANES, hi, single, 0)

    def gather(tile, tok_ref, b, op):
        for_range(0, np_ref[tile], functools.partial(gather_p, tok_ref, b), op)
        for_range(np_ref[tile], cnt_ref[tile], functools.partial(gather_s, tok_ref, b), op)

    def scatter(tile, tok_ref, b, op):
        for_range(0, np_ref[tile], functools.partial(scatter_p, tok_ref, b), op)
        for_range(np_ref[tile], cnt_ref[tile], functools.partial(scatter_s, tok_ref, b), op)

    def start(copy):
        copy.start()

    def wait(copy):
        copy.wait()

    @pl.when(t == 0)
    def _():
        xbuf[...] = jnp.zeros_like(xbuf)
        gather(t, tokc_ref, 0, start)

    @pl.when(has_next)
    def _():
        gather(t_next, tokn_ref, 1 - buf, start)

    @pl.when(cnt > 0)
    def _():
        gather(t, tokc_ref, buf, wait)

        ja = ea_ref[t] % EXPERTS_PER_GROUP + SLOT_GATE_LANE
        jb = eb_ref[t] % EXPERTS_PER_GROUP + SLOT_GATE_LANE
        n_chunks = (cnt + ROW_CHUNK - 1) // ROW_CHUNK

        def compute(m):
            x = xbuf[buf, 0:m // SUBLANES].reshape(m, D_MODEL)
            xn = _rms(x, g_ref[...]).astype(BF16)
            gates = gates_ref[0:m, :]
            lane = lax.broadcasted_iota(jnp.int32, gates.shape, 1)
            ga = jnp.sum(jnp.where(lane == ja, gates, 0.0), axis=-1, keepdims=True)
            gb = jnp.sum(jnp.where(lane == jb, gates, 0.0), axis=-1, keepdims=True)
            act_a = (jax.nn.silu(_dot(xn, wga_ref[...])) * _dot(xn, wua_ref[...]) * ga).astype(BF16)
            act_b = (jax.nn.silu(_dot(xn, wgb_ref[...])) * _dot(xn, wub_ref[...]) * gb).astype(BF16)
            y = x + (_dot(act_a, wda_ref[...]) + _dot(act_b, wdb_ref[...]))
            if final_norm:
                y = _rms(y, gfin_ref[...])
            obuf[buf, 0:m // SUBLANES] = y.reshape(m // SUBLANES, SUBLANES, D_MODEL)

        for k in range(1, TG // ROW_CHUNK + 1):
            pl.when(n_chunks == k)(functools.partial(compute, k * ROW_CHUNK))

        @pl.when(t > 0)
        def _():
            scatter(t_prev, tokp_ref, 1 - buf, wait)

        scatter(t, tokc_ref, buf, start)

        @pl.when(jnp.logical_not(has_next))
        def _():
            scatter(t, tokc_ref, buf, wait)


def _experts(xp, xs, l, pw, expert_w, gfin, tables, final_norm):
    tile_ea, tile_eb, tile_cnt, tile_np, slot_tok, gates = tables
    w_gate, w_up, w_down = expert_w

    def tok_spec(shift):
        return pl.BlockSpec(
            (None, 1, TG),
            lambda t, *_: (jnp.clip(t + shift, 0, N_TILES - 1), 0, 0),
            memory_space=pltpu.SMEM)

    def w_spec(shape, which):
        return pl.BlockSpec(
            (None,) + shape,
            lambda t, ea, eb, *_: ((ea, eb)[which][t], 0, 0))

    any_spec = pl.BlockSpec(memory_space=pl.ANY)
    grid_spec = pltpu.PrefetchScalarGridSpec(
        num_scalar_prefetch=4,
        grid=(N_TILES,),
        in_specs=[
            any_spec, any_spec,
            tok_spec(-1), tok_spec(0), tok_spec(1),
            pl.BlockSpec((TG, SLOT_GATE_LANE + EXPERTS_PER_GROUP), lambda t, *_: (t, 0)),
            pl.BlockSpec((None, 1, D_MODEL), lambda t, *_: (l, 0, 0)),
            w_spec((D_MODEL, D_FF), 0), w_spec((D_MODEL, D_FF), 0), w_spec((D_FF, D_MODEL), 0),
            w_spec((D_MODEL, D_FF), 1), w_spec((D_MODEL, D_FF), 1), w_spec((D_FF, D_MODEL), 1),
            pl.BlockSpec((1, D_MODEL), lambda t, *_: (0, 0)),
        ],
        out_specs=[any_spec, any_spec],
        scratch_shapes=[
            pltpu.VMEM((2, TG // SUBLANES, SUBLANES, D_MODEL), F32),
            pltpu.VMEM((2, TG // SUBLANES, SUBLANES, D_MODEL), F32),
            pltpu.SemaphoreType.DMA((2,)),
            pltpu.SemaphoreType.DMA((2,)),
        ],
    )
    return pl.pallas_call(
        functools.partial(_experts_kernel, final_norm=final_norm, sample_batch_major=final_norm),
        grid_spec=grid_spec,
        out_shape=[jax.ShapeDtypeStruct((N_PROMPT, D_MODEL), F32),
                   jax.ShapeDtypeStruct((N_SAMPLE, D_MODEL), F32)],
        compiler_params=_params(("arbitrary",)),
        name="experts",
    )(tile_ea, tile_eb, tile_cnt, tile_np, xp, xs, slot_tok, slot_tok, slot_tok, gates,
      pw["gmoe"], w_gate, w_up, w_down, w_gate, w_up, w_down, gfin)


def _block_diag_gates(w_a, w_x):
    def diag(w):
        w = w.reshape(DEPTH, N_GATE_BLOCKS, HEADS_PER_GATE_BLOCK, RG_HEAD_DIM, RG_HEAD_DIM)
        eye = jnp.eye(HEADS_PER_GATE_BLOCK, dtype=w.dtype)
        full = jnp.einsum("lqhij,hk->lqhikj", w, eye)
        return full.reshape(DEPTH, N_GATE_BLOCKS, MXU_DIM, MXU_DIM)
    return jnp.concatenate([diag(w_a), diag(w_x)], axis=-1).astype(BF16)


def _to_batch_major(a):
    return jnp.transpose(a.reshape(DEC_SEQ, DEC_BATCH, -1), (1, 0, 2)).reshape(N_SAMPLE, -1)


def _to_time_major(a):
    return jnp.transpose(a.reshape(DEC_BATCH, DEC_SEQ, -1), (1, 0, 2)).reshape(N_SAMPLE, -1)


def _stack_rows(rows, n):
    width = rows[0].shape[-1]
    rows = [r.reshape(DEPTH, -1, width) for r in rows]
    have = sum(r.shape[1] for r in rows)
    if have < n:
        rows.append(jnp.zeros((DEPTH, n - have, width), F32))
    return jnp.concatenate(rows, axis=1)


def kernel(x_prompt, x_sample, state_rglru_h, state_rglru_conv, state_sconv, cache_mem_k,
           cache_mem_v, mem_prompt, norm_mix, w_in, rg_conv_w, rg_conv_b, rg_w_a, rg_b_a,
           rg_w_x, rg_b_x, rg_lambda, sc_conv_w, norm_rg_out, norm_sc_out, w_out, norm_xattn,
           norm_mem, xa_w_q, xa_w_k, xa_w_v, xa_w_o, norm_moe, router_group_w, router_group_b,
           router_expert_w, router_expert_b, expert_w_gate, expert_w_up, expert_w_down,
           norm_final):
    wr = jnp.concatenate([router_group_w, router_expert_w], axis=2)
    wr = jnp.pad(wr, ((0, 0), (0, 0), (0, LANES - wr.shape[2])))
    wr_hi = wr.astype(BF16)
    wr_lo = (wr - wr_hi.astype(F32)).astype(BF16)
    br = jnp.concatenate([router_group_b, router_expert_b], axis=1)
    br = jnp.pad(br, ((0, 0), (0, LANES - br.shape[1]))).reshape(DEPTH, 1, LANES)
    pw = dict(
        gmix=norm_mix.reshape(DEPTH, 1, D_MODEL),
        wg=_block_diag_gates(rg_w_a, rg_w_x),
        rgv=_stack_rows([rg_conv_w, rg_conv_b, rg_b_a, rg_b_x, rg_lambda], 8),
        scv=_stack_rows([sc_conv_w, norm_rg_out, norm_sc_out], 8),
        gxa=norm_xattn.reshape(DEPTH, 1, D_MODEL),
        gmem=norm_mem.reshape(DEPTH, 1, D_MODEL),
        gmoe=norm_moe.reshape(DEPTH, 1, D_MODEL),
        wr_hi=wr_hi, wr_lo=wr_lo, br=br,
    )
    gfin = norm_final.reshape(1, D_MODEL)

    conv_in = jnp.transpose(state_rglru_conv, (0, 2, 1, 3))
    sc_in = state_sconv.reshape(DEPTH, DEC_BATCH, (SC_CONV_W - 1) * D_SC)
    w_gate_rows = expert_w_gate.reshape(DEPTH, N_EXPERTS * D_MODEL, D_FF)
    w_up_rows = expert_w_up.reshape(DEPTH, N_EXPERTS * D_MODEL, D_FF)
    w_down_rows = expert_w_down.reshape(DEPTH, N_EXPERTS * D_FF, D_MODEL)
    w_in_l = w_in[0].astype(BF16)
    cache_k = _lane_pieces(cache_mem_k)
    cache_v = _lane_pieces(cache_mem_v)

    mem = mem_prompt.reshape(BATCH * N_MEM, D_MODEL)
    p_k, p_v, w_out_bf16 = _mem_kv(mem, pw["gmem"], xa_w_k, xa_w_v,
                                   [w_out.reshape(1, DEPTH * D_MODEL, D_MODEL)])
    pw["w_out"] = w_out_bf16.reshape(DEPTH, D_MODEL, D_MODEL)

    xp = x_prompt.reshape(N_PROMPT, D_MODEL)
    xs = jnp.transpose(x_sample, (1, 0, 2)).reshape(N_SAMPLE, D_MODEL)

    p_conv, p_h, p_sc, s_conv, s_h, s_sc = [], [], [], [], [], []
    for l in range(DEPTH):
        xp, c, hh, sc, w_down_l, wq_l, wo_l = _mixer_prompt(
            xp, l, pw, w_in_l, [w_down_rows, xa_w_q, xa_w_o])
        xs, cs, hs, scs = _mixer_sample(xs, l, pw, w_in_l, conv_in, state_rglru_h, sc_in)
        p_conv.append(c)
        p_h.append(hh.reshape(BATCH, D_RG))
        p_sc.append(sc)
        s_conv.append(cs)
        s_h.append(hs)
        s_sc.append(scs.reshape(DEC_BATCH, SC_CONV_W - 1, D_SC))

        next_w_in = [w_in] if l + 1 < DEPTH else []
        xp, *w_in_next = _attn_prompt(xp, l, pw, wq_l, wo_l, p_k, p_v, next_w_in, l + 1)
        q_s = _to_batch_major(_q_sample(xs, l, pw, wq_l))
        o_s = _attn_sample(q_s, l, cache_k, cache_v)
        xs = _oproj_sample(_to_time_major(o_s), wo_l, xs)
        if w_in_next:
            w_in_l = w_in_next[0]

        rinfo_p, w_gate_l, w_up_l = _router_prompt(xp, l, pw, [w_gate_rows, w_up_rows])
        rinfo = jnp.concatenate([rinfo_p, _router(xs, l, pw)], axis=0)
        expert_w = (w_gate_l.reshape(N_EXPERTS, D_MODEL, D_FF),
                    w_up_l.reshape(N_EXPERTS, D_MODEL, D_FF),
                    w_down_l.reshape(N_EXPERTS, D_FF, D_MODEL))
        xp, xs = _experts(xp, xs, l, pw, expert_w, gfin, _dispatch_tables(rinfo),
                          final_norm=(l == DEPTH - 1))

    y_prompt = xp.reshape(BATCH, SEQ, D_MODEL)
    y_sample = xs.reshape(DEC_BATCH, DEC_SEQ, D_MODEL)
    mem_shape = (DEPTH, BATCH, N_MEM, XA_HEADS, XA_HEAD_DIM)
    return (y_prompt, y_sample,
            jnp.stack(p_h), jnp.stack(p_conv), jnp.stack(p_sc),
            p_k.reshape(mem_shape), p_v.reshape(mem_shape),
            jnp.stack(s_h), jnp.transpose(jnp.stack(s_conv), (0, 2, 1, 3)), jnp.stack(s_sc))
```

```python
import functools
import math

import jax
import jax.numpy as jnp
from jax import lax
from jax.experimental import pallas as pl
from jax.experimental.pallas import tpu as pltpu

D_MODEL = 2048
BATCH = 4
SEQ = 2048
DEPTH = 2
DEC_BATCH = 128
DEC_SEQ = 4
D_RG = 1024
D_SC = 1024
RG_HEADS = 16
RG_HEAD_DIM = 64
RG_CONV_W = 4
RG_C = 8.0
SC_CONV_W = 3
D_IN = 2 * D_RG + 3 * D_SC
N_MEM = 256
XA_HEADS = 4
XA_HEAD_DIM = 512
N_GROUPS = 4
EXPERTS_PER_GROUP = 4
N_EXPERTS = 16
D_FF = 512
EPS = 1e-6

N_PROMPT = BATCH * SEQ
N_SAMPLE = DEC_BATCH * DEC_SEQ
N_TOK = N_PROMPT + N_SAMPLE

V7X_VMEM_LIMIT_BYTES = 56 * 1024 * 1024
SUBLANES = 8
LANES = 128
MXU_DIM = 256

HEADS_PER_GATE_BLOCK = MXU_DIM // RG_HEAD_DIM
N_GATE_BLOCKS = D_RG // MXU_DIM

TM_MIX = 256
TM_ATT = 512
TM_ROUTE = 512
ATT_BB = 4
ATT_QB = 8
HD_CHUNKS = XA_HEAD_DIM // LANES

PAIR_A = (0, 0, 0, 1, 2, 2)
PAIR_B = (1, 2, 3, 3, 3, 1)
N_PAIRS = len(PAIR_A)
N_BUCKETS = N_GROUPS * N_PAIRS
TG = 512
ROW_CHUNK = 128
N_TILES = -(-(N_TOK + N_BUCKETS * (TG - 1)) // TG)
N_SLOTS = N_TILES * TG
BUCKET_LANE = EXPERTS_PER_GROUP
SLOT_GATE_LANE = 1

BF16 = jnp.bfloat16
F32 = jnp.float32


def _params(sem, vmem=V7X_VMEM_LIMIT_BYTES):
    return pltpu.CompilerParams(dimension_semantics=sem, vmem_limit_bytes=vmem)


def _layer_resident(shape, l):
    nd = len(shape)
    return pl.BlockSpec((None,) + shape, lambda *_: (l,) + (0,) * nd, pipeline_mode=pl.Buffered(1))


def _layer_block(shape, l):
    nd = len(shape)
    return pl.BlockSpec((None,) + shape, lambda *_: (l,) + (0,) * nd)


def _whole_resident(shape):
    nd = len(shape)
    return pl.BlockSpec(shape, lambda *_: (0,) * nd, pipeline_mode=pl.Buffered(1))


def _rider_specs(srcs, l, n_steps, step_of):
    in_specs, out_specs, out_shapes = [], [], []
    for a in srcs:
        _, r, c = a.shape
        rows = r // n_steps
        assert rows * n_steps == r and rows % 16 == 0
        in_specs.append(pl.BlockSpec((None, rows, c), lambda *g: (l, step_of(*g), 0)))
        out_specs.append(pl.BlockSpec((rows, c), lambda *g: (step_of(*g), 0)))
        out_shapes.append(jax.ShapeDtypeStruct((r, c), BF16))
    return in_specs, out_specs, out_shapes


def _round_riders(src_refs, dst_refs):
    for src, dst in zip(src_refs, dst_refs):
        dst[...] = src[...].astype(BF16)


def _rms(x, g):
    return x * lax.rsqrt(jnp.mean(x * x, axis=-1, keepdims=True) + EPS) * g


def _dot(a, b):
    return jnp.dot(a, b, preferred_element_type=F32)


def _rg_gate_block(xc_q, wg_ref, rgv_ref, q):
    sl = slice(q * MXU_DIM, (q + 1) * MXU_DIM)
    g = _dot(xc_q.astype(BF16), wg_ref[q])
    r = jax.nn.sigmoid(g[:, :MXU_DIM] + rgv_ref[5:6, sl])
    i = jax.nn.sigmoid(g[:, MXU_DIM:] + rgv_ref[6:7, sl])
    log_a = -RG_C * r * jax.nn.softplus(-rgv_ref[7:8, sl])
    a = jnp.exp(log_a)
    mult = jnp.sqrt(1.0 - a * a)
    return a, mult * (i * xc_q)


def _rg_gate_inputs(xc, wg_ref, rgv_ref):
    parts = [_rg_gate_block(xc[:, q * MXU_DIM:(q + 1) * MXU_DIM], wg_ref, rgv_ref, q)
             for q in range(N_GATE_BLOCKS)]
    return (jnp.concatenate([p[0] for p in parts], axis=1),
            jnp.concatenate([p[1] for p in parts], axis=1))


def _mix_out(hs, rg_gate, sc_b, uc, scv_ref, w_out_ref):
    rg_out = _rms(hs * jax.nn.gelu(rg_gate), scv_ref[3:4, :]).astype(BF16)
    sc_out = _rms(sc_b * uc, scv_ref[4:5, :]).astype(BF16)
    return _dot(rg_out, w_out_ref[0:D_RG, :]) + _dot(sc_out, w_out_ref[D_RG:, :])


def _shift_rows(x, d, fill):
    m = x.shape[0]
    if d % SUBLANES == 0:
        head = jnp.full((d, x.shape[1]), fill, x.dtype)
        return jnp.concatenate([head, x[:m - d]], axis=0)
    rolled = pltpu.roll(x, d, axis=0)
    row = lax.broadcasted_iota(jnp.int32, x.shape, 0)
    return jnp.where(row >= d, rolled, fill)


def _scan_rows(a, b, h0):
    m = a.shape[0]
    d = 1
    while d < m:
        a_sh = _shift_rows(a, d, 1.0)
        b_sh = _shift_rows(b, d, 0.0)
        b = a * b_sh + b
        a = a * a_sh
        d *= 2
    return a * h0 + b


def _mixer_prompt_kernel(x_ref, gmix_ref, w_in_ref, wg_ref, rgv_ref, scv_ref, w_out_ref, *rest,
                         n_riders):
    rider_src, rest = rest[:n_riders], rest[n_riders:]
    o_ref, conv_ref, h_ref, sc_ref = rest[:4]
    rider_dst, (rgx, usc, hcar) = rest[4:4 + n_riders], rest[4 + n_riders:]
    _round_riders(rider_src, rider_dst)
    t = pl.program_id(1)
    tm = x_ref.shape[0]

    @pl.when(t == 0)
    def _():
        rgx[0:SUBLANES, :] = jnp.zeros((SUBLANES, D_RG), F32)
        usc[0:SUBLANES, :] = jnp.zeros((SUBLANES, D_SC), F32)
        hcar[...] = jnp.zeros_like(hcar)

    x = x_ref[...]
    xn = _rms(x, gmix_ref[...]).astype(BF16)

    rg_x = _dot(xn, w_in_ref[:, 0:D_RG])
    rgx[SUBLANES:SUBLANES + tm, :] = rg_x

    w_cols = (slice(2 * D_RG + D_SC, 2 * D_RG + 2 * D_SC), slice(2 * D_RG + 2 * D_SC, D_IN),
              slice(2 * D_RG, 2 * D_RG + D_SC), slice(D_RG, 2 * D_RG))
    hs_parts, proj = [], []
    for q in range(N_GATE_BLOCKS):
        sl = slice(q * MXU_DIM, (q + 1) * MXU_DIM)
        xc = rgx[5:5 + tm, sl] * rgv_ref[0:1, sl]
        xc = xc + rgx[6:6 + tm, sl] * rgv_ref[1:2, sl]
        xc = xc + rgx[7:7 + tm, sl] * rgv_ref[2:3, sl]
        xc = xc + rg_x[:, sl] * rgv_ref[3:4, sl]
        xc = xc + rgv_ref[4:5, sl]
        a, b = _rg_gate_block(xc, wg_ref, rgv_ref, q)
        hs_parts.append(_scan_rows(a, b, hcar[0:1, sl]))
        proj.append(_dot(xn, w_in_ref[:, w_cols[q]]))
        if q == 1:
            sc_c, sc_x = proj
            u = sc_c * sc_x
            usc[SUBLANES:SUBLANES + tm, :] = u
            uc = usc[6:6 + tm, :] * scv_ref[0:1, :]
            uc = uc + usc[7:7 + tm, :] * scv_ref[1:2, :]
            uc = uc + u * scv_ref[2:3, :]
    hs = jnp.concatenate(hs_parts, axis=1)
    hcar[0:1, :] = hs[tm - 1:tm, :]
    sc_b, rg_gate = proj[2], proj[3]

    o_ref[...] = x + _mix_out(hs, rg_gate, sc_b, uc, scv_ref, w_out_ref)

    @pl.when(t == pl.num_programs(1) - 1)
    def _():
        conv_ref[0] = rgx[tm + 5:tm + 8, :]
        sc_ref[0] = usc[tm + 6:tm + 8, :]
        h_ref[0] = hs[tm - 1:tm, :]

    rgx[0:SUBLANES, :] = rgx[tm:tm + SUBLANES, :]
    usc[0:SUBLANES, :] = usc[tm:tm + SUBLANES, :]


def _mixer_weight_specs(l):
    return [
        _layer_resident((1, D_MODEL), l),
        _whole_resident((D_MODEL, D_IN)),
        _layer_resident((N_GATE_BLOCKS, MXU_DIM, 2 * MXU_DIM), l),
        _layer_resident((8, D_RG), l),
        _layer_resident((8, D_SC), l),
        _layer_resident((D_MODEL, D_MODEL), l),
    ]


def _mixer_prompt(x, l, pw, w_in, riders):
    nt = SEQ // TM_MIX
    r_in, r_out, r_shape = _rider_specs(riders, l, BATCH * nt, lambda b, t: b * nt + t)
    return pl.pallas_call(
        functools.partial(_mixer_prompt_kernel, n_riders=len(riders)),
        grid=(BATCH, nt),
        in_specs=[pl.BlockSpec((TM_MIX, D_MODEL), lambda b, t: (b * nt + t, 0))]
        + _mixer_weight_specs(l) + r_in,
        out_specs=[
            pl.BlockSpec((TM_MIX, D_MODEL), lambda b, t: (b * nt + t, 0)),
            pl.BlockSpec((1, RG_CONV_W - 1, D_RG), lambda b, t: (b, 0, 0)),
            pl.BlockSpec((1, 1, D_RG), lambda b, t: (b, 0, 0)),
            pl.BlockSpec((1, SC_CONV_W - 1, D_SC), lambda b, t: (b, 0, 0)),
        ] + r_out,
        out_shape=[
            jax.ShapeDtypeStruct((N_PROMPT, D_MODEL), F32),
            jax.ShapeDtypeStruct((BATCH, RG_CONV_W - 1, D_RG), F32),
            jax.ShapeDtypeStruct((BATCH, 1, D_RG), F32),
            jax.ShapeDtypeStruct((BATCH, SC_CONV_W - 1, D_SC), F32),
        ] + r_shape,
        scratch_shapes=[
            pltpu.VMEM((SUBLANES + TM_MIX, D_RG), F32),
            pltpu.VMEM((SUBLANES + TM_MIX, D_SC), F32),
            pltpu.VMEM((SUBLANES, D_RG), F32),
        ],
        compiler_params=_params(("arbitrary", "arbitrary")),
        name="mixer_prompt",
    )(x, pw["gmix"], w_in, pw["wg"], pw["rgv"], pw["scv"], pw["w_out"], *riders)


def _mixer_sample_kernel(x_ref, gmix_ref, w_in_ref, wg_ref, rgv_ref, scv_ref, w_out_ref,
                         conv_in_ref, h_in_ref, sc_in_ref,
                         o_ref, conv_ref, h_ref, sc_ref, xcs, hss, ucs):
    nb = DEC_BATCH
    x = x_ref[...]
    xn = _rms(x, gmix_ref[...]).astype(BF16)

    rg_x = _dot(xn, w_in_ref[:, 0:D_RG])
    seq = [conv_in_ref[k] for k in range(RG_CONV_W - 1)]
    seq += [rg_x[t * nb:(t + 1) * nb, :] for t in range(DEC_SEQ)]
    for t in range(DEC_SEQ):
        xc_t = seq[t] * rgv_ref[0:1, :]
        for k in range(1, RG_CONV_W):
            xc_t = xc_t + seq[t + k] * rgv_ref[k:k + 1, :]
        xcs[t * nb:(t + 1) * nb, :] = xc_t + rgv_ref[4:5, :]
    for k in range(RG_CONV_W - 1):
        conv_ref[k] = seq[DEC_SEQ + k]

    a, b = _rg_gate_inputs(xcs[...], wg_ref, rgv_ref)
    h = h_in_ref[...]
    for t in range(DEC_SEQ):
        h = a[t * nb:(t + 1) * nb, :] * h + b[t * nb:(t + 1) * nb, :]
        hss[t * nb:(t + 1) * nb, :] = h
    h_ref[...] = h

    rg_gate = _dot(xn, w_in_ref[:, D_RG:2 * D_RG])
    sc_b = _dot(xn, w_in_ref[:, 2 * D_RG:2 * D_RG + D_SC])
    sc_c = _dot(xn, w_in_ref[:, 2 * D_RG + D_SC:2 * D_RG + 2 * D_SC])
    sc_x = _dot(xn, w_in_ref[:, 2 * D_RG + 2 * D_SC:])
    u = sc_c * sc_x
    useq = [sc_in_ref[:, k * D_SC:(k + 1) * D_SC] for k in range(SC_CONV_W - 1)]
    useq += [u[t * nb:(t + 1) * nb, :] for t in range(DEC_SEQ)]
    for t in range(DEC_SEQ):
        uc_t = useq[t] * scv_ref[0:1, :]
        for k in range(1, SC_CONV_W):
            uc_t = uc_t + useq[t + k] * scv_ref[k:k + 1, :]
        ucs[t * nb:(t + 1) * nb, :] = uc_t
    for k in range(SC_CONV_W - 1):
        sc_ref[:, k * D_SC:(k + 1) * D_SC] = useq[DEC_SEQ + k]

    o_ref[...] = x + _mix_out(hss[...], rg_gate, sc_b, ucs[...], scv_ref, w_out_ref)


def _mixer_sample(x, l, pw, w_in, conv_in, h_in, sc_in):
    conv_shape = (RG_CONV_W - 1, DEC_BATCH, D_RG)
    sc_w = (SC_CONV_W - 1) * D_SC
    return pl.pallas_call(
        _mixer_sample_kernel,
        grid=(1,),
        in_specs=[
            pl.BlockSpec((N_SAMPLE, D_MODEL), lambda i: (0, 0)),
        ] + _mixer_weight_specs(l) + [
            _layer_resident(conv_shape, l),
            _layer_resident((DEC_BATCH, D_RG), l),
            _layer_resident((DEC_BATCH, sc_w), l),
        ],
        out_specs=[
            pl.BlockSpec((N_SAMPLE, D_MODEL), lambda i: (0, 0)),
            pl.BlockSpec(conv_shape, lambda i: (0, 0, 0)),
            pl.BlockSpec((DEC_BATCH, D_RG), lambda i: (0, 0)),
            pl.BlockSpec((DEC_BATCH, sc_w), lambda i: (0, 0)),
        ],
        out_shape=[
            jax.ShapeDtypeStruct((N_SAMPLE, D_MODEL), F32),
            jax.ShapeDtypeStruct(conv_shape, F32),
            jax.ShapeDtypeStruct((DEC_BATCH, D_RG), F32),
            jax.ShapeDtypeStruct((DEC_BATCH, sc_w), F32),
        ],
        scratch_shapes=[
            pltpu.VMEM((N_SAMPLE, D_RG), F32),
            pltpu.VMEM((N_SAMPLE, D_RG), F32),
            pltpu.VMEM((N_SAMPLE, D_SC), F32),
        ],
        compiler_params=_params(("arbitrary",)),
        name="mixer_sample",
    )(x, pw["gmix"], w_in, pw["wg"], pw["rgv"], pw["scv"], pw["w_out"],
      conv_in, h_in, sc_in)


def _mem_kv_kernel(m_ref, g_ref, wk_ref, wv_ref, *rest, n_riders):
    rider_src, rest = rest[:n_riders], rest[n_riders:]
    k_ref, v_ref = rest[:2]
    rider_dst, (mn_s,) = rest[2:2 + n_riders], rest[2 + n_riders:]
    _round_riders(rider_src, rider_dst)

    @pl.when(pl.program_id(1) == 0)
    def _():
        mn_s[...] = _rms(m_ref[...], g_ref[...]).astype(BF16)

    mn = mn_s[...]
    k_ref[...] = _dot(mn, wk_ref[...].astype(BF16))
    v_ref[...] = _dot(mn, wv_ref[...].astype(BF16))


def _mem_kv(mem, gmem, w_k, w_v, riders, tn=512):
    rows = BATCH * N_MEM
    nj = D_MODEL // tn
    w_spec = pl.BlockSpec((None, D_MODEL, tn), lambda l, j: (l, 0, j))
    o_spec = pl.BlockSpec((None, rows, tn), lambda l, j: (l, 0, j))
    shape = jax.ShapeDtypeStruct((DEPTH, rows, D_MODEL), F32)
    r_in, r_out, r_shape = _rider_specs(riders, 0, DEPTH * nj, lambda l, j: l * nj + j)
    return pl.pallas_call(
        functools.partial(_mem_kv_kernel, n_riders=len(riders)),
        grid=(DEPTH, nj),
        in_specs=[
            pl.BlockSpec((rows, D_MODEL), lambda l, j: (0, 0), pipeline_mode=pl.Buffered(1)),
            pl.BlockSpec((None, 1, D_MODEL), lambda l, j: (l, 0, 0)),
            w_spec, w_spec,
        ] + r_in,
        out_specs=[o_spec, o_spec] + r_out,
        out_shape=[shape, shape] + r_shape,
        scratch_shapes=[pltpu.VMEM((rows, D_MODEL), BF16)],
        compiler_params=_params(("arbitrary",) * 2),
        name="mem_kv",
    )(mem, gmem, w_k, w_v, *riders)


def _q_sample_kernel(x_ref, g_ref, w_ref, o_ref):
    xn = _rms(x_ref[...], g_ref[...]).astype(BF16)
    o_ref[...] = _dot(xn, w_ref[...])


def _q_sample(x, l, pw, wq, tn=1024):
    return pl.pallas_call(
        _q_sample_kernel,
        grid=(D_MODEL // tn,),
        in_specs=[
            pl.BlockSpec((N_SAMPLE, D_MODEL), lambda j: (0, 0)),
            _layer_block((1, D_MODEL), l),
            pl.BlockSpec((D_MODEL, tn), lambda j: (0, j)),
        ],
        out_specs=pl.BlockSpec((N_SAMPLE, tn), lambda j: (0, j)),
        out_shape=jax.ShapeDtypeStruct((N_SAMPLE, D_MODEL), F32),
        compiler_params=_params(("arbitrary",)),
        name="q_sample",
    )(x, pw["gxa"], wq)


def _oproj_sample_kernel(a_ref, w_ref, res_ref, o_ref):
    o_ref[...] = res_ref[...] + _dot(a_ref[...].astype(BF16), w_ref[...])


def _oproj_sample(a, wo, res, tn=1024):
    return pl.pallas_call(
        _oproj_sample_kernel,
        grid=(D_MODEL // tn,),
        in_specs=[
            pl.BlockSpec((N_SAMPLE, D_MODEL), lambda j: (0, 0)),
            pl.BlockSpec((D_MODEL, tn), lambda j: (0, j)),
            pl.BlockSpec((N_SAMPLE, tn), lambda j: (0, j)),
        ],
        out_specs=pl.BlockSpec((N_SAMPLE, tn), lambda j: (0, j)),
        out_shape=jax.ShapeDtypeStruct((N_SAMPLE, D_MODEL), F32),
        compiler_params=_params(("arbitrary",)),
        name="oproj_sample",
    )(a, wo, res)


def _softmax_rows(s):
    m = jnp.max(s, axis=-1, keepdims=True)
    e = jnp.exp(s - m)
    return e / jnp.sum(e, axis=-1, keepdims=True)


def _attn_prompt_kernel(x_ref, g_ref, wq_ref, k_ref, v_ref, wo_ref, *rest, n_riders):
    rider_src, o_ref, rider_dst = rest[:n_riders], rest[n_riders], rest[n_riders + 1:]
    _round_riders(rider_src, rider_dst)
    x = x_ref[...]
    xn = _rms(x, g_ref[...]).astype(BF16)
    q = _dot(xn, wq_ref[...]).astype(BF16)
    k = k_ref[...].astype(BF16)
    v = v_ref[...].astype(BF16)
    outs = []
    for h in range(XA_HEADS):
        sl = slice(h * XA_HEAD_DIM, (h + 1) * XA_HEAD_DIM)
        s = lax.dot_general(q[:, sl], k[:, sl], (((1,), (1,)), ((), ())),
                            preferred_element_type=F32) / math.sqrt(XA_HEAD_DIM)
        p = _softmax_rows(s).astype(BF16)
        outs.append(_dot(p, v[:, sl]).astype(BF16))
    o = jnp.concatenate(outs, axis=1)
    o_ref[...] = x + _dot(o, wo_ref[...])


def _attn_prompt(x, l, pw, wq, wo, k, v, riders, rider_layer):
    nt = SEQ // TM_ATT
    kv_spec = pl.BlockSpec((None, N_MEM, D_MODEL), lambda b, t: (l, b, 0))
    r_in, r_out, r_shape = _rider_specs(riders, rider_layer, BATCH * nt, lambda b, t: b * nt + t)
    return pl.pallas_call(
        functools.partial(_attn_prompt_kernel, n_riders=len(riders)),
        grid=(BATCH, nt),
        in_specs=[
            pl.BlockSpec((TM_ATT, D_MODEL), lambda b, t: (b * nt + t, 0)),
            _layer_resident((1, D_MODEL), l),
            _whole_resident((D_MODEL, D_MODEL)),
            kv_spec, kv_spec,
            _whole_resident((D_MODEL, D_MODEL)),
        ] + r_in,
        out_specs=[pl.BlockSpec((TM_ATT, D_MODEL), lambda b, t: (b * nt + t, 0))] + r_out,
        out_shape=[jax.ShapeDtypeStruct((N_PROMPT, D_MODEL), F32)] + r_shape,
        compiler_params=_params(("arbitrary", "arbitrary")),
        name="attn_prompt",
    )(x, pw["gxa"], wq, k, v, wo, *riders)


def _head_matrix(mem_ref, stage_ref, b, h):
    n = N_MEM * HD_CHUNKS
    stage_ref[...] = mem_ref[pl.ds(b * n * XA_HEADS + h, n, stride=XA_HEADS), :]
    chunks = [stage_ref[pl.ds(c, N_MEM, stride=HD_CHUNKS), :] for c in range(HD_CHUNKS)]
    return jnp.concatenate(chunks, axis=1).astype(BF16)


def _attn_sample_kernel(q_ref, k_ref, v_ref, o_ref, stage_ref):
    j = pl.program_id(1)
    rows = DEC_SEQ * ATT_QB
    keys = ATT_BB * N_MEM
    q_all = q_ref[...].reshape(rows, D_MODEL)
    row_b = lax.broadcasted_iota(jnp.int32, (rows, keys), 0) % ATT_QB
    key_b = lax.broadcasted_iota(jnp.int32, (rows, keys), 1) // N_MEM + j * ATT_BB
    own = row_b == key_b
    mine = (lax.broadcasted_iota(jnp.int32, (rows, 1), 0) % ATT_QB) // ATT_BB == j
    outs = []
    for h in range(XA_HEADS):
        q = q_all[:, h * XA_HEAD_DIM:(h + 1) * XA_HEAD_DIM].astype(BF16)
        k = jnp.concatenate([_head_matrix(k_ref, stage_ref, b, h) for b in range(ATT_BB)], axis=0)
        v = jnp.concatenate([_head_matrix(v_ref, stage_ref, b, h) for b in range(ATT_BB)], axis=0)
        s = lax.dot_general(q, k, (((1,), (1,)), ((), ())),
                            preferred_element_type=F32) / math.sqrt(XA_HEAD_DIM)
        s = jnp.where(own, s, -jnp.inf)
        m = jnp.where(mine, jnp.max(s, axis=-1, keepdims=True), 0.0)
        e = jnp.where(own, jnp.exp(s - m), 0.0)
        denom = jnp.where(mine, jnp.sum(e, axis=-1, keepdims=True), 1.0)
        p = (e / denom).astype(BF16)
        outs.append(_dot(p, v))
    o = jnp.concatenate(outs, axis=1).reshape(DEC_SEQ, ATT_QB, D_MODEL)

    @pl.when(j == 0)
    def _():
        o_ref[...] = o

    @pl.when(j > 0)
    def _():
        o_ref[...] += o


def _lane_pieces(cache):
    c = cache.reshape(DEPTH, DEC_BATCH, N_MEM, XA_HEADS, HD_CHUNKS, LANES)
    c = jnp.transpose(c, (0, 1, 2, 4, 3, 5))
    return c.reshape(DEPTH * DEC_BATCH * N_MEM * HD_CHUNKS * XA_HEADS, LANES)


def _attn_sample(q, l, k, v):
    blk = ATT_BB * N_MEM * HD_CHUNKS * XA_HEADS
    parts = ATT_QB // ATT_BB
    nb = DEC_BATCH // ATT_BB
    kv_spec = pl.BlockSpec((blk, LANES), lambda i, j: (l * nb + i * parts + j, 0))
    q_spec = pl.BlockSpec((DEC_SEQ, ATT_QB, D_MODEL), lambda i, j: (0, i, 0))
    return pl.pallas_call(
        _attn_sample_kernel,
        grid=(DEC_BATCH // ATT_QB, parts),
        in_specs=[q_spec, kv_spec, kv_spec],
        out_specs=q_spec,
        out_shape=jax.ShapeDtypeStruct((DEC_SEQ, DEC_BATCH, D_MODEL), F32),
        scratch_shapes=[pltpu.VMEM((N_MEM * HD_CHUNKS, LANES), F32)],
        compiler_params=_params(("arbitrary", "arbitrary")),
        name="attn_sample",
    )(q, k, v)


def _route(logits):
    lane = lax.broadcasted_iota(jnp.int32, logits.shape, 1)
    neg = -jnp.inf
    is_g = lane < N_GROUPS
    lg = jnp.where(is_g, logits, neg)
    mg = jnp.max(lg, axis=-1, keepdims=True)
    eg = jnp.where(is_g, jnp.exp(lg - mg), 0.0)
    p_g = eg / jnp.sum(eg, axis=-1, keepdims=True)
    pg_sel = jnp.max(p_g, axis=-1, keepdims=True)
    g_sel = jnp.min(jnp.where(p_g == pg_sel, lane, LANES), axis=-1, keepdims=True)

    lo = N_GROUPS + g_sel * EXPERTS_PER_GROUP
    in_grp = (lane >= lo) & (lane < lo + EXPERTS_PER_GROUP)
    le = jnp.where(in_grp, logits, neg)
    me = jnp.max(le, axis=-1, keepdims=True)
    ee = jnp.where(in_grp, jnp.exp(le - me), 0.0)
    p_e = ee / jnp.sum(ee, axis=-1, keepdims=True)

    p1 = jnp.max(p_e, axis=-1, keepdims=True)
    i1 = jnp.min(jnp.where(in_grp & (p_e == p1), lane, LANES), axis=-1, keepdims=True)
    rest = in_grp & (lane != i1)
    p2 = jnp.max(jnp.where(rest, p_e, neg), axis=-1, keepdims=True)
    i2 = jnp.min(jnp.where(rest & (p_e == p2), lane, LANES), axis=-1, keepdims=True)
    tot = p1 + p2
    w1 = p1 / tot * pg_sel
    w2 = p2 / tot * pg_sel
    gates = jnp.where(lane == i1, w1, 0.0) + jnp.where(lane == i2, w2, 0.0)

    local = jnp.zeros_like(gates)
    for g in range(N_GROUPS):
        start = N_GROUPS + g * EXPERTS_PER_GROUP
        local = local + jnp.where(lane < EXPERTS_PER_GROUP,
                                  pltpu.roll(gates, LANES - start, axis=1), 0.0)

    ja = jnp.minimum(i1, i2) - lo
    jb = jnp.maximum(i1, i2) - lo
    code = ja * EXPERTS_PER_GROUP + jb
    pair = jnp.full_like(code, N_PAIRS - 1)
    for p in range(N_PAIRS - 1):
        lo_j, hi_j = min(PAIR_A[p], PAIR_B[p]), max(PAIR_A[p], PAIR_B[p])
        pair = jnp.where(code == lo_j * EXPERTS_PER_GROUP + hi_j, p, pair)
    bucket = (g_sel * N_PAIRS + pair).astype(F32)
    return local + jnp.where(lane == BUCKET_LANE, bucket, 0.0)


def _router_kernel(x_ref, g_ref, whi_ref, wlo_ref, br_ref, o_ref):
    xn = _rms(x_ref[...], g_ref[...])
    hi = xn.astype(BF16)
    lo = (xn - hi.astype(F32)).astype(BF16)
    w_hi = whi_ref[...]
    logits = _dot(hi, w_hi) + _dot(lo, w_hi) + _dot(hi, wlo_ref[...]) + br_ref[...]
    o_ref[...] = _route(logits)


def _router_riders_kernel(x_ref, g_ref, whi_ref, wlo_ref, br_ref, *rest, n_riders):
    rider_src, o_ref, rider_dst = rest[:n_riders], rest[n_riders], rest[n_riders + 1:]
    _round_riders(rider_src, rider_dst)
    _router_kernel(x_ref, g_ref, whi_ref, wlo_ref, br_ref, o_ref)


def _router_specs(l):
    return [
        pl.BlockSpec((TM_ROUTE, D_MODEL), lambda i: (i, 0)),
        _layer_block((1, D_MODEL), l),
        _layer_block((D_MODEL, LANES), l),
        _layer_block((D_MODEL, LANES), l),
        _layer_block((1, LANES), l),
    ]


def _router(x, l, pw):
    n_rows = x.shape[0]
    return pl.pallas_call(
        _router_kernel,
        grid=(n_rows // TM_ROUTE,),
        in_specs=_router_specs(l),
        out_specs=pl.BlockSpec((TM_ROUTE, LANES), lambda i: (i, 0)),
        out_shape=jax.ShapeDtypeStruct((n_rows, LANES), F32),
        compiler_params=_params(("arbitrary",)),
        name="router",
    )(x, pw["gmoe"], pw["wr_hi"], pw["wr_lo"], pw["br"])


def _router_prompt(x, l, pw, riders):
    n_steps = N_PROMPT // TM_ROUTE
    r_in, r_out, r_shape = _rider_specs(riders, l, n_steps, lambda i: i)
    return pl.pallas_call(
        functools.partial(_router_riders_kernel, n_riders=len(riders)),
        grid=(n_steps,),
        in_specs=_router_specs(l) + r_in,
        out_specs=[pl.BlockSpec((TM_ROUTE, LANES), lambda i: (i, 0))] + r_out,
        out_shape=[jax.ShapeDtypeStruct((N_PROMPT, LANES), F32)] + r_shape,
        compiler_params=_params(("arbitrary",)),
        name="router_prompt",
    )(x, pw["gmoe"], pw["wr_hi"], pw["wr_lo"], pw["br"], *riders)


def _dispatch_tables(rinfo):
    i32 = jnp.int32
    bucket = rinfo[:, BUCKET_LANE].astype(i32)
    bucket_ids = jnp.arange(N_BUCKETS, dtype=i32)
    onehot = (bucket[:, None] == bucket_ids[None, :]).astype(i32)
    csum = jnp.cumsum(onehot, axis=0)
    counts = csum[-1]
    tiles_per = (counts + TG - 1) // TG
    tile_end = jnp.cumsum(tiles_per)
    tile_start = tile_end - tiles_per
    slot = jnp.sum(onehot * (tile_start[None, :] * TG + csum - 1), axis=1)

    tok1 = jnp.arange(1, N_TOK + 1, dtype=i32).astype(F32)
    payload = jnp.concatenate([tok1[:, None], rinfo[:, :EXPERTS_PER_GROUP]], axis=1)
    slots = jnp.zeros((N_SLOTS, 1 + EXPERTS_PER_GROUP), F32).at[slot].set(
        payload, unique_indices=True)
    slot_tok = slots[:, 0].astype(i32) - 1

    tile = jnp.arange(N_TILES, dtype=i32)
    n_valid = tile_end[-1]
    tq = jnp.minimum(tile, n_valid - 1)
    tile_bucket = jnp.sum((tile_end[None, :] <= tq[:, None]).astype(i32), axis=1)
    tile_oh = (tile_bucket[:, None] == bucket_ids[None, :]).astype(i32)
    in_bucket = tile - jnp.sum(tile_oh * tile_start[None, :], axis=1)
    tile_cnt = jnp.clip(jnp.sum(tile_oh * counts[None, :], axis=1) - in_bucket * TG, 0, TG)
    tile_cnt = jnp.where(tile < n_valid, tile_cnt, 0)
    pair_a = jnp.asarray([(b // N_PAIRS) * EXPERTS_PER_GROUP + PAIR_A[b % N_PAIRS]
                          for b in range(N_BUCKETS)], i32)
    pair_b = jnp.asarray([(b // N_PAIRS) * EXPERTS_PER_GROUP + PAIR_B[b % N_PAIRS]
                          for b in range(N_BUCKETS)], i32)
    tile_ea = jnp.sum(tile_oh * pair_a[None, :], axis=1)
    tile_eb = jnp.sum(tile_oh * pair_b[None, :], axis=1)
    is_prompt = (slot_tok >= 0) & (slot_tok < N_PROMPT)
    tile_np = jnp.sum(is_prompt.reshape(N_TILES, TG).astype(i32), axis=1)
    return tile_ea, tile_eb, tile_cnt, tile_np, slot_tok.reshape(N_TILES, 1, TG), slots


def _experts_kernel(ea_ref, eb_ref, cnt_ref, np_ref,
                    xp_hbm, xs_hbm, tokp_ref, tokc_ref, tokn_ref, gates_ref, g_ref,
                    wga_ref, wua_ref, wda_ref, wgb_ref, wub_ref, wdb_ref, gfin_ref,
                    op_hbm, os_hbm, xbuf, obuf, gsem, ssem, *, final_norm, sample_batch_major):
    t = pl.program_id(0)
    nt = pl.num_programs(0)
    buf = lax.rem(t, 2)
    t_prev = jnp.maximum(t - 1, 0)
    t_next = jnp.minimum(t + 1, nt - 1)
    cnt = cnt_ref[t]
    has_next = (t + 1 < nt) & (cnt_ref[t_next] > 0)

    def sample_dst(r):
        if not sample_batch_major:
            return r
        return (r % DEC_BATCH) * DEC_SEQ + r // DEC_BATCH

    def row_slot(r):
        return lax.shift_right_logical(r, 3), lax.bitwise_and(r, SUBLANES - 1)

    def gather_p(tok_ref, b, r, g, s):
        return pltpu.make_async_copy(
            xp_hbm.at[pl.ds(tok_ref[0, r], 1)], xbuf.at[b, g, pl.ds(s, 1)], gsem.at[b])

    def gather_s(tok_ref, b, r, g, s):
        return pltpu.make_async_copy(
            xs_hbm.at[pl.ds(tok_ref[0, r] - N_PROMPT, 1)], xbuf.at[b, g, pl.ds(s, 1)], gsem.at[b])

    def scatter_p(tok_ref, b, r, g, s):
        return pltpu.make_async_copy(
            obuf.at[b, g, pl.ds(s, 1)], op_hbm.at[pl.ds(tok_ref[0, r], 1)], ssem.at[b])

    def scatter_s(tok_ref, b, r, g, s):
        dst = sample_dst(tok_ref[0, r] - N_PROMPT)
        return pltpu.make_async_copy(
            obuf.at[b, g, pl.ds(s, 1)], os_hbm.at[pl.ds(dst, 1)], ssem.at[b])

    def for_range(lo, hi, make_copy, op):
        full = (hi - lo) // SUBLANES
        aligned = isinstance(lo, int) and lo == 0

        def group(i, c):
            for j in range(SUBLANES):
                r = lo + i * SUBLANES + j
                op(make_copy(r, i, j) if aligned else make_copy(r, *row_slot(r)))
            return c

        def single(r, c):
            op(make_copy(r, *row_slot(r)))
            return c
        lax.fori_loop(0, full, group, 0)
        lax.fori_loop(lo + full * SUBLANES, hi, single, 0)

    def gather(tile, tok_ref, b, op):
        for_range(0, np_ref[tile], functools.partial(gather_p, tok_ref, b), op)
        for_range(np_ref[tile], cnt_ref[tile], functools.partial(gather_s, tok_ref, b), op)

    def scatter(tile, tok_ref, b, op):
        for_range(0, np_ref[tile], functools.partial(scatter_p, tok_ref, b), op)
        for_range(np_ref[tile], cnt_ref[tile], functools.partial(scatter_s, tok_ref, b), op)

    def start(copy):
        copy.start()

    def wait(copy):
        copy.wait()

    @pl.when(t == 0)
    def _():
        xbuf[...] = jnp.zeros_like(xbuf)
        gather(t, tokc_ref, 0, start)

    @pl.when(has_next)
    def _():
        gather(t_next, tokn_ref, 1 - buf, start)

    @pl.when(cnt > 0)
    def _():
        gather(t, tokc_ref, buf, wait)

        ja = ea_ref[t] % EXPERTS_PER_GROUP + SLOT_GATE_LANE
        jb = eb_ref[t] % EXPERTS_PER_GROUP + SLOT_GATE_LANE
        n_chunks = (cnt + ROW_CHUNK - 1) // ROW_CHUNK

        def compute(m):
            x = xbuf[buf, 0:m // SUBLANES].reshape(m, D_MODEL)
            xn = _rms(x, g_ref[...]).astype(BF16)
            gates = gates_ref[0:m, :]
            lane = lax.broadcasted_iota(jnp.int32, gates.shape, 1)
            ga = jnp.sum(jnp.where(lane == ja, gates, 0.0), axis=-1, keepdims=True)
            gb = jnp.sum(jnp.where(lane == jb, gates, 0.0), axis=-1, keepdims=True)
            act_a = (jax.nn.silu(_dot(xn, wga_ref[...])) * _dot(xn, wua_ref[...]) * ga).astype(BF16)
            act_b = (jax.nn.silu(_dot(xn, wgb_ref[...])) * _dot(xn, wub_ref[...]) * gb).astype(BF16)
            y = x + (_dot(act_a, wda_ref[...]) + _dot(act_b, wdb_ref[...]))
            if final_norm:
                y = _rms(y, gfin_ref[...])
            obuf[buf, 0:m // SUBLANES] = y.reshape(m // SUBLANES, SUBLANES, D_MODEL)

        for k in range(1, TG // ROW_CHUNK + 1):
            pl.when(n_chunks == k)(functools.partial(compute, k * ROW_CHUNK))

        @pl.when(t > 0)
        def _():
            scatter(t_prev, tokp_ref, 1 - buf, wait)

        scatter(t, tokc_ref, buf, start)

        @pl.when(jnp.logical_not(has_next))
        def _():
            scatter(t, tokc_ref, buf, wait)


def _experts(xp, xs, l, pw, expert_w, gfin, tables, final_norm):
    tile_ea, tile_eb, tile_cnt, tile_np, slot_tok, gates = tables
    w_gate, w_up, w_down = expert_w

    def tok_spec(shift):
        return pl.BlockSpec(
            (None, 1, TG),
            lambda t, *_: (jnp.clip(t + shift, 0, N_TILES - 1), 0, 0),
            memory_space=pltpu.SMEM)

    def w_spec(shape, which):
        return pl.BlockSpec(
            (None,) + shape,
            lambda t, ea, eb, *_: ((ea, eb)[which][t], 0, 0))

    any_spec = pl.BlockSpec(memory_space=pl.ANY)
    grid_spec = pltpu.PrefetchScalarGridSpec(
        num_scalar_prefetch=4,
        grid=(N_TILES,),
        in_specs=[
            any_spec, any_spec,
            tok_spec(-1), tok_spec(0), tok_spec(1),
            pl.BlockSpec((TG, SLOT_GATE_LANE + EXPERTS_PER_GROUP), lambda t, *_: (t, 0)),
            pl.BlockSpec((None, 1, D_MODEL), lambda t, *_: (l, 0, 0)),
            w_spec((D_MODEL, D_FF), 0), w_spec((D_MODEL, D_FF), 0), w_spec((D_FF, D_MODEL), 0),
            w_spec((D_MODEL, D_FF), 1), w_spec((D_MODEL, D_FF), 1), w_spec((D_FF, D_MODEL), 1),
            pl.BlockSpec((1, D_MODEL), lambda t, *_: (0, 0)),
        ],
        out_specs=[any_spec, any_spec],
        scratch_shapes=[
            pltpu.VMEM((2, TG // SUBLANES, SUBLANES, D_MODEL), F32),
            pltpu.VMEM((2, TG // SUBLANES, SUBLANES, D_MODEL), F32),
            pltpu.SemaphoreType.DMA((2,)),
            pltpu.SemaphoreType.DMA((2,)),
        ],
    )
    return pl.pallas_call(
        functools.partial(_experts_kernel, final_norm=final_norm, sample_batch_major=final_norm),
        grid_spec=grid_spec,
        out_shape=[jax.ShapeDtypeStruct((N_PROMPT, D_MODEL), F32),
                   jax.ShapeDtypeStruct((N_SAMPLE, D_MODEL), F32)],
        compiler_params=_params(("arbitrary",)),
        name="experts",
    )(tile_ea, tile_eb, tile_cnt, tile_np, xp, xs, slot_tok, slot_tok, slot_tok, gates,
      pw["gmoe"], w_gate, w_up, w_down, w_gate, w_up, w_down, gfin)


def _block_diag_gates(w_a, w_x):
    def diag(w):
        w = w.reshape(DEPTH, N_GATE_BLOCKS, HEADS_PER_GATE_BLOCK, RG_HEAD_DIM, RG_HEAD_DIM)
        eye = jnp.eye(HEADS_PER_GATE_BLOCK, dtype=w.dtype)
        full = jnp.einsum("lqhij,hk->lqhikj", w, eye)
        return full.reshape(DEPTH, N_GATE_BLOCKS, MXU_DIM, MXU_DIM)
    return jnp.concatenate([diag(w_a), diag(w_x)], axis=-1).astype(BF16)


def _stack_rows(rows, n):
    width = rows[0].shape[-1]
    rows = [r.reshape(DEPTH, -1, width) for r in rows]
    have = sum(r.shape[1] for r in rows)
    if have < n:
        rows.append(jnp.zeros((DEPTH, n - have, width), F32))
    return jnp.concatenate(rows, axis=1)


def kernel(x_prompt, x_sample, state_rglru_h, state_rglru_conv, state_sconv, cache_mem_k,
           cache_mem_v, mem_prompt, norm_mix, w_in, rg_conv_w, rg_conv_b, rg_w_a, rg_b_a,
           rg_w_x, rg_b_x, rg_lambda, sc_conv_w, norm_rg_out, norm_sc_out, w_out, norm_xattn,
           norm_mem, xa_w_q, xa_w_k, xa_w_v, xa_w_o, norm_moe, router_group_w, router_group_b,
           router_expert_w, router_expert_b, expert_w_gate, expert_w_up, expert_w_down,
           norm_final):
    wr = jnp.concatenate([router_group_w, router_expert_w], axis=2)
    wr = jnp.pad(wr, ((0, 0), (0, 0), (0, LANES - wr.shape[2])))
    wr_hi = wr.astype(BF16)
    wr_lo = (wr - wr_hi.astype(F32)).astype(BF16)
    br = jnp.concatenate([router_group_b, router_expert_b], axis=1)
    br = jnp.pad(br, ((0, 0), (0, LANES - br.shape[1]))).reshape(DEPTH, 1, LANES)
    pw = dict(
        gmix=norm_mix.reshape(DEPTH, 1, D_MODEL),
        wg=_block_diag_gates(rg_w_a, rg_w_x),
        rgv=_stack_rows([rg_conv_w, rg_conv_b, rg_b_a, rg_b_x, rg_lambda], 8),
        scv=_stack_rows([sc_conv_w, norm_rg_out, norm_sc_out], 8),
        gxa=norm_xattn.reshape(DEPTH, 1, D_MODEL),
        gmem=norm_mem.reshape(DEPTH, 1, D_MODEL),
        gmoe=norm_moe.reshape(DEPTH, 1, D_MODEL),
        wr_hi=wr_hi, wr_lo=wr_lo, br=br,
    )
    gfin = norm_final.reshape(1, D_MODEL)

    conv_in = jnp.transpose(state_rglru_conv, (0, 2, 1, 3))
    sc_in = state_sconv.reshape(DEPTH, DEC_BATCH, (SC_CONV_W - 1) * D_SC)
    w_gate_rows = expert_w_gate.reshape(DEPTH, N_EXPERTS * D_MODEL, D_FF)
    w_up_rows = expert_w_up.reshape(DEPTH, N_EXPERTS * D_MODEL, D_FF)
    w_down_rows = expert_w_down.reshape(DEPTH, N_EXPERTS * D_FF, D_MODEL)
    w_in_l = w_in[0].astype(BF16)
    cache_k = _lane_pieces(cache_mem_k)
    cache_v = _lane_pieces(cache_mem_v)

    mem = mem_prompt.reshape(BATCH * N_MEM, D_MODEL)
    p_k, p_v, w_out_bf16 = _mem_kv(mem, pw["gmem"], xa_w_k, xa_w_v,
                                   [w_out.reshape(1, DEPTH * D_MODEL, D_MODEL)])
    pw["w_out"] = w_out_bf16.reshape(DEPTH, D_MODEL, D_MODEL)

    xp = x_prompt.reshape(N_PROMPT, D_MODEL)
    xs = jnp.transpose(x_sample, (1, 0, 2)).reshape(N_SAMPLE, D_MODEL)

    p_conv, p_h, p_sc, s_conv, s_h, s_sc = [], [], [], [], [], []
    for l in range(DEPTH):
        xp, c, hh, sc, w_down_l, wq_l, wo_l = _mixer_prompt(
            xp, l, pw, w_in_l, [w_down_rows, xa_w_q, xa_w_o])
        xs, cs, hs, scs = _mixer_sample(xs, l, pw, w_in_l, conv_in, state_rglru_h, sc_in)
        p_conv.append(c)
        p_h.append(hh.reshape(BATCH, D_RG))
        p_sc.append(sc)
        s_conv.append(cs)
        s_h.append(hs)
        s_sc.append(scs.reshape(DEC_BATCH, SC_CONV_W - 1, D_SC))

        next_w_in = [w_in] if l + 1 < DEPTH else []
        xp, *w_in_next = _attn_prompt(xp, l, pw, wq_l, wo_l, p_k, p_v, next_w_in, l + 1)
        q_s = _q_sample(xs, l, pw, wq_l).reshape(DEC_SEQ, DEC_BATCH, D_MODEL)
        o_s = _attn_sample(q_s, l, cache_k, cache_v)
        xs = _oproj_sample(o_s.reshape(N_SAMPLE, D_MODEL), wo_l, xs)
        if w_in_next:
            w_in_l = w_in_next[0]

        rinfo_p, w_gate_l, w_up_l = _router_prompt(xp, l, pw, [w_gate_rows, w_up_rows])
        rinfo = jnp.concatenate([rinfo_p, _router(xs, l, pw)], axis=0)
        expert_w = (w_gate_l.reshape(N_EXPERTS, D_MODEL, D_FF),
                    w_up_l.reshape(N_EXPERTS, D_MODEL, D_FF),
                    w_down_l.reshape(N_EXPERTS, D_FF, D_MODEL))
        xp, xs = _experts(xp, xs, l, pw, expert_w, gfin, _dispatch_tables(rinfo),
                          final_norm=(l == DEPTH - 1))

    y_prompt = xp.reshape(BATCH, SEQ, D_MODEL)
    y_sample = xs.reshape(DEC_BATCH, DEC_SEQ, D_MODEL)
    mem_shape = (DEPTH, BATCH, N_MEM, XA_HEADS, XA_HEAD_DIM)
    return (y_prompt, y_sample,
            jnp.stack(p_h), jnp.stack(p_conv), jnp.stack(p_sc),
            p_k.reshape(mem_shape), p_v.reshape(mem_shape),
            jnp.stack(s_h), jnp.transpose(jnp.stack(s_conv), (0, 2, 1, 3)), jnp.stack(s_sc))
```

```python
import functools
import math

import jax
import jax.numpy as jnp
from jax import lax
from jax.experimental import pallas as pl
from jax.experimental.pallas import tpu as pltpu

D_MODEL = 2048
BATCH = 4
SEQ = 2048
DEPTH = 2
DEC_BATCH = 128
DEC_SEQ = 4
D_RG = 1024
D_SC = 1024
RG_HEADS = 16
RG_HEAD_DIM = 64
RG_CONV_W = 4
RG_C = 8.0
SC_CONV_W = 3
D_IN = 2 * D_RG + 3 * D_SC
N_MEM = 256
XA_HEADS = 4
XA_HEAD_DIM = 512
N_GROUPS = 4
EXPERTS_PER_GROUP = 4
N_EXPERTS = 16
D_FF = 512
EPS = 1e-6

N_PROMPT = BATCH * SEQ
N_SAMPLE = DEC_BATCH * DEC_SEQ
N_TOK = N_PROMPT + N_SAMPLE

V7X_VMEM_LIMIT_BYTES = 56 * 1024 * 1024
SUBLANES = 8
LANES = 128
MXU_DIM = 256

HEADS_PER_GATE_BLOCK = MXU_DIM // RG_HEAD_DIM
N_GATE_BLOCKS = D_RG // MXU_DIM

TM_MIX = 256
TM_ATT = 512
TM_ROUTE = 512
ATT_BB = 4
ATT_QB = 8
HD_CHUNKS = XA_HEAD_DIM // LANES

PAIR_A = (0, 0, 0, 1, 2, 2)
PAIR_B = (1, 2, 3, 3, 3, 1)
N_PAIRS = len(PAIR_A)
N_BUCKETS = N_GROUPS * N_PAIRS
TG = 512
ROW_CHUNK = 128
N_TILES = -(-(N_TOK + N_BUCKETS * (TG - 1)) // TG)
N_SLOTS = N_TILES * TG
BUCKET_LANE = EXPERTS_PER_GROUP
SLOT_GATE_LANE = 1

BF16 = jnp.bfloat16
F32 = jnp.float32


def _params(sem, vmem=V7X_VMEM_LIMIT_BYTES):
    return pltpu.CompilerParams(dimension_semantics=sem, vmem_limit_bytes=vmem)


def _layer_resident(shape, l):
    nd = len(shape)
    return pl.BlockSpec((None,) + shape, lambda *_: (l,) + (0,) * nd, pipeline_mode=pl.Buffered(1))


def _layer_block(shape, l):
    nd = len(shape)
    return pl.BlockSpec((None,) + shape, lambda *_: (l,) + (0,) * nd)


def _whole_resident(shape):
    nd = len(shape)
    return pl.BlockSpec(shape, lambda *_: (0,) * nd, pipeline_mode=pl.Buffered(1))


def _rider_specs(srcs, l, n_steps, step_of):
    in_specs, out_specs, out_shapes = [], [], []
    for a in srcs:
        _, r, c = a.shape
        rows = r // n_steps
        assert rows * n_steps == r and rows % 16 == 0
        in_specs.append(pl.BlockSpec((None, rows, c), lambda *g: (l, step_of(*g), 0)))
        out_specs.append(pl.BlockSpec((rows, c), lambda *g: (step_of(*g), 0)))
        out_shapes.append(jax.ShapeDtypeStruct((r, c), BF16))
    return in_specs, out_specs, out_shapes


def _round_riders(src_refs, dst_refs):
    for src, dst in zip(src_refs, dst_refs):
        dst[...] = src[...].astype(BF16)


def _rms(x, g):
    return x * lax.rsqrt(jnp.mean(x * x, axis=-1, keepdims=True) + EPS) * g


def _dot(a, b):
    return jnp.dot(a, b, preferred_element_type=F32)


def _rg_gate_block(xc_q, wg_ref, rgv_ref, q):
    sl = slice(q * MXU_DIM, (q + 1) * MXU_DIM)
    g = _dot(xc_q.astype(BF16), wg_ref[q])
    r = jax.nn.sigmoid(g[:, :MXU_DIM] + rgv_ref[5:6, sl])
    i = jax.nn.sigmoid(g[:, MXU_DIM:] + rgv_ref[6:7, sl])
    log_a = -RG_C * r * jax.nn.softplus(-rgv_ref[7:8, sl])
    a = jnp.exp(log_a)
    mult = jnp.sqrt(1.0 - a * a)
    return a, mult * (i * xc_q)


def _rg_gate_inputs(xc, wg_ref, rgv_ref):
    parts = [_rg_gate_block(xc[:, q * MXU_DIM:(q + 1) * MXU_DIM], wg_ref, rgv_ref, q)
             for q in range(N_GATE_BLOCKS)]
    return (jnp.concatenate([p[0] for p in parts], axis=1),
            jnp.concatenate([p[1] for p in parts], axis=1))


def _mix_out(hs, rg_gate, sc_b, uc, scv_ref, w_out_ref):
    rg_out = _rms(hs * jax.nn.gelu(rg_gate), scv_ref[3:4, :]).astype(BF16)
    sc_out = _rms(sc_b * uc, scv_ref[4:5, :]).astype(BF16)
    return _dot(rg_out, w_out_ref[0:D_RG, :]) + _dot(sc_out, w_out_ref[D_RG:, :])


def _shift_rows(x, d, fill):
    m = x.shape[0]
    if d % SUBLANES == 0:
        head = jnp.full((d, x.shape[1]), fill, x.dtype)
        return jnp.concatenate([head, x[:m - d]], axis=0)
    rolled = pltpu.roll(x, d, axis=0)
    row = lax.broadcasted_iota(jnp.int32, x.shape, 0)
    return jnp.where(row >= d, rolled, fill)


def _scan_rows(a, b, h0):
    m = a.shape[0]
    d = 1
    while d < m:
        a_sh = _shift_rows(a, d, 1.0)
        b_sh = _shift_rows(b, d, 0.0)
        b = a * b_sh + b
        a = a * a_sh
        d *= 2
    return a * h0 + b


def _mixer_prompt_kernel(x_ref, gmix_ref, w_in_ref, wg_ref, rgv_ref, scv_ref, w_out_ref, *rest,
                         n_riders):
    rider_src, rest = rest[:n_riders], rest[n_riders:]
    o_ref, conv_ref, h_ref, sc_ref = rest[:4]
    rider_dst, (rgx, usc, hcar) = rest[4:4 + n_riders], rest[4 + n_riders:]
    _round_riders(rider_src, rider_dst)
    t = pl.program_id(1)
    tm = x_ref.shape[0]

    @pl.when(t == 0)
    def _():
        rgx[0:SUBLANES, :] = jnp.zeros((SUBLANES, D_RG), F32)
        usc[0:SUBLANES, :] = jnp.zeros((SUBLANES, D_SC), F32)
        hcar[...] = jnp.zeros_like(hcar)

    x = x_ref[...]
    xn = _rms(x, gmix_ref[...]).astype(BF16)

    rg_x = _dot(xn, w_in_ref[:, 0:D_RG])
    rgx[SUBLANES:SUBLANES + tm, :] = rg_x

    w_cols = (slice(2 * D_RG + D_SC, 2 * D_RG + 2 * D_SC), slice(2 * D_RG + 2 * D_SC, D_IN),
              slice(2 * D_RG, 2 * D_RG + D_SC), slice(D_RG, 2 * D_RG))
    hs_parts, proj = [], []
    for q in range(N_GATE_BLOCKS):
        sl = slice(q * MXU_DIM, (q + 1) * MXU_DIM)
        xc = rgx[5:5 + tm, sl] * rgv_ref[0:1, sl]
        xc = xc + rgx[6:6 + tm, sl] * rgv_ref[1:2, sl]
        xc = xc + rgx[7:7 + tm, sl] * rgv_ref[2:3, sl]
        xc = xc + rg_x[:, sl] * rgv_ref[3:4, sl]
        xc = xc + rgv_ref[4:5, sl]
        a, b = _rg_gate_block(xc, wg_ref, rgv_ref, q)
        hs_parts.append(_scan_rows(a, b, hcar[0:1, sl]))
        proj.append(_dot(xn, w_in_ref[:, w_cols[q]]))
        if q == 1:
            sc_c, sc_x = proj
            u = sc_c * sc_x
            usc[SUBLANES:SUBLANES + tm, :] = u
            uc = usc[6:6 + tm, :] * scv_ref[0:1, :]
            uc = uc + usc[7:7 + tm, :] * scv_ref[1:2, :]
            uc = uc + u * scv_ref[2:3, :]
    hs = jnp.concatenate(hs_parts, axis=1)
    hcar[0:1, :] = hs[tm - 1:tm, :]
    sc_b, rg_gate = proj[2], proj[3]

    o_ref[...] = x + _mix_out(hs, rg_gate, sc_b, uc, scv_ref, w_out_ref)

    @pl.when(t == pl.num_programs(1) - 1)
    def _():
        conv_ref[0] = rgx[tm + 5:tm + 8, :]
        sc_ref[0] = usc[tm + 6:tm + 8, :]
        h_ref[0] = hs[tm - 1:tm, :]

    rgx[0:SUBLANES, :] = rgx[tm:tm + SUBLANES, :]
    usc[0:SUBLANES, :] = usc[tm:tm + SUBLANES, :]


def _mixer_weight_specs(l):
    return [
        _layer_resident((1, D_MODEL), l),
        _whole_resident((D_MODEL, D_IN)),
        _layer_resident((N_GATE_BLOCKS, MXU_DIM, 2 * MXU_DIM), l),
        _layer_resident((8, D_RG), l),
        _layer_resident((8, D_SC), l),
        _layer_resident((D_MODEL, D_MODEL), l),
    ]


def _mixer_prompt(x, l, pw, w_in, riders):
    nt = SEQ // TM_MIX
    r_in, r_out, r_shape = _rider_specs(riders, l, BATCH * nt, lambda b, t: b * nt + t)
    return pl.pallas_call(
        functools.partial(_mixer_prompt_kernel, n_riders=len(riders)),
        grid=(BATCH, nt),
        in_specs=[pl.BlockSpec((TM_MIX, D_MODEL), lambda b, t: (b * nt + t, 0))]
        + _mixer_weight_specs(l) + r_in,
        out_specs=[
            pl.BlockSpec((TM_MIX, D_MODEL), lambda b, t: (b * nt + t, 0)),
            pl.BlockSpec((1, RG_CONV_W - 1, D_RG), lambda b, t: (b, 0, 0)),
            pl.BlockSpec((1, 1, D_RG), lambda b, t: (b, 0, 0)),
            pl.BlockSpec((1, SC_CONV_W - 1, D_SC), lambda b, t: (b, 0, 0)),
        ] + r_out,
        out_shape=[
            jax.ShapeDtypeStruct((N_PROMPT, D_MODEL), F32),
            jax.ShapeDtypeStruct((BATCH, RG_CONV_W - 1, D_RG), F32),
            jax.ShapeDtypeStruct((BATCH, 1, D_RG), F32),
            jax.ShapeDtypeStruct((BATCH, SC_CONV_W - 1, D_SC), F32),
        ] + r_shape,
        scratch_shapes=[
            pltpu.VMEM((SUBLANES + TM_MIX, D_RG), F32),
            pltpu.VMEM((SUBLANES + TM_MIX, D_SC), F32),
            pltpu.VMEM((SUBLANES, D_RG), F32),
        ],
        compiler_params=_params(("arbitrary", "arbitrary")),
        name="mixer_prompt",
    )(x, pw["gmix"], w_in, pw["wg"], pw["rgv"], pw["scv"], pw["w_out"], *riders)


def _mixer_sample_kernel(x_ref, gmix_ref, w_in_ref, wg_ref, rgv_ref, scv_ref, w_out_ref,
                         conv_in_ref, h_in_ref, sc_in_ref,
                         o_ref, conv_ref, h_ref, sc_ref, xcs, hss, ucs):
    nb = DEC_BATCH
    x = x_ref[...]
    xn = _rms(x, gmix_ref[...]).astype(BF16)

    rg_x = _dot(xn, w_in_ref[:, 0:D_RG])
    seq = [conv_in_ref[k] for k in range(RG_CONV_W - 1)]
    seq += [rg_x[t * nb:(t + 1) * nb, :] for t in range(DEC_SEQ)]
    for t in range(DEC_SEQ):
        xc_t = seq[t] * rgv_ref[0:1, :]
        for k in range(1, RG_CONV_W):
            xc_t = xc_t + seq[t + k] * rgv_ref[k:k + 1, :]
        xcs[t * nb:(t + 1) * nb, :] = xc_t + rgv_ref[4:5, :]
    for k in range(RG_CONV_W - 1):
        conv_ref[k] = seq[DEC_SEQ + k]

    a, b = _rg_gate_inputs(xcs[...], wg_ref, rgv_ref)
    h = h_in_ref[...]
    for t in range(DEC_SEQ):
        h = a[t * nb:(t + 1) * nb, :] * h + b[t * nb:(t + 1) * nb, :]
        hss[t * nb:(t + 1) * nb, :] = h
    h_ref[...] = h

    rg_gate = _dot(xn, w_in_ref[:, D_RG:2 * D_RG])
    sc_b = _dot(xn, w_in_ref[:, 2 * D_RG:2 * D_RG + D_SC])
    sc_c = _dot(xn, w_in_ref[:, 2 * D_RG + D_SC:2 * D_RG + 2 * D_SC])
    sc_x = _dot(xn, w_in_ref[:, 2 * D_RG + 2 * D_SC:])
    u = sc_c * sc_x
    useq = [sc_in_ref[:, k * D_SC:(k + 1) * D_SC] for k in range(SC_CONV_W - 1)]
    useq += [u[t * nb:(t + 1) * nb, :] for t in range(DEC_SEQ)]
    for t in range(DEC_SEQ):
        uc_t = useq[t] * scv_ref[0:1, :]
        for k in range(1, SC_CONV_W):
            uc_t = uc_t + useq[t + k] * scv_ref[k:k + 1, :]
        ucs[t * nb:(t + 1) * nb, :] = uc_t
    for k in range(SC_CONV_W - 1):
        sc_ref[:, k * D_SC:(k + 1) * D_SC] = useq[DEC_SEQ + k]

    o_ref[...] = x + _mix_out(hss[...], rg_gate, sc_b, ucs[...], scv_ref, w_out_ref)


def _mixer_sample(x, l, pw, w_in, conv_in, h_in, sc_in):
    conv_shape = (RG_CONV_W - 1, DEC_BATCH, D_RG)
    sc_w = (SC_CONV_W - 1) * D_SC
    return pl.pallas_call(
        _mixer_sample_kernel,
        grid=(1,),
        in_specs=[
            pl.BlockSpec((N_SAMPLE, D_MODEL), lambda i: (0, 0)),
        ] + _mixer_weight_specs(l) + [
            _layer_resident(conv_shape, l),
            _layer_resident((DEC_BATCH, D_RG), l),
            _layer_resident((DEC_BATCH, sc_w), l),
        ],
        out_specs=[
            pl.BlockSpec((N_SAMPLE, D_MODEL), lambda i: (0, 0)),
            pl.BlockSpec(conv_shape, lambda i: (0, 0, 0)),
            pl.BlockSpec((DEC_BATCH, D_RG), lambda i: (0, 0)),
            pl.BlockSpec((DEC_BATCH, sc_w), lambda i: (0, 0)),
        ],
        out_shape=[
            jax.ShapeDtypeStruct((N_SAMPLE, D_MODEL), F32),
            jax.ShapeDtypeStruct(conv_shape, F32),
            jax.ShapeDtypeStruct((DEC_BATCH, D_RG), F32),
            jax.ShapeDtypeStruct((DEC_BATCH, sc_w), F32),
        ],
        scratch_shapes=[
            pltpu.VMEM((N_SAMPLE, D_RG), F32),
            pltpu.VMEM((N_SAMPLE, D_RG), F32),
            pltpu.VMEM((N_SAMPLE, D_SC), F32),
        ],
        compiler_params=_params(("arbitrary",)),
        name="mixer_sample",
    )(x, pw["gmix"], w_in, pw["wg"], pw["rgv"], pw["scv"], pw["w_out"],
      conv_in, h_in, sc_in)


def _mem_kv_kernel(m_ref, g_ref, wk_ref, wv_ref, *rest, n_riders):
    rider_src, rest = rest[:n_riders], rest[n_riders:]
    k_ref, v_ref = rest[:2]
    rider_dst, (mn_s,) = rest[2:2 + n_riders], rest[2 + n_riders:]
    _round_riders(rider_src, rider_dst)

    @pl.when(pl.program_id(1) == 0)
    def _():
        mn_s[...] = _rms(m_ref[...], g_ref[...]).astype(BF16)

    mn = mn_s[...]
    k_ref[...] = _dot(mn, wk_ref[...].astype(BF16))
    v_ref[...] = _dot(mn, wv_ref[...].astype(BF16))


def _mem_kv(mem, gmem, w_k, w_v, riders, tn=256):
    rows = BATCH * N_MEM
    nj = D_MODEL // tn
    w_spec = pl.BlockSpec((None, D_MODEL, tn), lambda l, j: (l, 0, j))
    o_spec = pl.BlockSpec((None, rows, tn), lambda l, j: (l, 0, j))
    shape = jax.ShapeDtypeStruct((DEPTH, rows, D_MODEL), F32)
    r_in, r_out, r_shape = _rider_specs(riders, 0, DEPTH * nj, lambda l, j: l * nj + j)
    return pl.pallas_call(
        functools.partial(_mem_kv_kernel, n_riders=len(riders)),
        grid=(DEPTH, nj),
        in_specs=[
            pl.BlockSpec((rows, D_MODEL), lambda l, j: (0, 0), pipeline_mode=pl.Buffered(1)),
            pl.BlockSpec((None, 1, D_MODEL), lambda l, j: (l, 0, 0)),
            w_spec, w_spec,
        ] + r_in,
        out_specs=[o_spec, o_spec] + r_out,
        out_shape=[shape, shape] + r_shape,
        scratch_shapes=[pltpu.VMEM((rows, D_MODEL), BF16)],
        compiler_params=_params(("arbitrary",) * 2),
        name="mem_kv",
    )(mem, gmem, w_k, w_v, *riders)


def _q_sample_kernel(x_ref, g_ref, w_ref, o_ref):
    xn = _rms(x_ref[...], g_ref[...]).astype(BF16)
    o_ref[...] = _dot(xn, w_ref[...])


def _q_sample(x, l, pw, wq, tn=1024):
    return pl.pallas_call(
        _q_sample_kernel,
        grid=(D_MODEL // tn,),
        in_specs=[
            pl.BlockSpec((N_SAMPLE, D_MODEL), lambda j: (0, 0)),
            _layer_block((1, D_MODEL), l),
            pl.BlockSpec((D_MODEL, tn), lambda j: (0, j)),
        ],
        out_specs=pl.BlockSpec((N_SAMPLE, tn), lambda j: (0, j)),
        out_shape=jax.ShapeDtypeStruct((N_SAMPLE, D_MODEL), F32),
        compiler_params=_params(("arbitrary",)),
        name="q_sample",
    )(x, pw["gxa"], wq)


def _oproj_sample_kernel(a_ref, w_ref, res_ref, o_ref):
    o_ref[...] = res_ref[...] + _dot(a_ref[...].astype(BF16), w_ref[...])


def _oproj_sample(a, wo, res, tn=1024):
    return pl.pallas_call(
        _oproj_sample_kernel,
        grid=(D_MODEL // tn,),
        in_specs=[
            pl.BlockSpec((N_SAMPLE, D_MODEL), lambda j: (0, 0)),
            pl.BlockSpec((D_MODEL, tn), lambda j: (0, j)),
            pl.BlockSpec((N_SAMPLE, tn), lambda j: (0, j)),
        ],
        out_specs=pl.BlockSpec((N_SAMPLE, tn), lambda j: (0, j)),
        out_shape=jax.ShapeDtypeStruct((N_SAMPLE, D_MODEL), F32),
        compiler_params=_params(("arbitrary",)),
        name="oproj_sample",
    )(a, wo, res)


def _softmax_rows(s):
    m = jnp.max(s, axis=-1, keepdims=True)
    e = jnp.exp(s - m)
    return e / jnp.sum(e, axis=-1, keepdims=True)


def _attn_prompt_kernel(x_ref, g_ref, wq_ref, k_ref, v_ref, wo_ref, *rest, n_riders):
    rider_src, o_ref, rider_dst = rest[:n_riders], rest[n_riders], rest[n_riders + 1:]
    _round_riders(rider_src, rider_dst)
    x = x_ref[...]
    xn = _rms(x, g_ref[...]).astype(BF16)
    q = _dot(xn, wq_ref[...]).astype(BF16)
    k = k_ref[...].astype(BF16)
    v = v_ref[...].astype(BF16)
    outs = []
    for h in range(XA_HEADS):
        sl = slice(h * XA_HEAD_DIM, (h + 1) * XA_HEAD_DIM)
        s = lax.dot_general(q[:, sl], k[:, sl], (((1,), (1,)), ((), ())),
                            preferred_element_type=F32) / math.sqrt(XA_HEAD_DIM)
        p = _softmax_rows(s).astype(BF16)
        outs.append(_dot(p, v[:, sl]).astype(BF16))
    o = jnp.concatenate(outs, axis=1)
    o_ref[...] = x + _dot(o, wo_ref[...])


def _attn_prompt(x, l, pw, wq, wo, k, v, riders, rider_layer):
    nt = SEQ // TM_ATT
    kv_spec = pl.BlockSpec((None, N_MEM, D_MODEL), lambda b, t: (l, b, 0))
    r_in, r_out, r_shape = _rider_specs(riders, rider_layer, BATCH * nt, lambda b, t: b * nt + t)
    return pl.pallas_call(
        functools.partial(_attn_prompt_kernel, n_riders=len(riders)),
        grid=(BATCH, nt),
        in_specs=[
            pl.BlockSpec((TM_ATT, D_MODEL), lambda b, t: (b * nt + t, 0)),
            _layer_resident((1, D_MODEL), l),
            _whole_resident((D_MODEL, D_MODEL)),
            kv_spec, kv_spec,
            _whole_resident((D_MODEL, D_MODEL)),
        ] + r_in,
        out_specs=[pl.BlockSpec((TM_ATT, D_MODEL), lambda b, t: (b * nt + t, 0))] + r_out,
        out_shape=[jax.ShapeDtypeStruct((N_PROMPT, D_MODEL), F32)] + r_shape,
        compiler_params=_params(("arbitrary", "arbitrary")),
        name="attn_prompt",
    )(x, pw["gxa"], wq, k, v, wo, *riders)


def _head_matrix(mem_ref, stage_ref, b, h):
    n = N_MEM * HD_CHUNKS
    stage_ref[...] = mem_ref[pl.ds(b * n * XA_HEADS + h, n, stride=XA_HEADS), :]
    chunks = [stage_ref[pl.ds(c, N_MEM, stride=HD_CHUNKS), :] for c in range(HD_CHUNKS)]
    return jnp.concatenate(chunks, axis=1).astype(BF16)


def _attn_sample_kernel(q_ref, k_ref, v_ref, o_ref, stage_ref):
    j = pl.program_id(1)
    rows = DEC_SEQ * ATT_QB
    keys = ATT_BB * N_MEM
    q_all = q_ref[...].reshape(rows, D_MODEL)
    row_b = lax.broadcasted_iota(jnp.int32, (rows, keys), 0) % ATT_QB
    key_b = lax.broadcasted_iota(jnp.int32, (rows, keys), 1) // N_MEM + j * ATT_BB
    own = row_b == key_b
    mine = (lax.broadcasted_iota(jnp.int32, (rows, 1), 0) % ATT_QB) // ATT_BB == j
    outs = []
    for h in range(XA_HEADS):
        q = q_all[:, h * XA_HEAD_DIM:(h + 1) * XA_HEAD_DIM].astype(BF16)
        k = jnp.concatenate([_head_matrix(k_ref, stage_ref, b, h) for b in range(ATT_BB)], axis=0)
        v = jnp.concatenate([_head_matrix(v_ref, stage_ref, b, h) for b in range(ATT_BB)], axis=0)
        s = lax.dot_general(q, k, (((1,), (1,)), ((), ())),
                            preferred_element_type=F32) / math.sqrt(XA_HEAD_DIM)
        s = jnp.where(own, s, -jnp.inf)
        m = jnp.where(mine, jnp.max(s, axis=-1, keepdims=True), 0.0)
        e = jnp.where(own, jnp.exp(s - m), 0.0)
        denom = jnp.where(mine, jnp.sum(e, axis=-1, keepdims=True), 1.0)
        p = (e / denom).astype(BF16)
        outs.append(_dot(p, v))
    o = jnp.concatenate(outs, axis=1).reshape(DEC_SEQ, ATT_QB, D_MODEL)

    @pl.when(j == 0)
    def _():
        o_ref[...] = o

    @pl.when(j > 0)
    def _():
        o_ref[...] += o


def _lane_pieces(cache):
    c = cache.reshape(DEPTH, DEC_BATCH, N_MEM, XA_HEADS, HD_CHUNKS, LANES)
    c = jnp.transpose(c, (0, 1, 2, 4, 3, 5))
    return c.reshape(DEPTH * DEC_BATCH * N_MEM * HD_CHUNKS * XA_HEADS, LANES)


def _attn_sample(q, l, k, v):
    blk = ATT_BB * N_MEM * HD_CHUNKS * XA_HEADS
    parts = ATT_QB // ATT_BB
    nb = DEC_BATCH // ATT_BB
    kv_spec = pl.BlockSpec((blk, LANES), lambda i, j: (l * nb + i * parts + j, 0))
    q_spec = pl.BlockSpec((DEC_SEQ, ATT_QB, D_MODEL), lambda i, j: (0, i, 0))
    return pl.pallas_call(
        _attn_sample_kernel,
        grid=(DEC_BATCH // ATT_QB, parts),
        in_specs=[q_spec, kv_spec, kv_spec],
        out_specs=q_spec,
        out_shape=jax.ShapeDtypeStruct((DEC_SEQ, DEC_BATCH, D_MODEL), F32),
        scratch_shapes=[pltpu.VMEM((N_MEM * HD_CHUNKS, LANES), F32)],
        compiler_params=_params(("arbitrary", "arbitrary")),
        name="attn_sample",
    )(q, k, v)


def _route(logits):
    lane = lax.broadcasted_iota(jnp.int32, logits.shape, 1)
    neg = -jnp.inf
    is_g = lane < N_GROUPS
    lg = jnp.where(is_g, logits, neg)
    mg = jnp.max(lg, axis=-1, keepdims=True)
    eg = jnp.where(is_g, jnp.exp(lg - mg), 0.0)
    p_g = eg / jnp.sum(eg, axis=-1, keepdims=True)
    pg_sel = jnp.max(p_g, axis=-1, keepdims=True)
    g_sel = jnp.min(jnp.where(p_g == pg_sel, lane, LANES), axis=-1, keepdims=True)

    lo = N_GROUPS + g_sel * EXPERTS_PER_GROUP
    in_grp = (lane >= lo) & (lane < lo + EXPERTS_PER_GROUP)
    le = jnp.where(in_grp, logits, neg)
    me = jnp.max(le, axis=-1, keepdims=True)
    ee = jnp.where(in_grp, jnp.exp(le - me), 0.0)
    p_e = ee / jnp.sum(ee, axis=-1, keepdims=True)

    p1 = jnp.max(p_e, axis=-1, keepdims=True)
    i1 = jnp.min(jnp.where(in_grp & (p_e == p1), lane, LANES), axis=-1, keepdims=True)
    rest = in_grp & (lane != i1)
    p2 = jnp.max(jnp.where(rest, p_e, neg), axis=-1, keepdims=True)
    i2 = jnp.min(jnp.where(rest & (p_e == p2), lane, LANES), axis=-1, keepdims=True)
    tot = p1 + p2
    w1 = p1 / tot * pg_sel
    w2 = p2 / tot * pg_sel
    gates = jnp.where(lane == i1, w1, 0.0) + jnp.where(lane == i2, w2, 0.0)

    local = jnp.zeros_like(gates)
    for g in range(N_GROUPS):
        start = N_GROUPS + g * EXPERTS_PER_GROUP
        local = local + jnp.where(lane < EXPERTS_PER_GROUP,
                                  pltpu.roll(gates, LANES - start, axis=1), 0.0)

    ja = jnp.minimum(i1, i2) - lo
    jb = jnp.maximum(i1, i2) - lo
    code = ja * EXPERTS_PER_GROUP + jb
    pair = jnp.full_like(code, N_PAIRS - 1)
    for p in range(N_PAIRS - 1):
        lo_j, hi_j = min(PAIR_A[p], PAIR_B[p]), max(PAIR_A[p], PAIR_B[p])
        pair = jnp.where(code == lo_j * EXPERTS_PER_GROUP + hi_j, p, pair)
    bucket = (g_sel * N_PAIRS + pair).astype(F32)
    return local + jnp.where(lane == BUCKET_LANE, bucket, 0.0)


def _router_kernel(x_ref, g_ref, whi_ref, wlo_ref, br_ref, o_ref):
    xn = _rms(x_ref[...], g_ref[...])
    hi = xn.astype(BF16)
    lo = (xn - hi.astype(F32)).astype(BF16)
    w_hi = whi_ref[...]
    logits = _dot(hi, w_hi) + _dot(lo, w_hi) + _dot(hi, wlo_ref[...]) + br_ref[...]
    o_ref[...] = _route(logits)


def _router_riders_kernel(x_ref, g_ref, whi_ref, wlo_ref, br_ref, *rest, n_riders):
    rider_src, o_ref, rider_dst = rest[:n_riders], rest[n_riders], rest[n_riders + 1:]
    _round_riders(rider_src, rider_dst)
    _router_kernel(x_ref, g_ref, whi_ref, wlo_ref, br_ref, o_ref)


def _router_specs(l):
    return [
        pl.BlockSpec((TM_ROUTE, D_MODEL), lambda i: (i, 0)),
        _layer_block((1, D_MODEL), l),
        _layer_block((D_MODEL, LANES), l),
        _layer_block((D_MODEL, LANES), l),
        _layer_block((1, LANES), l),
    ]


def _router(x, l, pw):
    n_rows = x.shape[0]
    return pl.pallas_call(
        _router_kernel,
        grid=(n_rows // TM_ROUTE,),
        in_specs=_router_specs(l),
        out_specs=pl.BlockSpec((TM_ROUTE, LANES), lambda i: (i, 0)),
        out_shape=jax.ShapeDtypeStruct((n_rows, LANES), F32),
        compiler_params=_params(("arbitrary",)),
        name="router",
    )(x, pw["gmoe"], pw["wr_hi"], pw["wr_lo"], pw["br"])


def _router_prompt(x, l, pw, riders):
    n_steps = N_PROMPT // TM_ROUTE
    r_in, r_out, r_shape = _rider_specs(riders, l, n_steps, lambda i: i)
    return pl.pallas_call(
        functools.partial(_router_riders_kernel, n_riders=len(riders)),
        grid=(n_steps,),
        in_specs=_router_specs(l) + r_in,
        out_specs=[pl.BlockSpec((TM_ROUTE, LANES), lambda i: (i, 0))] + r_out,
        out_shape=[jax.ShapeDtypeStruct((N_PROMPT, LANES), F32)] + r_shape,
        compiler_params=_params(("arbitrary",)),
        name="router_prompt",
    )(x, pw["gmoe"], pw["wr_hi"], pw["wr_lo"], pw["br"], *riders)


def _dispatch_tables(rinfo):
    i32 = jnp.int32
    bucket = rinfo[:, BUCKET_LANE].astype(i32)
    bucket_ids = jnp.arange(N_BUCKETS, dtype=i32)
    onehot = (bucket[:, None] == bucket_ids[None, :]).astype(i32)
    csum = jnp.cumsum(onehot, axis=0)
    counts = csum[-1]
    tiles_per = (counts + TG - 1) // TG
    tile_end = jnp.cumsum(tiles_per)
    tile_start = tile_end - tiles_per
    slot = jnp.sum(onehot * (tile_start[None, :] * TG + csum - 1), axis=1)

    tok1 = jnp.arange(1, N_TOK + 1, dtype=i32).astype(F32)
    payload = jnp.concatenate([tok1[:, None], rinfo[:, :EXPERTS_PER_GROUP]], axis=1)
    slots = jnp.zeros((N_SLOTS, 1 + EXPERTS_PER_GROUP), F32).at[slot].set(
        payload, unique_indices=True)
    slot_tok = slots[:, 0].astype(i32) - 1

    tile = jnp.arange(N_TILES, dtype=i32)
    n_valid = tile_end[-1]
    tq = jnp.minimum(tile, n_valid - 1)
    tile_bucket = jnp.sum((tile_end[None, :] <= tq[:, None]).astype(i32), axis=1)
    tile_oh = (tile_bucket[:, None] == bucket_ids[None, :]).astype(i32)
    in_bucket = tile - jnp.sum(tile_oh * tile_start[None, :], axis=1)
    tile_cnt = jnp.clip(jnp.sum(tile_oh * counts[None, :], axis=1) - in_bucket * TG, 0, TG)
    tile_cnt = jnp.where(tile < n_valid, tile_cnt, 0)
    pair_a = jnp.asarray([(b // N_PAIRS) * EXPERTS_PER_GROUP + PAIR_A[b % N_PAIRS]
                          for b in range(N_BUCKETS)], i32)
    pair_b = jnp.asarray([(b // N_PAIRS) * EXPERTS_PER_GROUP + PAIR_B[b % N_PAIRS]
                          for b in range(N_BUCKETS)], i32)
    tile_ea = jnp.sum(tile_oh * pair_a[None, :], axis=1)
    tile_eb = jnp.sum(tile_oh * pair_b[None, :], axis=1)
    is_prompt = (slot_tok >= 0) & (slot_tok < N_PROMPT)
    tile_np = jnp.sum(is_prompt.reshape(N_TILES, TG).astype(i32), axis=1)
    return tile_ea, tile_eb, tile_cnt, tile_np, slot_tok.reshape(N_TILES, 1, TG), slots


def _experts_kernel(ea_ref, eb_ref, cnt_ref, np_ref,
                    xp_hbm, xs_hbm, tokp_ref, tokc_ref, tokn_ref, gates_ref, g_ref,
                    wga_ref, wua_ref, wda_ref, wgb_ref, wub_ref, wdb_ref, gfin_ref,
                    op_hbm, os_hbm, xbuf, obuf, gsem, ssem, *, final_norm, sample_batch_major):
    t = pl.program_id(0)
    nt = pl.num_programs(0)
    buf = lax.rem(t, 2)
    t_prev = jnp.maximum(t - 1, 0)
    t_next = jnp.minimum(t + 1, nt - 1)
    cnt = cnt_ref[t]
    has_next = (t + 1 < nt) & (cnt_ref[t_next] > 0)

    def sample_dst(r):
        if not sample_batch_major:
            return r
        return (r % DEC_BATCH) * DEC_SEQ + r // DEC_BATCH

    def row_slot(r):
        return lax.shift_right_logical(r, 3), lax.bitwise_and(r, SUBLANES - 1)

    def gather_p(tok_ref, b, r, g, s):
        return pltpu.make_async_copy(
            xp_hbm.at[pl.ds(tok_ref[0, r], 1)], xbuf.at[b, g, pl.ds(s, 1)], gsem.at[b])

    def gather_s(tok_ref, b, r, g, s):
        return pltpu.make_async_copy(
            xs_hbm.at[pl.ds(tok_ref[0, r] - N_PROMPT, 1)], xbuf.at[b, g, pl.ds(s, 1)], gsem.at[b])

    def scatter_p(tok_ref, b, r, g, s):
        return pltpu.make_async_copy(
            obuf.at[b, g, pl.ds(s, 1)], op_hbm.at[pl.ds(tok_ref[0, r], 1)], ssem.at[b])

    def scatter_s(tok_ref, b, r, g, s):
        dst = sample_dst(tok_ref[0, r] - N_PROMPT)
        return pltpu.make_async_copy(
            obuf.at[b, g, pl.ds(s, 1)], os_hbm.at[pl.ds(dst, 1)], ssem.at[b])

    def for_range(lo, hi, make_copy, op):
        full = (hi - lo) // SUBLANES
        aligned = isinstance(lo, int) and lo == 0

        def group(i, c):
            for j in range(SUBLANES):
                r = lo + i * SUBLANES + j
                op(make_copy(r, i, j) if aligned else make_copy(r, *row_slot(r)))
            return c

        def single(r, c):
            op(make_copy(r, *row_slot(r)))
            return c
        lax.fori_loop(0, full, group, 0)
        lax.fori_loop(lo + full * SUBLANES, hi, single, 0)

    def gather(tile, tok_ref, b, op):
        for_range(0, np_ref[tile], functools.partial(gather_p, tok_ref, b), op)
        for_range(np_ref[tile], cnt_ref[tile], functools.partial(gather_s, tok_ref, b), op)

    def scatter(tile, tok_ref, b, op):
        for_range(0, np_ref[tile], functools.partial(scatter_p, tok_ref, b), op)
        for_range(np_ref[tile], cnt_ref[tile], functools.partial(scatter_s, tok_ref, b), op)

    def start(copy):
        copy.start()

    def wait(copy):
        copy.wait()

    @pl.when(t == 0)
    def _():
        xbuf[...] = jnp.zeros_like(xbuf)
        gather(t, tokc_ref, 0, start)

    @pl.when(has_next)
    def _():
        gather(t_next, tokn_ref, 1 - buf, start)

    @pl.when(cnt > 0)
    def _():
        gather(t, tokc_ref, buf, wait)

        ja = ea_ref[t] % EXPERTS_PER_GROUP + SLOT_GATE_LANE
        jb = eb_ref[t] % EXPERTS_PER_GROUP + SLOT_GATE_LANE
        n_chunks = (cnt + ROW_CHUNK - 1) // ROW_CHUNK

        def compute(m):
            x = xbuf[buf, 0:m // SUBLANES].reshape(m, D_MODEL)
            xn = _rms(x, g_ref[...]).astype(BF16)
            gates = gates_ref[0:m, :]
            lane = lax.broadcasted_iota(jnp.int32, gates.shape, 1)
            ga = jnp.sum(jnp.where(lane == ja, gates, 0.0), axis=-1, keepdims=True)
            gb = jnp.sum(jnp.where(lane == jb, gates, 0.0), axis=-1, keepdims=True)
            act_a = (jax.nn.silu(_dot(xn, wga_ref[...])) * _dot(xn, wua_ref[...]) * ga).astype(BF16)
            act_b = (jax.nn.silu(_dot(xn, wgb_ref[...])) * _dot(xn, wub_ref[...]) * gb).astype(BF16)
            y = x + (_dot(act_a, wda_ref[...]) + _dot(act_b, wdb_ref[...]))
            if final_norm:
                y = _rms(y, gfin_ref[...])
            obuf[buf, 0:m // SUBLANES] = y.reshape(m // SUBLANES, SUBLANES, D_MODEL)

        for k in range(1, TG // ROW_CHUNK + 1):
            pl.when(n_chunks == k)(functools.partial(compute, k * ROW_CHUNK))

        @pl.when(t > 0)
        def _():
            scatter(t_prev, tokp_ref, 1 - buf, wait)

        scatter(t, tokc_ref, buf, start)

        @pl.when(jnp.logical_not(has_next))
        def _():
            scatter(t, tokc_ref, buf, wait)


def _experts(xp, xs, l, pw, expert_w, gfin, tables, final_norm):
    tile_ea, tile_eb, tile_cnt, tile_np, slot_tok, gates = tables
    w_gate, w_up, w_down = expert_w

    def tok_spec(shift):
        return pl.BlockSpec(
            (None, 1, TG),
            lambda t, *_: (jnp.clip(t + shift, 0, N_TILES - 1), 0, 0),
            memory_space=pltpu.SMEM)

    def w_spec(shape, which):
        return pl.BlockSpec(
            (None,) + shape,
            lambda t, ea, eb, *_: ((ea, eb)[which][t], 0, 0))

    any_spec = pl.BlockSpec(memory_space=pl.ANY)
    grid_spec = pltpu.PrefetchScalarGridSpec(
        num_scalar_prefetch=4,
        grid=(N_TILES,),
        in_specs=[
            any_spec, any_spec,
            tok_spec(-1), tok_spec(0), tok_spec(1),
            pl.BlockSpec((TG, SLOT_GATE_LANE + EXPERTS_PER_GROUP), lambda t, *_: (t, 0)),
            pl.BlockSpec((None, 1, D_MODEL), lambda t, *_: (l, 0, 0)),
            w_spec((D_MODEL, D_FF), 0), w_spec((D_MODEL, D_FF), 0), w_spec((D_FF, D_MODEL), 0),
            w_spec((D_MODEL, D_FF), 1), w_spec((D_MODEL, D_FF), 1), w_spec((D_FF, D_MODEL), 1),
            pl.BlockSpec((1, D_MODEL), lambda t, *_: (0, 0)),
        ],
        out_specs=[any_spec, any_spec],
        scratch_shapes=[
            pltpu.VMEM((2, TG // SUBLANES, SUBLANES, D_MODEL), F32),
            pltpu.VMEM((2, TG // SUBLANES, SUBLANES, D_MODEL), F32),
            pltpu.SemaphoreType.DMA((2,)),
            pltpu.SemaphoreType.DMA((2,)),
        ],
    )
    return pl.pallas_call(
        functools.partial(_experts_kernel, final_norm=final_norm, sample_batch_major=final_norm),
        grid_spec=grid_spec,
        out_shape=[jax.ShapeDtypeStruct((N_PROMPT, D_MODEL), F32),
                   jax.ShapeDtypeStruct((N_SAMPLE, D_MODEL), F32)],
        compiler_params=_params(("arbitrary",)),
        name="experts",
    )(tile_ea, tile_eb, tile_cnt, tile_np, xp, xs, slot_tok, slot_tok, slot_tok, gates,
      pw["gmoe"], w_gate, w_up, w_down, w_gate, w_up, w_down, gfin)


def _block_diag_gates(w_a, w_x):
    def diag(w):
        w = w.reshape(DEPTH, N_GATE_BLOCKS, HEADS_PER_GATE_BLOCK, RG_HEAD_DIM, RG_HEAD_DIM)
        eye = jnp.eye(HEADS_PER_GATE_BLOCK, dtype=w.dtype)
        full = jnp.einsum("lqhij,hk->lqhikj", w, eye)
        return full.reshape(DEPTH, N_GATE_BLOCKS, MXU_DIM, MXU_DIM)
    return jnp.concatenate([diag(w_a), diag(w_x)], axis=-1).astype(BF16)


def _stack_rows(rows, n):
    width = rows[0].shape[-1]
    rows = [r.reshape(DEPTH, -1, width) for r in rows]
    have = sum(r.shape[1] for r in rows)
    if have < n:
        rows.append(jnp.zeros((DEPTH, n - have, width), F32))
    return jnp.concatenate(rows, axis=1)


def kernel(x_prompt, x_sample, state_rglru_h, state_rglru_conv, state_sconv, cache_mem_k,
           cache_mem_v, mem_prompt, norm_mix, w_in, rg_conv_w, rg_conv_b, rg_w_a, rg_b_a,
           rg_w_x, rg_b_x, rg_lambda, sc_conv_w, norm_rg_out, norm_sc_out, w_out, norm_xattn,
           norm_mem, xa_w_q, xa_w_k, xa_w_v, xa_w_o, norm_moe, router_group_w, router_group_b,
           router_expert_w, router_expert_b, expert_w_gate, expert_w_up, expert_w_down,
           norm_final):
    wr = jnp.concatenate([router_group_w, router_expert_w], axis=2)
    wr = jnp.pad(wr, ((0, 0), (0, 0), (0, LANES - wr.shape[2])))
    wr_hi = wr.astype(BF16)
    wr_lo = (wr - wr_hi.astype(F32)).astype(BF16)
    br = jnp.concatenate([router_group_b, router_expert_b], axis=1)
    br = jnp.pad(br, ((0, 0), (0, LANES - br.shape[1]))).reshape(DEPTH, 1, LANES)
    pw = dict(
        gmix=norm_mix.reshape(DEPTH, 1, D_MODEL),
        wg=_block_diag_gates(rg_w_a, rg_w_x),
        rgv=_stack_rows([rg_conv_w, rg_conv_b, rg_b_a, rg_b_x, rg_lambda], 8),
        scv=_stack_rows([sc_conv_w, norm_rg_out, norm_sc_out], 8),
        gxa=norm_xattn.reshape(DEPTH, 1, D_MODEL),
        gmem=norm_mem.reshape(DEPTH, 1, D_MODEL),
        gmoe=norm_moe.reshape(DEPTH, 1, D_MODEL),
        wr_hi=wr_hi, wr_lo=wr_lo, br=br,
    )
    gfin = norm_final.reshape(1, D_MODEL)

    conv_in = jnp.transpose(state_rglru_conv, (0, 2, 1, 3))
    sc_in = state_sconv.reshape(DEPTH, DEC_BATCH, (SC_CONV_W - 1) * D_SC)
    w_gate_rows = expert_w_gate.reshape(DEPTH, N_EXPERTS * D_MODEL, D_FF)
    w_up_rows = expert_w_up.reshape(DEPTH, N_EXPERTS * D_MODEL, D_FF)
    w_down_rows = expert_w_down.reshape(DEPTH, N_EXPERTS * D_FF, D_MODEL)
    cache_k = _lane_pieces(cache_mem_k)
    cache_v = _lane_pieces(cache_mem_v)

    mem = mem_prompt.reshape(BATCH * N_MEM, D_MODEL)
    p_k, p_v, w_out_bf16, w_in_l = _mem_kv(
        mem, pw["gmem"], xa_w_k, xa_w_v,
        [w_out.reshape(1, DEPTH * D_MODEL, D_MODEL), w_in[:1]])
    pw["w_out"] = w_out_bf16.reshape(DEPTH, D_MODEL, D_MODEL)

    xp = x_prompt.reshape(N_PROMPT, D_MODEL)
    xs = jnp.transpose(x_sample, (1, 0, 2)).reshape(N_SAMPLE, D_MODEL)

    p_conv, p_h, p_sc, s_conv, s_h, s_sc = [], [], [], [], [], []
    for l in range(DEPTH):
        xp, c, hh, sc, w_down_l, wq_l, wo_l = _mixer_prompt(
            xp, l, pw, w_in_l, [w_down_rows, xa_w_q, xa_w_o])
        xs, cs, hs, scs = _mixer_sample(xs, l, pw, w_in_l, conv_in, state_rglru_h, sc_in)
        p_conv.append(c)
        p_h.append(hh.reshape(BATCH, D_RG))
        p_sc.append(sc)
        s_conv.append(cs)
        s_h.append(hs)
        s_sc.append(scs.reshape(DEC_BATCH, SC_CONV_W - 1, D_SC))

        next_w_in = [w_in] if l + 1 < DEPTH else []
        xp, *w_in_next = _attn_prompt(xp, l, pw, wq_l, wo_l, p_k, p_v, next_w_in, l + 1)
        q_s = _q_sample(xs, l, pw, wq_l).reshape(DEC_SEQ, DEC_BATCH, D_MODEL)
        o_s = _attn_sample(q_s, l, cache_k, cache_v)
        xs = _oproj_sample(o_s.reshape(N_SAMPLE, D_MODEL), wo_l, xs)
        if w_in_next:
            w_in_l = w_in_next[0]

        rinfo_p, w_gate_l, w_up_l = _router_prompt(xp, l, pw, [w_gate_rows, w_up_rows])
        rinfo = jnp.concatenate([rinfo_p, _router(xs, l, pw)], axis=0)
        expert_w = (w_gate_l.reshape(N_EXPERTS, D_MODEL, D_FF),
                    w_up_l.reshape(N_EXPERTS, D_MODEL, D_FF),
                    w_down_l.reshape(N_EXPERTS, D_FF, D_MODEL))
        xp, xs = _experts(xp, xs, l, pw, expert_w, gfin, _dispatch_tables(rinfo),
                          final_norm=(l == DEPTH - 1))

    y_prompt = xp.reshape(BATCH, SEQ, D_MODEL)
    y_sample = xs.reshape(DEC_BATCH, DEC_SEQ, D_MODEL)
    mem_shape = (DEPTH, BATCH, N_MEM, XA_HEADS, XA_HEAD_DIM)
    return (y_prompt, y_sample,
            jnp.stack(p_h), jnp.stack(p_conv), jnp.stack(p_sc),
            p_k.reshape(mem_shape), p_v.reshape(mem_shape),
            jnp.stack(s_h), jnp.transpose(jnp.stack(s_conv), (0, 2, 1, 3)), jnp.stack(s_sc))
```

```python
import functools
import math

import jax
import jax.numpy as jnp
from jax import lax
from jax.experimental import pallas as pl
from jax.experimental.pallas import tpu as pltpu

D_MODEL = 2048
BATCH = 4
SEQ = 2048
DEPTH = 2
DEC_BATCH = 128
DEC_SEQ = 4
D_RG = 1024
D_SC = 1024
RG_HEADS = 16
RG_HEAD_DIM = 64
RG_CONV_W = 4
RG_C = 8.0
SC_CONV_W = 3
D_IN = 2 * D_RG + 3 * D_SC
N_MEM = 256
XA_HEADS = 4
XA_HEAD_DIM = 512
N_GROUPS = 4
EXPERTS_PER_GROUP = 4
N_EXPERTS = 16
D_FF = 512
EPS = 1e-6

N_PROMPT = BATCH * SEQ
N_SAMPLE = DEC_BATCH * DEC_SEQ
N_TOK = N_PROMPT + N_SAMPLE

V7X_VMEM_LIMIT_BYTES = 56 * 1024 * 1024
SUBLANES = 8
LANES = 128
MXU_DIM = 256

HEADS_PER_GATE_BLOCK = MXU_DIM // RG_HEAD_DIM
N_GATE_BLOCKS = D_RG // MXU_DIM

TM_MIX = 256
TM_ATT = 512
TM_ROUTE = 512
ATT_BB = 4
HD_CHUNKS = XA_HEAD_DIM // LANES

PAIR_A = (0, 0, 0, 1, 2, 2)
PAIR_B = (1, 2, 3, 3, 3, 1)
N_PAIRS = len(PAIR_A)
N_BUCKETS = N_GROUPS * N_PAIRS
TG = 512
ROW_CHUNK = 64
N_TILES = -(-(N_TOK + N_BUCKETS * (TG - 1)) // TG)
N_SLOTS = N_TILES * TG
BUCKET_LANE = EXPERTS_PER_GROUP
SLOT_GATE_LANE = 1

BF16 = jnp.bfloat16
F32 = jnp.float32


def _params(sem, vmem=V7X_VMEM_LIMIT_BYTES):
    return pltpu.CompilerParams(dimension_semantics=sem, vmem_limit_bytes=vmem)


def _layer_resident(shape, l):
    nd = len(shape)
    return pl.BlockSpec((None,) + shape, lambda *_: (l,) + (0,) * nd, pipeline_mode=pl.Buffered(1))


def _layer_block(shape, l):
    nd = len(shape)
    return pl.BlockSpec((None,) + shape, lambda *_: (l,) + (0,) * nd)


def _whole_resident(shape):
    nd = len(shape)
    return pl.BlockSpec(shape, lambda *_: (0,) * nd, pipeline_mode=pl.Buffered(1))


def _rider_specs(srcs, l, n_steps, step_of):
    in_specs, out_specs, out_shapes = [], [], []
    for a in srcs:
        _, r, c = a.shape
        rows = r // n_steps
        assert rows * n_steps == r and rows % 16 == 0
        in_specs.append(pl.BlockSpec((None, rows, c), lambda *g: (l, step_of(*g), 0)))
        out_specs.append(pl.BlockSpec((rows, c), lambda *g: (step_of(*g), 0)))
        out_shapes.append(jax.ShapeDtypeStruct((r, c), BF16))
    return in_specs, out_specs, out_shapes


def _round_riders(src_refs, dst_refs):
    for src, dst in zip(src_refs, dst_refs):
        dst[...] = src[...].astype(BF16)


def _rms(x, g):
    return x * lax.rsqrt(jnp.mean(x * x, axis=-1, keepdims=True) + EPS) * g


def _dot(a, b):
    return jnp.dot(a, b, preferred_element_type=F32)


def _rg_gate_block(xc_q, wg_ref, rgv_ref, q):
    sl = slice(q * MXU_DIM, (q + 1) * MXU_DIM)
    g = _dot(xc_q.astype(BF16), wg_ref[q])
    r = jax.nn.sigmoid(g[:, :MXU_DIM] + rgv_ref[5:6, sl])
    i = jax.nn.sigmoid(g[:, MXU_DIM:] + rgv_ref[6:7, sl])
    log_a = -RG_C * r * jax.nn.softplus(-rgv_ref[7:8, sl])
    a = jnp.exp(log_a)
    mult = jnp.sqrt(1.0 - a * a)
    return a, mult * (i * xc_q)


def _rg_gate_inputs(xc, wg_ref, rgv_ref):
    parts = [_rg_gate_block(xc[:, q * MXU_DIM:(q + 1) * MXU_DIM], wg_ref, rgv_ref, q)
             for q in range(N_GATE_BLOCKS)]
    return (jnp.concatenate([p[0] for p in parts], axis=1),
            jnp.concatenate([p[1] for p in parts], axis=1))


def _mix_out(hs, rg_gate, sc_b, uc, scv_ref, w_out_ref):
    rg_out = _rms(hs * jax.nn.gelu(rg_gate), scv_ref[3:4, :]).astype(BF16)
    sc_out = _rms(sc_b * uc, scv_ref[4:5, :]).astype(BF16)
    return _dot(rg_out, w_out_ref[0:D_RG, :]) + _dot(sc_out, w_out_ref[D_RG:, :])


def _shift_rows(x, d, fill):
    m = x.shape[0]
    if d % SUBLANES == 0:
        head = jnp.full((d, x.shape[1]), fill, x.dtype)
        return jnp.concatenate([head, x[:m - d]], axis=0)
    rolled = pltpu.roll(x, d, axis=0)
    row = lax.broadcasted_iota(jnp.int32, x.shape, 0)
    return jnp.where(row >= d, rolled, fill)


def _scan_rows(a, b, h0):
    m = a.shape[0]
    d = 1
    while d < m:
        a_sh = _shift_rows(a, d, 1.0)
        b_sh = _shift_rows(b, d, 0.0)
        b = a * b_sh + b
        a = a * a_sh
        d *= 2
    return a * h0 + b


def _mixer_prompt_kernel(x_ref, gmix_ref, w_in_ref, wg_ref, rgv_ref, scv_ref, w_out_ref, *rest,
                         n_riders):
    rider_src, rest = rest[:n_riders], rest[n_riders:]
    o_ref, conv_ref, h_ref, sc_ref = rest[:4]
    rider_dst, (rgx, usc, hcar) = rest[4:4 + n_riders], rest[4 + n_riders:]
    _round_riders(rider_src, rider_dst)
    t = pl.program_id(1)
    tm = x_ref.shape[0]

    @pl.when(t == 0)
    def _():
        rgx[0:SUBLANES, :] = jnp.zeros((SUBLANES, D_RG), F32)
        usc[0:SUBLANES, :] = jnp.zeros((SUBLANES, D_SC), F32)
        hcar[...] = jnp.zeros_like(hcar)

    x = x_ref[...]
    xn = _rms(x, gmix_ref[...]).astype(BF16)

    rg_x = _dot(xn, w_in_ref[:, 0:D_RG])
    rgx[SUBLANES:SUBLANES + tm, :] = rg_x

    w_cols = (slice(2 * D_RG + D_SC, 2 * D_RG + 2 * D_SC), slice(2 * D_RG + 2 * D_SC, D_IN),
              slice(2 * D_RG, 2 * D_RG + D_SC), slice(D_RG, 2 * D_RG))
    hs_parts, proj = [], []
    for q in range(N_GATE_BLOCKS):
        sl = slice(q * MXU_DIM, (q + 1) * MXU_DIM)
        xc = rgx[5:5 + tm, sl] * rgv_ref[0:1, sl]
        xc = xc + rgx[6:6 + tm, sl] * rgv_ref[1:2, sl]
        xc = xc + rgx[7:7 + tm, sl] * rgv_ref[2:3, sl]
        xc = xc + rg_x[:, sl] * rgv_ref[3:4, sl]
        xc = xc + rgv_ref[4:5, sl]
        a, b = _rg_gate_block(xc, wg_ref, rgv_ref, q)
        hs_parts.append(_scan_rows(a, b, hcar[0:1, sl]))
        proj.append(_dot(xn, w_in_ref[:, w_cols[q]]))
        if q == 1:
            sc_c, sc_x = proj
            u = sc_c * sc_x
            usc[SUBLANES:SUBLANES + tm, :] = u
            uc = usc[6:6 + tm, :] * scv_ref[0:1, :]
            uc = uc + usc[7:7 + tm, :] * scv_ref[1:2, :]
            uc = uc + u * scv_ref[2:3, :]
    hs = jnp.concatenate(hs_parts, axis=1)
    hcar[0:1, :] = hs[tm - 1:tm, :]
    sc_b, rg_gate = proj[2], proj[3]

    o_ref[...] = x + _mix_out(hs, rg_gate, sc_b, uc, scv_ref, w_out_ref)

    @pl.when(t == pl.num_programs(1) - 1)
    def _():
        conv_ref[0] = rgx[tm + 5:tm + 8, :]
        sc_ref[0] = usc[tm + 6:tm + 8, :]
        h_ref[0] = hs[tm - 1:tm, :]

    rgx[0:SUBLANES, :] = rgx[tm:tm + SUBLANES, :]
    usc[0:SUBLANES, :] = usc[tm:tm + SUBLANES, :]


def _mixer_weight_specs(l):
    return [
        _layer_resident((1, D_MODEL), l),
        _whole_resident((D_MODEL, D_IN)),
        _layer_resident((N_GATE_BLOCKS, MXU_DIM, 2 * MXU_DIM), l),
        _layer_resident((8, D_RG), l),
        _layer_resident((8, D_SC), l),
        _layer_resident((D_MODEL, D_MODEL), l),
    ]


def _mixer_prompt(x, l, pw, w_in, riders):
    nt = SEQ // TM_MIX
    r_in, r_out, r_shape = _rider_specs(riders, l, BATCH * nt, lambda b, t: b * nt + t)
    return pl.pallas_call(
        functools.partial(_mixer_prompt_kernel, n_riders=len(riders)),
        grid=(BATCH, nt),
        in_specs=[pl.BlockSpec((TM_MIX, D_MODEL), lambda b, t: (b * nt + t, 0))]
        + _mixer_weight_specs(l) + r_in,
        out_specs=[
            pl.BlockSpec((TM_MIX, D_MODEL), lambda b, t: (b * nt + t, 0)),
            pl.BlockSpec((1, RG_CONV_W - 1, D_RG), lambda b, t: (b, 0, 0)),
            pl.BlockSpec((1, 1, D_RG), lambda b, t: (b, 0, 0)),
            pl.BlockSpec((1, SC_CONV_W - 1, D_SC), lambda b, t: (b, 0, 0)),
        ] + r_out,
        out_shape=[
            jax.ShapeDtypeStruct((N_PROMPT, D_MODEL), F32),
            jax.ShapeDtypeStruct((BATCH, RG_CONV_W - 1, D_RG), F32),
            jax.ShapeDtypeStruct((BATCH, 1, D_RG), F32),
            jax.ShapeDtypeStruct((BATCH, SC_CONV_W - 1, D_SC), F32),
        ] + r_shape,
        scratch_shapes=[
            pltpu.VMEM((SUBLANES + TM_MIX, D_RG), F32),
            pltpu.VMEM((SUBLANES + TM_MIX, D_SC), F32),
            pltpu.VMEM((SUBLANES, D_RG), F32),
        ],
        compiler_params=_params(("arbitrary", "arbitrary")),
        name="mixer_prompt",
    )(x, pw["gmix"], w_in, pw["wg"], pw["rgv"], pw["scv"], pw["w_out"], *riders)


def _mixer_sample_kernel(x_ref, gmix_ref, w_in_ref, wg_ref, rgv_ref, scv_ref, w_out_ref,
                         conv_in_ref, h_in_ref, sc_in_ref,
                         o_ref, conv_ref, h_ref, sc_ref, xcs, hss, ucs):
    nb = DEC_BATCH
    x = x_ref[...]
    xn = _rms(x, gmix_ref[...]).astype(BF16)

    rg_x = _dot(xn, w_in_ref[:, 0:D_RG])
    seq = [conv_in_ref[k] for k in range(RG_CONV_W - 1)]
    seq += [rg_x[t * nb:(t + 1) * nb, :] for t in range(DEC_SEQ)]
    for t in range(DEC_SEQ):
        xc_t = seq[t] * rgv_ref[0:1, :]
        for k in range(1, RG_CONV_W):
            xc_t = xc_t + seq[t + k] * rgv_ref[k:k + 1, :]
        xcs[t * nb:(t + 1) * nb, :] = xc_t + rgv_ref[4:5, :]
    for k in range(RG_CONV_W - 1):
        conv_ref[k] = seq[DEC_SEQ + k]

    a, b = _rg_gate_inputs(xcs[...], wg_ref, rgv_ref)
    h = h_in_ref[...]
    for t in range(DEC_SEQ):
        h = a[t * nb:(t + 1) * nb, :] * h + b[t * nb:(t + 1) * nb, :]
        hss[t * nb:(t + 1) * nb, :] = h
    h_ref[...] = h

    rg_gate = _dot(xn, w_in_ref[:, D_RG:2 * D_RG])
    sc_b = _dot(xn, w_in_ref[:, 2 * D_RG:2 * D_RG + D_SC])
    sc_c = _dot(xn, w_in_ref[:, 2 * D_RG + D_SC:2 * D_RG + 2 * D_SC])
    sc_x = _dot(xn, w_in_ref[:, 2 * D_RG + 2 * D_SC:])
    u = sc_c * sc_x
    useq = [sc_in_ref[:, k * D_SC:(k + 1) * D_SC] for k in range(SC_CONV_W - 1)]
    useq += [u[t * nb:(t + 1) * nb, :] for t in range(DEC_SEQ)]
    for t in range(DEC_SEQ):
        uc_t = useq[t] * scv_ref[0:1, :]
        for k in range(1, SC_CONV_W):
            uc_t = uc_t + useq[t + k] * scv_ref[k:k + 1, :]
        ucs[t * nb:(t + 1) * nb, :] = uc_t
    for k in range(SC_CONV_W - 1):
        sc_ref[:, k * D_SC:(k + 1) * D_SC] = useq[DEC_SEQ + k]

    o_ref[...] = x + _mix_out(hss[...], rg_gate, sc_b, ucs[...], scv_ref, w_out_ref)


def _mixer_sample(x, l, pw, w_in, conv_in, h_in, sc_in):
    conv_shape = (RG_CONV_W - 1, DEC_BATCH, D_RG)
    sc_w = (SC_CONV_W - 1) * D_SC
    return pl.pallas_call(
        _mixer_sample_kernel,
        grid=(1,),
        in_specs=[
            pl.BlockSpec((N_SAMPLE, D_MODEL), lambda i: (0, 0)),
        ] + _mixer_weight_specs(l) + [
            _layer_resident(conv_shape, l),
            _layer_resident((DEC_BATCH, D_RG), l),
            _layer_resident((DEC_BATCH, sc_w), l),
        ],
        out_specs=[
            pl.BlockSpec((N_SAMPLE, D_MODEL), lambda i: (0, 0)),
            pl.BlockSpec(conv_shape, lambda i: (0, 0, 0)),
            pl.BlockSpec((DEC_BATCH, D_RG), lambda i: (0, 0)),
            pl.BlockSpec((DEC_BATCH, sc_w), lambda i: (0, 0)),
        ],
        out_shape=[
            jax.ShapeDtypeStruct((N_SAMPLE, D_MODEL), F32),
            jax.ShapeDtypeStruct(conv_shape, F32),
            jax.ShapeDtypeStruct((DEC_BATCH, D_RG), F32),
            jax.ShapeDtypeStruct((DEC_BATCH, sc_w), F32),
        ],
        scratch_shapes=[
            pltpu.VMEM((N_SAMPLE, D_RG), F32),
            pltpu.VMEM((N_SAMPLE, D_RG), F32),
            pltpu.VMEM((N_SAMPLE, D_SC), F32),
        ],
        compiler_params=_params(("arbitrary",)),
        name="mixer_sample",
    )(x, pw["gmix"], w_in, pw["wg"], pw["rgv"], pw["scv"], pw["w_out"],
      conv_in, h_in, sc_in)


def _mem_kv_kernel(m_ref, g_ref, wk_ref, wv_ref, *rest, n_riders):
    rider_src, rest = rest[:n_riders], rest[n_riders:]
    k_ref, v_ref = rest[:2]
    rider_dst, (mn_s,) = rest[2:2 + n_riders], rest[2 + n_riders:]
    _round_riders(rider_src, rider_dst)

    @pl.when(pl.program_id(1) == 0)
    def _():
        mn_s[...] = _rms(m_ref[...], g_ref[...]).astype(BF16)

    mn = mn_s[...]
    k_ref[...] = _dot(mn, wk_ref[...].astype(BF16))
    v_ref[...] = _dot(mn, wv_ref[...].astype(BF16))


def _mem_kv(mem, gmem, w_k, w_v, riders, tn=512):
    rows = BATCH * N_MEM
    nj = D_MODEL // tn
    w_spec = pl.BlockSpec((None, D_MODEL, tn), lambda l, j: (l, 0, j))
    o_spec = pl.BlockSpec((None, rows, tn), lambda l, j: (l, 0, j))
    shape = jax.ShapeDtypeStruct((DEPTH, rows, D_MODEL), F32)
    r_in, r_out, r_shape = _rider_specs(riders, 0, DEPTH * nj, lambda l, j: l * nj + j)
    return pl.pallas_call(
        functools.partial(_mem_kv_kernel, n_riders=len(riders)),
        grid=(DEPTH, nj),
        in_specs=[
            pl.BlockSpec((rows, D_MODEL), lambda l, j: (0, 0), pipeline_mode=pl.Buffered(1)),
            pl.BlockSpec((None, 1, D_MODEL), lambda l, j: (l, 0, 0)),
            w_spec, w_spec,
        ] + r_in,
        out_specs=[o_spec, o_spec] + r_out,
        out_shape=[shape, shape] + r_shape,
        scratch_shapes=[pltpu.VMEM((rows, D_MODEL), BF16)],
        compiler_params=_params(("arbitrary",) * 2),
        name="mem_kv",
    )(mem, gmem, w_k, w_v, *riders)


def _q_sample_kernel(x_ref, g_ref, w_ref, o_ref):
    xn = _rms(x_ref[...], g_ref[...]).astype(BF16)
    o_ref[...] = _dot(xn, w_ref[...])


def _q_sample(x, l, pw, wq, tn=1024):
    return pl.pallas_call(
        _q_sample_kernel,
        grid=(D_MODEL // tn,),
        in_specs=[
            pl.BlockSpec((N_SAMPLE, D_MODEL), lambda j: (0, 0)),
            _layer_block((1, D_MODEL), l),
            pl.BlockSpec((D_MODEL, tn), lambda j: (0, j)),
        ],
        out_specs=pl.BlockSpec((N_SAMPLE, tn), lambda j: (0, j)),
        out_shape=jax.ShapeDtypeStruct((N_SAMPLE, D_MODEL), F32),
        compiler_params=_params(("arbitrary",)),
        name="q_sample",
    )(x, pw["gxa"], wq)


def _oproj_sample_kernel(a_ref, w_ref, res_ref, o_ref):
    o_ref[...] = res_ref[...] + _dot(a_ref[...].astype(BF16), w_ref[...])


def _oproj_sample(a, wo, res, tn=1024):
    return pl.pallas_call(
        _oproj_sample_kernel,
        grid=(D_MODEL // tn,),
        in_specs=[
            pl.BlockSpec((N_SAMPLE, D_MODEL), lambda j: (0, 0)),
            pl.BlockSpec((D_MODEL, tn), lambda j: (0, j)),
            pl.BlockSpec((N_SAMPLE, tn), lambda j: (0, j)),
        ],
        out_specs=pl.BlockSpec((N_SAMPLE, tn), lambda j: (0, j)),
        out_shape=jax.ShapeDtypeStruct((N_SAMPLE, D_MODEL), F32),
        compiler_params=_params(("arbitrary",)),
        name="oproj_sample",
    )(a, wo, res)


def _softmax_rows(s):
    m = jnp.max(s, axis=-1, keepdims=True)
    e = jnp.exp(s - m)
    return e / jnp.sum(e, axis=-1, keepdims=True)


def _attn_prompt_kernel(x_ref, g_ref, wq_ref, k_ref, v_ref, wo_ref, *rest, n_riders):
    rider_src, o_ref, rider_dst = rest[:n_riders], rest[n_riders], rest[n_riders + 1:]
    _round_riders(rider_src, rider_dst)
    x = x_ref[...]
    xn = _rms(x, g_ref[...]).astype(BF16)
    q = _dot(xn, wq_ref[...]).astype(BF16)
    k = k_ref[...].astype(BF16)
    v = v_ref[...].astype(BF16)
    outs = []
    for h in range(XA_HEADS):
        sl = slice(h * XA_HEAD_DIM, (h + 1) * XA_HEAD_DIM)
        s = lax.dot_general(q[:, sl], k[:, sl], (((1,), (1,)), ((), ())),
                            preferred_element_type=F32) / math.sqrt(XA_HEAD_DIM)
        p = _softmax_rows(s).astype(BF16)
        outs.append(_dot(p, v[:, sl]).astype(BF16))
    o = jnp.concatenate(outs, axis=1)
    o_ref[...] = x + _dot(o, wo_ref[...])


def _attn_prompt(x, l, pw, wq, wo, k, v, riders, rider_layer):
    nt = SEQ // TM_ATT
    kv_spec = pl.BlockSpec((None, N_MEM, D_MODEL), lambda b, t: (l, b, 0))
    r_in, r_out, r_shape = _rider_specs(riders, rider_layer, BATCH * nt, lambda b, t: b * nt + t)
    return pl.pallas_call(
        functools.partial(_attn_prompt_kernel, n_riders=len(riders)),
        grid=(BATCH, nt),
        in_specs=[
            pl.BlockSpec((TM_ATT, D_MODEL), lambda b, t: (b * nt + t, 0)),
            _layer_resident((1, D_MODEL), l),
            _whole_resident((D_MODEL, D_MODEL)),
            kv_spec, kv_spec,
            _whole_resident((D_MODEL, D_MODEL)),
        ] + r_in,
        out_specs=[pl.BlockSpec((TM_ATT, D_MODEL), lambda b, t: (b * nt + t, 0))] + r_out,
        out_shape=[jax.ShapeDtypeStruct((N_PROMPT, D_MODEL), F32)] + r_shape,
        compiler_params=_params(("arbitrary", "arbitrary")),
        name="attn_prompt",
    )(x, pw["gxa"], wq, k, v, wo, *riders)


def _head_matrix(mem_ref, stage_ref, b, h):
    n = N_MEM * HD_CHUNKS
    stage_ref[...] = mem_ref[pl.ds(b * n * XA_HEADS + h, n, stride=XA_HEADS), :]
    chunks = [stage_ref[pl.ds(c, N_MEM, stride=HD_CHUNKS), :] for c in range(HD_CHUNKS)]
    return jnp.concatenate(chunks, axis=1).astype(BF16)


def _attn_sample_kernel(q_ref, k_ref, v_ref, o_ref, stage_ref):
    rows = DEC_SEQ * ATT_BB
    keys = ATT_BB * N_MEM
    row_b = lax.broadcasted_iota(jnp.int32, (rows, keys), 0) // DEC_SEQ
    key_b = lax.broadcasted_iota(jnp.int32, (rows, keys), 1) // N_MEM
    own = row_b == key_b
    outs = []
    for h in range(XA_HEADS):
        q = q_ref[:, h * XA_HEAD_DIM:(h + 1) * XA_HEAD_DIM].astype(BF16)
        k = jnp.concatenate([_head_matrix(k_ref, stage_ref, b, h) for b in range(ATT_BB)], axis=0)
        v = jnp.concatenate([_head_matrix(v_ref, stage_ref, b, h) for b in range(ATT_BB)], axis=0)
        s = lax.dot_general(q, k, (((1,), (1,)), ((), ())),
                            preferred_element_type=F32) / math.sqrt(XA_HEAD_DIM)
        s = jnp.where(own, s, -jnp.inf)
        m = jnp.max(s, axis=-1, keepdims=True)
        e = jnp.where(own, jnp.exp(s - m), 0.0)
        p = (e / jnp.sum(e, axis=-1, keepdims=True)).astype(BF16)
        outs.append(_dot(p, v))
    o_ref[...] = jnp.concatenate(outs, axis=1)


def _lane_pieces(cache):
    c = cache.reshape(DEPTH, DEC_BATCH, N_MEM, XA_HEADS, HD_CHUNKS, LANES)
    c = jnp.transpose(c, (0, 1, 2, 4, 3, 5))
    return c.reshape(DEPTH * DEC_BATCH * N_MEM * HD_CHUNKS * XA_HEADS, LANES)


def _attn_sample(q, l, k, v):
    rows = ATT_BB * DEC_SEQ
    blk = ATT_BB * N_MEM * HD_CHUNKS * XA_HEADS
    nb = DEC_BATCH // ATT_BB
    kv_spec = pl.BlockSpec((blk, LANES), lambda i: (l * nb + i, 0))
    return pl.pallas_call(
        _attn_sample_kernel,
        grid=(nb,),
        in_specs=[pl.BlockSpec((rows, D_MODEL), lambda i: (i, 0)), kv_spec, kv_spec],
        out_specs=pl.BlockSpec((rows, D_MODEL), lambda i: (i, 0)),
        out_shape=jax.ShapeDtypeStruct((N_SAMPLE, D_MODEL), F32),
        scratch_shapes=[pltpu.VMEM((N_MEM * HD_CHUNKS, LANES), F32)],
        compiler_params=_params(("arbitrary",)),
        name="attn_sample",
    )(q, k, v)


def _route(logits):
    lane = lax.broadcasted_iota(jnp.int32, logits.shape, 1)
    neg = -jnp.inf
    is_g = lane < N_GROUPS
    lg = jnp.where(is_g, logits, neg)
    mg = jnp.max(lg, axis=-1, keepdims=True)
    eg = jnp.where(is_g, jnp.exp(lg - mg), 0.0)
    p_g = eg / jnp.sum(eg, axis=-1, keepdims=True)
    pg_sel = jnp.max(p_g, axis=-1, keepdims=True)
    g_sel = jnp.min(jnp.where(p_g == pg_sel, lane, LANES), axis=-1, keepdims=True)

    lo = N_GROUPS + g_sel * EXPERTS_PER_GROUP
    in_grp = (lane >= lo) & (lane < lo + EXPERTS_PER_GROUP)
    le = jnp.where(in_grp, logits, neg)
    me = jnp.max(le, axis=-1, keepdims=True)
    ee = jnp.where(in_grp, jnp.exp(le - me), 0.0)
    p_e = ee / jnp.sum(ee, axis=-1, keepdims=True)

    p1 = jnp.max(p_e, axis=-1, keepdims=True)
    i1 = jnp.min(jnp.where(in_grp & (p_e == p1), lane, LANES), axis=-1, keepdims=True)
    rest = in_grp & (lane != i1)
    p2 = jnp.max(jnp.where(rest, p_e, neg), axis=-1, keepdims=True)
    i2 = jnp.min(jnp.where(rest & (p_e == p2), lane, LANES), axis=-1, keepdims=True)
    tot = p1 + p2
    w1 = p1 / tot * pg_sel
    w2 = p2 / tot * pg_sel
    gates = jnp.where(lane == i1, w1, 0.0) + jnp.where(lane == i2, w2, 0.0)

    local = jnp.zeros_like(gates)
    for g in range(N_GROUPS):
        start = N_GROUPS + g * EXPERTS_PER_GROUP
        local = local + jnp.where(lane < EXPERTS_PER_GROUP,
                                  pltpu.roll(gates, LANES - start, axis=1), 0.0)

    ja = jnp.minimum(i1, i2) - lo
    jb = jnp.maximum(i1, i2) - lo
    code = ja * EXPERTS_PER_GROUP + jb
    pair = jnp.full_like(code, N_PAIRS - 1)
    for p in range(N_PAIRS - 1):
        lo_j, hi_j = min(PAIR_A[p], PAIR_B[p]), max(PAIR_A[p], PAIR_B[p])
        pair = jnp.where(code == lo_j * EXPERTS_PER_GROUP + hi_j, p, pair)
    bucket = (g_sel * N_PAIRS + pair).astype(F32)
    return local + jnp.where(lane == BUCKET_LANE, bucket, 0.0)


def _router_kernel(x_ref, g_ref, whi_ref, wlo_ref, br_ref, o_ref):
    xn = _rms(x_ref[...], g_ref[...])
    hi = xn.astype(BF16)
    lo = (xn - hi.astype(F32)).astype(BF16)
    w_hi = whi_ref[...]
    logits = _dot(hi, w_hi) + _dot(lo, w_hi) + _dot(hi, wlo_ref[...]) + br_ref[...]
    o_ref[...] = _route(logits)


def _router_riders_kernel(x_ref, g_ref, whi_ref, wlo_ref, br_ref, *rest, n_riders):
    rider_src, o_ref, rider_dst = rest[:n_riders], rest[n_riders], rest[n_riders + 1:]
    _round_riders(rider_src, rider_dst)
    _router_kernel(x_ref, g_ref, whi_ref, wlo_ref, br_ref, o_ref)


def _router_specs(l):
    return [
        pl.BlockSpec((TM_ROUTE, D_MODEL), lambda i: (i, 0)),
        _layer_block((1, D_MODEL), l),
        _layer_block((D_MODEL, LANES), l),
        _layer_block((D_MODEL, LANES), l),
        _layer_block((1, LANES), l),
    ]


def _router(x, l, pw):
    n_rows = x.shape[0]
    return pl.pallas_call(
        _router_kernel,
        grid=(n_rows // TM_ROUTE,),
        in_specs=_router_specs(l),
        out_specs=pl.BlockSpec((TM_ROUTE, LANES), lambda i: (i, 0)),
        out_shape=jax.ShapeDtypeStruct((n_rows, LANES), F32),
        compiler_params=_params(("arbitrary",)),
        name="router",
    )(x, pw["gmoe"], pw["wr_hi"], pw["wr_lo"], pw["br"])


def _router_prompt(x, l, pw, riders):
    n_steps = N_PROMPT // TM_ROUTE
    r_in, r_out, r_shape = _rider_specs(riders, l, n_steps, lambda i: i)
    return pl.pallas_call(
        functools.partial(_router_riders_kernel, n_riders=len(riders)),
        grid=(n_steps,),
        in_specs=_router_specs(l) + r_in,
        out_specs=[pl.BlockSpec((TM_ROUTE, LANES), lambda i: (i, 0))] + r_out,
        out_shape=[jax.ShapeDtypeStruct((N_PROMPT, LANES), F32)] + r_shape,
        compiler_params=_params(("arbitrary",)),
        name="router_prompt",
    )(x, pw["gmoe"], pw["wr_hi"], pw["wr_lo"], pw["br"], *riders)


def _dispatch_tables(rinfo):
    i32 = jnp.int32
    bucket = rinfo[:, BUCKET_LANE].astype(i32)
    bucket_ids = jnp.arange(N_BUCKETS, dtype=i32)
    onehot = (bucket[:, None] == bucket_ids[None, :]).astype(i32)
    csum = jnp.cumsum(onehot, axis=0)
    counts = csum[-1]
    tiles_per = (counts + TG - 1) // TG
    tile_end = jnp.cumsum(tiles_per)
    tile_start = tile_end - tiles_per
    slot = jnp.sum(onehot * (tile_start[None, :] * TG + csum - 1), axis=1)

    tok1 = jnp.arange(1, N_TOK + 1, dtype=i32).astype(F32)
    payload = jnp.concatenate([tok1[:, None], rinfo[:, :EXPERTS_PER_GROUP]], axis=1)
    slots = jnp.zeros((N_SLOTS, 1 + EXPERTS_PER_GROUP), F32).at[slot].set(
        payload, unique_indices=True)
    slot_tok = slots[:, 0].astype(i32) - 1

    tile = jnp.arange(N_TILES, dtype=i32)
    n_valid = tile_end[-1]
    tq = jnp.minimum(tile, n_valid - 1)
    tile_bucket = jnp.sum((tile_end[None, :] <= tq[:, None]).astype(i32), axis=1)
    tile_oh = (tile_bucket[:, None] == bucket_ids[None, :]).astype(i32)
    in_bucket = tile - jnp.sum(tile_oh * tile_start[None, :], axis=1)
    tile_cnt = jnp.clip(jnp.sum(tile_oh * counts[None, :], axis=1) - in_bucket * TG, 0, TG)
    tile_cnt = jnp.where(tile < n_valid, tile_cnt, 0)
    pair_a = jnp.asarray([(b // N_PAIRS) * EXPERTS_PER_GROUP + PAIR_A[b % N_PAIRS]
                          for b in range(N_BUCKETS)], i32)
    pair_b = jnp.asarray([(b // N_PAIRS) * EXPERTS_PER_GROUP + PAIR_B[b % N_PAIRS]
                          for b in range(N_BUCKETS)], i32)
    tile_ea = jnp.sum(tile_oh * pair_a[None, :], axis=1)
    tile_eb = jnp.sum(tile_oh * pair_b[None, :], axis=1)
    is_prompt = (slot_tok >= 0) & (slot_tok < N_PROMPT)
    tile_np = jnp.sum(is_prompt.reshape(N_TILES, TG).astype(i32), axis=1)
    return tile_ea, tile_eb, tile_cnt, tile_np, slot_tok.reshape(N_TILES, 1, TG), slots


def _experts_kernel(ea_ref, eb_ref, cnt_ref, np_ref,
                    xp_hbm, xs_hbm, tokp_ref, tokc_ref, tokn_ref, gates_ref, g_ref,
                    wga_ref, wua_ref, wda_ref, wgb_ref, wub_ref, wdb_ref, gfin_ref,
                    op_hbm, os_hbm, xbuf, obuf, gsem, ssem, *, final_norm, sample_batch_major):
    t = pl.program_id(0)
    nt = pl.num_programs(0)
    buf = lax.rem(t, 2)
    t_prev = jnp.maximum(t - 1, 0)
    t_next = jnp.minimum(t + 1, nt - 1)
    cnt = cnt_ref[t]
    has_next = (t + 1 < nt) & (cnt_ref[t_next] > 0)

    def sample_dst(r):
        if not sample_batch_major:
            return r
        return (r % DEC_BATCH) * DEC_SEQ + r // DEC_BATCH

    def row_slot(r):
        return lax.shift_right_logical(r, 3), lax.bitwise_and(r, SUBLANES - 1)

    def gather_p(tok_ref, b, r, g, s):
        return pltpu.make_async_copy(
            xp_hbm.at[pl.ds(tok_ref[0, r], 1)], xbuf.at[b, g, pl.ds(s, 1)], gsem.at[b])

    def gather_s(tok_ref, b, r, g, s):
        return pltpu.make_async_copy(
            xs_hbm.at[pl.ds(tok_ref[0, r] - N_PROMPT, 1)], xbuf.at[b, g, pl.ds(s, 1)], gsem.at[b])

    def scatter_p(tok_ref, b, r, g, s):
        return pltpu.make_async_copy(
            obuf.at[b, g, pl.ds(s, 1)], op_hbm.at[pl.ds(tok_ref[0, r], 1)], ssem.at[b])

    def scatter_s(tok_ref, b, r, g, s):
        dst = sample_dst(tok_ref[0, r] - N_PROMPT)
        return pltpu.make_async_copy(
            obuf.at[b, g, pl.ds(s, 1)], os_hbm.at[pl.ds(dst, 1)], ssem.at[b])

    def for_range(lo, hi, make_copy, op):
        full = (hi - lo) // SUBLANES
        aligned = isinstance(lo, int) and lo == 0

        def group(i, c):
            for j in range(SUBLANES):
                r = lo + i * SUBLANES + j
                op(make_copy(r, i, j) if aligned else make_copy(r, *row_slot(r)))
            return c

        def single(r, c):
            op(make_copy(r, *row_slot(r)))
            return c
        lax.fori_loop(0, full, group, 0)
        lax.fori_loop(lo + full * SUBLANES, hi, single, 0)

    def gather(tile, tok_ref, b, op):
        for_range(0, np_ref[tile], functools.partial(gather_p, tok_ref, b), op)
        for_range(np_ref[tile], cnt_ref[tile], functools.partial(gather_s, tok_ref, b), op)

    def scatter(tile, tok_ref, b, op):
        for_range(0, np_ref[tile], functools.partial(scatter_p, tok_ref, b), op)
        for_range(np_ref[tile], cnt_ref[tile], functools.partial(scatter_s, tok_ref, b), op)

    def start(copy):
        copy.start()

    def wait(copy):
        copy.wait()

    @pl.when(t == 0)
    def _():
        xbuf[...] = jnp.zeros_like(xbuf)
        gather(t, tokc_ref, 0, start)

    @pl.when(has_next)
    def _():
        gather(t_next, tokn_ref, 1 - buf, start)

    @pl.when(cnt > 0)
    def _():
        gather(t, tokc_ref, buf, wait)

        ja = ea_ref[t] % EXPERTS_PER_GROUP + SLOT_GATE_LANE
        jb = eb_ref[t] % EXPERTS_PER_GROUP + SLOT_GATE_LANE
        n_chunks = (cnt + ROW_CHUNK - 1) // ROW_CHUNK

        def compute(m):
            x = xbuf[buf, 0:m // SUBLANES].reshape(m, D_MODEL)
            xn = _rms(x, g_ref[...]).astype(BF16)
            gates = gates_ref[0:m, :]
            lane = lax.broadcasted_iota(jnp.int32, gates.shape, 1)
            ga = jnp.sum(jnp.where(lane == ja, gates, 0.0), axis=-1, keepdims=True)
            gb = jnp.sum(jnp.where(lane == jb, gates, 0.0), axis=-1, keepdims=True)
            act_a = (jax.nn.silu(_dot(xn, wga_ref[...])) * _dot(xn, wua_ref[...]) * ga).astype(BF16)
            act_b = (jax.nn.silu(_dot(xn, wgb_ref[...])) * _dot(xn, wub_ref[...]) * gb).astype(BF16)
            y = x + (_dot(act_a, wda_ref[...]) + _dot(act_b, wdb_ref[...]))
            if final_norm:
                y = _rms(y, gfin_ref[...])
            obuf[buf, 0:m // SUBLANES] = y.reshape(m // SUBLANES, SUBLANES, D_MODEL)

        for k in range(1, TG // ROW_CHUNK + 1):
            pl.when(n_chunks == k)(functools.partial(compute, k * ROW_CHUNK))

        @pl.when(t > 0)
        def _():
            scatter(t_prev, tokp_ref, 1 - buf, wait)

        scatter(t, tokc_ref, buf, start)

        @pl.when(jnp.logical_not(has_next))
        def _():
            scatter(t, tokc_ref, buf, wait)


def _experts(xp, xs, l, pw, expert_w, gfin, tables, final_norm):
    tile_ea, tile_eb, tile_cnt, tile_np, slot_tok, gates = tables
    w_gate, w_up, w_down = expert_w

    def tok_spec(shift):
        return pl.BlockSpec(
            (None, 1, TG),
            lambda t, *_: (jnp.clip(t + shift, 0, N_TILES - 1), 0, 0),
            memory_space=pltpu.SMEM)

    def w_spec(shape, which):
        return pl.BlockSpec(
            (None,) + shape,
            lambda t, ea, eb, *_: ((ea, eb)[which][t], 0, 0))

    any_spec = pl.BlockSpec(memory_space=pl.ANY)
    grid_spec = pltpu.PrefetchScalarGridSpec(
        num_scalar_prefetch=4,
        grid=(N_TILES,),
        in_specs=[
            any_spec, any_spec,
            tok_spec(-1), tok_spec(0), tok_spec(1),
            pl.BlockSpec((TG, SLOT_GATE_LANE + EXPERTS_PER_GROUP), lambda t, *_: (t, 0)),
            pl.BlockSpec((None, 1, D_MODEL), lambda t, *_: (l, 0, 0)),
            w_spec((D_MODEL, D_FF), 0), w_spec((D_MODEL, D_FF), 0), w_spec((D_FF, D_MODEL), 0),
            w_spec((D_MODEL, D_FF), 1), w_spec((D_MODEL, D_FF), 1), w_spec((D_FF, D_MODEL), 1),
            pl.BlockSpec((1, D_MODEL), lambda t, *_: (0, 0)),
        ],
        out_specs=[any_spec, any_spec],
        scratch_shapes=[
            pltpu.VMEM((2, TG // SUBLANES, SUBLANES, D_MODEL), F32),
            pltpu.VMEM((2, TG // SUBLANES, SUBLANES, D_MODEL), F32),
            pltpu.SemaphoreType.DMA((2,)),
            pltpu.SemaphoreType.DMA((2,)),
        ],
    )
    return pl.pallas_call(
        functools.partial(_experts_kernel, final_norm=final_norm, sample_batch_major=final_norm),
        grid_spec=grid_spec,
        out_shape=[jax.ShapeDtypeStruct((N_PROMPT, D_MODEL), F32),
                   jax.ShapeDtypeStruct((N_SAMPLE, D_MODEL), F32)],
        compiler_params=_params(("arbitrary",)),
        name="experts",
    )(tile_ea, tile_eb, tile_cnt, tile_np, xp, xs, slot_tok, slot_tok, slot_tok, gates,
      pw["gmoe"], w_gate, w_up, w_down, w_gate, w_up, w_down, gfin)


def _block_diag_gates(w_a, w_x):
    def diag(w):
        w = w.reshape(DEPTH, N_GATE_BLOCKS, HEADS_PER_GATE_BLOCK, RG_HEAD_DIM, RG_HEAD_DIM)
        eye = jnp.eye(HEADS_PER_GATE_BLOCK, dtype=w.dtype)
        full = jnp.einsum("lqhij,hk->lqhikj", w, eye)
        return full.reshape(DEPTH, N_GATE_BLOCKS, MXU_DIM, MXU_DIM)
    return jnp.concatenate([diag(w_a), diag(w_x)], axis=-1).astype(BF16)


def _to_batch_major(a):
    return jnp.transpose(a.reshape(DEC_SEQ, DEC_BATCH, -1), (1, 0, 2)).reshape(N_SAMPLE, -1)


def _to_time_major(a):
    return jnp.transpose(a.reshape(DEC_BATCH, DEC_SEQ, -1), (1, 0, 2)).reshape(N_SAMPLE, -1)


def _stack_rows(rows, n):
    width = rows[0].shape[-1]
    rows = [r.reshape(DEPTH, -1, width) for r in rows]
    have = sum(r.shape[1] for r in rows)
    if have < n:
        rows.append(jnp.zeros((DEPTH, n - have, width), F32))
    return jnp.concatenate(rows, axis=1)


def kernel(x_prompt, x_sample, state_rglru_h, state_rglru_conv, state_sconv, cache_mem_k,
           cache_mem_v, mem_prompt, norm_mix, w_in, rg_conv_w, rg_conv_b, rg_w_a, rg_b_a,
           rg_w_x, rg_b_x, rg_lambda, sc_conv_w, norm_rg_out, norm_sc_out, w_out, norm_xattn,
           norm_mem, xa_w_q, xa_w_k, xa_w_v, xa_w_o, norm_moe, router_group_w, router_group_b,
           router_expert_w, router_expert_b, expert_w_gate, expert_w_up, expert_w_down,
           norm_final):
    wr = jnp.concatenate([router_group_w, router_expert_w], axis=2)
    wr = jnp.pad(wr, ((0, 0), (0, 0), (0, LANES - wr.shape[2])))
    wr_hi = wr.astype(BF16)
    wr_lo = (wr - wr_hi.astype(F32)).astype(BF16)
    br = jnp.concatenate([router_group_b, router_expert_b], axis=1)
    br = jnp.pad(br, ((0, 0), (0, LANES - br.shape[1]))).reshape(DEPTH, 1, LANES)
    pw = dict(
        gmix=norm_mix.reshape(DEPTH, 1, D_MODEL),
        wg=_block_diag_gates(rg_w_a, rg_w_x),
        rgv=_stack_rows([rg_conv_w, rg_conv_b, rg_b_a, rg_b_x, rg_lambda], 8),
        scv=_stack_rows([sc_conv_w, norm_rg_out, norm_sc_out], 8),
        gxa=norm_xattn.reshape(DEPTH, 1, D_MODEL),
        gmem=norm_mem.reshape(DEPTH, 1, D_MODEL),
        gmoe=norm_moe.reshape(DEPTH, 1, D_MODEL),
        wr_hi=wr_hi, wr_lo=wr_lo, br=br,
    )
    gfin = norm_final.reshape(1, D_MODEL)

    conv_in = jnp.transpose(state_rglru_conv, (0, 2, 1, 3))
    sc_in = state_sconv.reshape(DEPTH, DEC_BATCH, (SC_CONV_W - 1) * D_SC)
    w_gate_rows = expert_w_gate.reshape(DEPTH, N_EXPERTS * D_MODEL, D_FF)
    w_up_rows = expert_w_up.reshape(DEPTH, N_EXPERTS * D_MODEL, D_FF)
    w_down_rows = expert_w_down.reshape(DEPTH, N_EXPERTS * D_FF, D_MODEL)
    w_in_l = w_in[0].astype(BF16)
    cache_k = _lane_pieces(cache_mem_k)
    cache_v = _lane_pieces(cache_mem_v)

    mem = mem_prompt.reshape(BATCH * N_MEM, D_MODEL)
    p_k, p_v, w_out_bf16 = _mem_kv(mem, pw["gmem"], xa_w_k, xa_w_v,
                                   [w_out.reshape(1, DEPTH * D_MODEL, D_MODEL)])
    pw["w_out"] = w_out_bf16.reshape(DEPTH, D_MODEL, D_MODEL)

    xp = x_prompt.reshape(N_PROMPT, D_MODEL)
    xs = jnp.transpose(x_sample, (1, 0, 2)).reshape(N_SAMPLE, D_MODEL)

    p_conv, p_h, p_sc, s_conv, s_h, s_sc = [], [], [], [], [], []
    for l in range(DEPTH):
        xp, c, hh, sc, w_down_l, wq_l, wo_l = _mixer_prompt(
            xp, l, pw, w_in_l, [w_down_rows, xa_w_q, xa_w_o])
        xs, cs, hs, scs = _mixer_sample(xs, l, pw, w_in_l, conv_in, state_rglru_h, sc_in)
        p_conv.append(c)
        p_h.append(hh.reshape(BATCH, D_RG))
        p_sc.append(sc)
        s_conv.append(cs)
        s_h.append(hs)
        s_sc.append(scs.reshape(DEC_BATCH, SC_CONV_W - 1, D_SC))

        next_w_in = [w_in] if l + 1 < DEPTH else []
        xp, *w_in_next = _attn_prompt(xp, l, pw, wq_l, wo_l, p_k, p_v, next_w_in, l + 1)
        q_s = _to_batch_major(_q_sample(xs, l, pw, wq_l))
        o_s = _attn_sample(q_s, l, cache_k, cache_v)
        xs = _oproj_sample(_to_time_major(o_s), wo_l, xs)
        if w_in_next:
            w_in_l = w_in_next[0]

        rinfo_p, w_gate_l, w_up_l = _router_prompt(xp, l, pw, [w_gate_rows, w_up_rows])
        rinfo = jnp.concatenate([rinfo_p, _router(xs, l, pw)], axis=0)
        expert_w = (w_gate_l.reshape(N_EXPERTS, D_MODEL, D_FF),
                    w_up_l.reshape(N_EXPERTS, D_MODEL, D_FF),
                    w_down_l.reshape(N_EXPERTS, D_FF, D_MODEL))
        xp, xs = _experts(xp, xs, l, pw, expert_w, gfin, _dispatch_tables(rinfo),
                          final_norm=(l == DEPTH - 1))

    y_prompt = xp.reshape(BATCH, SEQ, D_MODEL)
    y_sample = xs.reshape(DEC_BATCH, DEC_SEQ, D_MODEL)
    mem_shape = (DEPTH, BATCH, N_MEM, XA_HEADS, XA_HEAD_DIM)
    return (y_prompt, y_sample,
            jnp.stack(p_h), jnp.stack(p_conv), jnp.stack(p_sc),
            p_k.reshape(mem_shape), p_v.reshape(mem_shape),
            jnp.stack(s_h), jnp.transpose(jnp.stack(s_conv), (0, 2, 1, 3)), jnp.stack(s_sc))
```
